```python
import math
import jax
import jax.numpy as jnp
from jax import lax
import numpy as np

D_MODEL = 1024
BATCH = 2
SEQ = 8192
DEPTH = 4
DEC_BATCH = 128
DEC_SEQ = 1
PAST_LEN = 8192
PAGE_SIZE = 128

N_MIXERS = 3
N_MLA_LAYERS = len(range(0, DEPTH, N_MIXERS))
N_SWA_LAYERS = len(range(1, DEPTH, N_MIXERS))
N_SB_LAYERS = len(range(2, DEPTH, N_MIXERS))

NORM_EPS = 1e-6
ROPE_THETA = 10000.0
Q_BLOCK = 128

MLA_HEADS = 8
MLA_NOPE = 128
MLA_ROPE = 64
MLA_V = 128
MLA_Q_RANK = 512
MLA_KV_RANK = 256

SWA_HEADS = 16
SWA_KV_HEADS = 4
SWA_HEAD_DIM = 64
WINDOW = 128

SB_HEADS = 8
SB_KV_HEADS = 4
SB_HEAD_DIM = 128

N_GROUPS = 4
EXPERTS_PER_GROUP = 8
N_EXPERTS = N_GROUPS * EXPERTS_PER_GROUP
TOP_K_FINE = 2
D_EXPERT = 512
MOE_BLOCK = 128

PLE_DIM = 256

kernel_name = 'hybrid_mla_swa_stickbreak_hmoe_step'


def rms_norm(x, g):
    xf = x.astype(jnp.float32)
    y = xf * lax.rsqrt(jnp.mean(xf * xf, axis=-1, keepdims=True) + NORM_EPS)
    return (y * g.astype(jnp.float32)).astype(x.dtype)


def rope(x, pos):
    half = x.shape[-1] // 2
    inv = jnp.exp(-math.log(ROPE_THETA) * jnp.arange(half, dtype=jnp.float32) / half)
    ang = pos.astype(jnp.float32)[:, None] * inv[None, :]
    cos, sin = jnp.cos(ang)[:, None, :], jnp.sin(ang)[:, None, :]
    xf = x.astype(jnp.float32)
    x1, x2 = xf[..., :half], xf[..., half:]
    return jnp.concatenate([x1 * cos - x2 * sin, x2 * cos + x1 * sin], axis=-1).astype(x.dtype)


def sink_softmax(sc, sink):
    m = jnp.maximum(sc.max(axis=-1, keepdims=True), sink)
    p = jnp.exp(sc - m)
    return p / (p.sum(axis=-1, keepdims=True) + jnp.exp(sink - m))


def mla_project(h, pos, w_in, g_q, g_kv, w_uq):
    proj = h @ w_in
    c_q = rms_norm(proj[..., :MLA_Q_RANK], g_q)
    c_kv = rms_norm(proj[..., MLA_Q_RANK:MLA_Q_RANK + MLA_KV_RANK], g_kv)
    k_pe = rope(proj[..., MLA_Q_RANK + MLA_KV_RANK:][:, :, None, :], pos)[:, :, 0]
    q = jnp.einsum('btr,rhd->bthd', c_q, w_uq)
    q_nope = q[..., :MLA_NOPE]
    q_pe = rope(q[..., MLA_NOPE:], pos)
    return q_nope, q_pe, c_kv, k_pe


def mla_prompt(h, w_in, g_q, g_kv, w_uq, w_uk, w_uv, w_o):
    b, s, _ = h.shape
    nb = s // Q_BLOCK
    q_nope, q_pe, c_kv, k_pe = mla_project(h, jnp.arange(s), w_in, g_q, g_kv, w_uq)
    k_nope = jnp.einsum('bsr,rhd->bshd', c_kv, w_uk)
    v = jnp.einsum('bsr,rhd->bshd', c_kv, w_uv)
    scale = (MLA_NOPE + MLA_ROPE) ** -0.5
    key_pos = jnp.arange(s)
    qn = q_nope.reshape(b, nb, Q_BLOCK, MLA_HEADS, MLA_NOPE).swapaxes(0, 1)
    qp = q_pe.reshape(b, nb, Q_BLOCK, MLA_HEADS, MLA_ROPE).swapaxes(0, 1)

    def block(args):
        qn_b, qp_b, blk = args
        sc = (jnp.einsum('bqhd,bshd->bhqs', qn_b, k_nope)
              + jnp.einsum('bqhd,bsd->bhqs', qp_b, k_pe)).astype(jnp.float32) * scale
        q_pos = blk * Q_BLOCK + jnp.arange(Q_BLOCK)
        sc = jnp.where(key_pos[None, :] <= q_pos[:, None], sc, -jnp.inf)
        p = jax.nn.softmax(sc, axis=-1).astype(v.dtype)
        return jnp.einsum('bhqs,bshd->bqhd', p, v)

    o = lax.map(block, (qn, qp, jnp.arange(nb)))
    o = o.swapaxes(0, 1).reshape(b, s, MLA_HEADS * MLA_V)
    return o @ w_o, c_kv, k_pe


def mla_sample(h, ckv_pool, kpe_pool, page_table, w_in, g_q, g_kv, w_uq, w_uk, w_uv, w_o):
    b, t, _ = h.shape
    pos = PAST_LEN + jnp.arange(t)
    q_nope, q_pe, c_kv, k_pe = mla_project(h, pos, w_in, g_q, g_kv, w_uq)
    q_lat = jnp.einsum('bthd,rhd->bthr', q_nope, w_uk)
    scale = (MLA_NOPE + MLA_ROPE) ** -0.5

    def scores(c, kp):
        return (jnp.einsum('bthr,bsr->bhts', q_lat, c)
                + jnp.einsum('bthd,bsd->bhts', q_pe, kp)).astype(jnp.float32) * scale

    sc = scores(c_kv, k_pe)
    causal = jnp.arange(t)[None, :] <= jnp.arange(t)[:, None]
    sc = jnp.where(causal, sc, -jnp.inf)
    m = sc.max(axis=-1)
    p = jnp.exp(sc - m[..., None])
    l = p.sum(axis=-1)
    acc = jnp.einsum('bhts,bsr->bhtr', p, c_kv.astype(jnp.float32))

    def page_step(carry, pages):
        m, l, acc = carry
        c = ckv_pool[pages]
        sc = scores(c, kpe_pool[pages])
        m_new = jnp.maximum(m, sc.max(axis=-1))
        corr = jnp.exp(m - m_new)
        p = jnp.exp(sc - m_new[..., None])
        acc = acc * corr[..., None] + jnp.einsum('bhts,bsr->bhtr', p, c.astype(jnp.float32))
        return (m_new, l * corr + p.sum(axis=-1), acc), None

    (m, l, acc), _ = lax.scan(page_step, (m, l, acc), page_table.T)
    o_lat = (acc / l[..., None]).astype(h.dtype)
    o = jnp.einsum('bhtr,rhd->bthd', o_lat, w_uv).reshape(b, t, MLA_HEADS * MLA_V)
    return o @ w_o, c_kv, k_pe


def swa_project(h, pos, w_in):
    b, t, _ = h.shape
    nq, nkv = SWA_HEADS * SWA_HEAD_DIM, SWA_KV_HEADS * SWA_HEAD_DIM
    proj = h @ w_in
    q = rope(proj[..., :nq].reshape(b, t, SWA_HEADS, SWA_HEAD_DIM), pos)
    k = rope(proj[..., nq:nq + nkv].reshape(b, t, SWA_KV_HEADS, SWA_HEAD_DIM), pos)
    v = proj[..., nq + nkv:].reshape(b, t, SWA_KV_HEADS, SWA_HEAD_DIM)
    return q, k, v


def swa_prompt(h, w_in, sinks, w_o):
    b, s, _ = h.shape
    nb = s // WINDOW
    g = SWA_HEADS // SWA_KV_HEADS
    q, k, v = swa_project(h, jnp.arange(s), w_in)

    def band(x):
        prev = jnp.concatenate([jnp.zeros_like(x[:, :WINDOW]), x[:, :s - WINDOW]], axis=1)
        return jnp.concatenate([prev.reshape(b, nb, WINDOW, SWA_KV_HEADS, SWA_HEAD_DIM),
                                x.reshape(b, nb, WINDOW, SWA_KV_HEADS, SWA_HEAD_DIM)], axis=2)

    kb, vb = band(k), band(v)
    qb = q.reshape(b, nb, WINDOW, SWA_KV_HEADS, g, SWA_HEAD_DIM)
    sc = jnp.einsum('bnqkgd,bnskd->bnkgqs', qb, kb).astype(jnp.float32) * SWA_HEAD_DIM ** -0.5
    rel = (WINDOW + jnp.arange(WINDOW))[:, None] - jnp.arange(2 * WINDOW)[None, :]
    key_abs = (jnp.arange(nb) * WINDOW - WINDOW)[:, None] + jnp.arange(2 * WINDOW)[None, :]
    mask = ((rel >= 0) & (rel < WINDOW))[None] & (key_abs >= 0)[:, None, :]
    sc = jnp.where(mask[None, :, None, None], sc, -jnp.inf)
    p = sink_softmax(sc, sinks.astype(jnp.float32).reshape(SWA_KV_HEADS, g, 1, 1)).astype(v.dtype)
    o = jnp.einsum('bnkgqs,bnskd->bnqkgd', p, vb).reshape(b, s, SWA_HEADS * SWA_HEAD_DIM)
    return o @ w_o, k[:, s - WINDOW:], v[:, s - WINDOW:]


def swa_sample(h, win_k, win_v, w_in, sinks, w_o):
    b, t, _ = h.shape
    g = SWA_HEADS // SWA_KV_HEADS
    pos = PAST_LEN + jnp.arange(t)
    q, k, v = swa_project(h, pos, w_in)
    kc = jnp.concatenate([win_k, k], axis=1)
    vc = jnp.concatenate([win_v, v], axis=1)
    key_pos = PAST_LEN - WINDOW + jnp.arange(WINDOW + t)
    rel = pos[:, None] - key_pos[None, :]
    mask = (rel >= 0) & (rel < WINDOW)
    qg = q.reshape(b, t, SWA_KV_HEADS, g, SWA_HEAD_DIM)
    sc = jnp.einsum('btkgd,bskd->bkgts', qg, kc).astype(jnp.float32) * SWA_HEAD_DIM ** -0.5
    sc = jnp.where(mask, sc, -jnp.inf)
    p = sink_softmax(sc, sinks.astype(jnp.float32).reshape(SWA_KV_HEADS, g, 1, 1)).astype(vc.dtype)
    o = jnp.einsum('bkgts,bskd->btkgd', p, vc).reshape(b, t, SWA_HEADS * SWA_HEAD_DIM)
    return o @ w_o, kc[:, -WINDOW:], vc[:, -WINDOW:]


def sb_project(h, w_in):
    b, t, _ = h.shape
    nq, nkv = SB_HEADS * SB_HEAD_DIM, SB_KV_HEADS * SB_HEAD_DIM
    proj = h @ w_in
    q = proj[..., :nq].reshape(b, t, SB_HEADS, SB_HEAD_DIM)
    k = proj[..., nq:nq + nkv].reshape(b, t, SB_KV_HEADS, SB_HEAD_DIM)
    v = proj[..., nq + nkv:].reshape(b, t, SB_KV_HEADS, SB_HEAD_DIM)
    return q, k, v


def sb_prompt(h, w_in, w_o):
    b, s, _ = h.shape
    nb = s // Q_BLOCK
    g = SB_HEADS // SB_KV_HEADS
    scale = SB_HEAD_DIM ** -0.5
    q, k, v = sb_project(h, w_in)
    qb = q.reshape(b, nb, Q_BLOCK, SB_KV_HEADS, g, SB_HEAD_DIM).swapaxes(0, 1)
    key_pos = jnp.arange(s)

    def block(args):
        q_b, blk = args
        z = jnp.einsum('bqkgd,bskd->bkgqs', q_b, k).astype(jnp.float32) * scale
        q_pos = blk * Q_BLOCK + jnp.arange(Q_BLOCK)
        strict = key_pos[None, :] < q_pos[:, None]
        log_1mb = jnp.where(strict, jax.nn.log_sigmoid(-z), 0.0)
        suffix = lax.cumsum(log_1mb, axis=4, reverse=True) - log_1mb
        a = jnp.where(strict, jnp.exp(jax.nn.log_sigmoid(z) + suffix), 0.0).astype(v.dtype)
        return jnp.einsum('bkgqs,bskd->bqkgd', a, v)

    o = lax.map(block, (qb, jnp.arange(nb)))
    o = o.swapaxes(0, 1).reshape(b, s, SB_HEADS * SB_HEAD_DIM)
    return o @ w_o, k, v


def sb_sample(h, k_pool, v_pool, page_table, w_in, w_o):
    b, t, _ = h.shape
    g = SB_HEADS // SB_KV_HEADS
    scale = SB_HEAD_DIM ** -0.5
    q, k, v = sb_project(h, w_in)
    qg = q.reshape(b, t, SB_KV_HEADS, g, SB_HEAD_DIM)
    strict = jnp.arange(t)[None, :] < jnp.arange(t)[:, None]
    z = jnp.einsum('btkgd,bskd->bkgts', qg, k).astype(jnp.float32) * scale
    log_1mb = jnp.where(strict, jax.nn.log_sigmoid(-z), 0.0)
    suffix = lax.cumsum(log_1mb, axis=4, reverse=True) - log_1mb
    a = jnp.where(strict, jnp.exp(jax.nn.log_sigmoid(z) + suffix), 0.0)
    out = jnp.einsum('bkgts,bskd->bkgtd', a, v.astype(jnp.float32))
    later = log_1mb.sum(axis=-1)

    def page_step(carry, pages):
        later, out = carry
        z = jnp.einsum('btkgd,bskd->bkgts', qg, k_pool[pages]).astype(jnp.float32) * scale
        log_1mb = jax.nn.log_sigmoid(-z)
        suffix = lax.cumsum(log_1mb, axis=4, reverse=True) - log_1mb + later[..., None]
        a = jnp.exp(jax.nn.log_sigmoid(z) + suffix)
        out = out + jnp.einsum('bkgts,bskd->bkgtd', a, v_pool[pages].astype(jnp.float32))
        return (later + log_1mb.sum(axis=-1), out), None

    (_, out), _ = lax.scan(page_step, (later, out), page_table.T, reverse=True)
    o = out.transpose(0, 3, 1, 2, 4).reshape(b, t, SB_HEADS * SB_HEAD_DIM).astype(h.dtype)
    return o @ w_o, k, v


def hmoe(hn, w_group, b_group, w_fine, b_fine, w_gate, w_up, w_down):
    shape = hn.shape
    x = hn.reshape(-1, D_MODEL)
    n_tok = x.shape[0]
    xf = x.astype(jnp.float32)
    lg = xf @ w_group.astype(jnp.float32) + b_group.astype(jnp.float32)
    grp = jnp.argmax(lg, axis=-1)
    p_grp = jax.nn.softmax(lg, axis=-1).max(axis=-1, keepdims=True)
    lf = (xf @ w_fine.astype(jnp.float32) + b_fine.astype(jnp.float32)).reshape(n_tok, N_GROUPS, EXPERTS_PER_GROUP)
    lf = lf[jnp.arange(n_tok), grp]
    top_v, top_i = lax.top_k(lf, TOP_K_FINE)
    gate = p_grp * jax.nn.softmax(top_v, axis=-1)
    expert = grp[:, None] * EXPERTS_PER_GROUP + top_i
    n_asg = n_tok * TOP_K_FINE
    flat_e = expert.reshape(n_asg).astype(jnp.int32)
    order = jnp.argsort(flat_e)
    e_sorted = flat_e[order]
    counts = jnp.zeros((N_EXPERTS,), jnp.int32).at[flat_e].add(1)
    padded = (counts + MOE_BLOCK - 1) // MOE_BLOCK * MOE_BLOCK
    pad_end = jnp.cumsum(padded)
    first = jnp.cumsum(counts) - counts
    dest = pad_end[e_sorted] - padded[e_sorted] + jnp.arange(n_asg) - first[e_sorted]
    n_blocks = -(-n_asg // MOE_BLOCK) + N_EXPERTS
    n_slots = n_blocks * MOE_BLOCK
    slot_tok = jnp.full((n_slots,), n_tok, jnp.int32).at[dest].set((order // TOP_K_FINE).astype(jnp.int32))
    slot_gate = jnp.zeros((n_slots,), jnp.float32).at[dest].set(gate.reshape(n_asg)[order])
    block_e = jnp.minimum(jnp.searchsorted(pad_end, jnp.arange(n_blocks) * MOE_BLOCK, side='right'), N_EXPERTS - 1)
    x_pad = jnp.concatenate([x, jnp.zeros((1, D_MODEL), x.dtype)], axis=0)
    xb = x_pad[slot_tok].reshape(n_blocks, MOE_BLOCK, D_MODEL)

    def expert_block(args):
        xe, e = args
        return (jax.nn.silu(xe @ w_gate[e]) * (xe @ w_up[e])) @ w_down[e]

    yb = lax.map(expert_block, (xb, block_e)).reshape(n_slots, D_MODEL)
    y = jnp.zeros((n_tok + 1, D_MODEL), jnp.float32).at[slot_tok].add(yb.astype(jnp.float32) * slot_gate[:, None])
    return y[:n_tok].astype(hn.dtype).reshape(shape)


def per_layer_embed(h, p, g, w_gate, w_proj):
    gate = jax.nn.sigmoid((rms_norm(h, g) @ w_gate).astype(jnp.float32))
    return h + (gate * (p @ w_proj).astype(jnp.float32)).astype(h.dtype)


def setup_inputs(seed: int = 0) -> dict:
    key = jax.random.key(seed)
    ks = jax.random.split(key, 40)
    n_pages = PAST_LEN // PAGE_SIZE
    n_used = DEC_BATCH * n_pages
    n_pool = n_used + max(1, n_used // 4)
    f32 = jnp.float32

    def nrm(k, shape, scale=1.0):
        return jax.random.normal(k, shape, f32) * scale

    def gain(k, shape):
        return 1.0 + 0.02 * jax.random.normal(k, shape, f32)

    page_table = jax.random.permutation(ks[8], n_pool)[:n_used].reshape(DEC_BATCH, n_pages).astype(jnp.int32)
    mla_in = MLA_Q_RANK + MLA_KV_RANK + MLA_ROPE
    swa_in = (SWA_HEADS + 2 * SWA_KV_HEADS) * SWA_HEAD_DIM
    sb_in = (SB_HEADS + 2 * SB_KV_HEADS) * SB_HEAD_DIM
    return {
        'x_prompt': nrm(ks[0], (BATCH, SEQ, D_MODEL)),
        'x_sample': nrm(ks[1], (DEC_BATCH, DEC_SEQ, D_MODEL)),
        'cache_mla_ckv': nrm(ks[2], (N_MLA_LAYERS, n_pool, PAGE_SIZE, MLA_KV_RANK)),
        'cache_mla_kpe': nrm(ks[3], (N_MLA_LAYERS, n_pool, PAGE_SIZE, MLA_ROPE)),
        'cache_swa_k': nrm(ks[4], (N_SWA_LAYERS, DEC_BATCH, WINDOW, SWA_KV_HEADS, SWA_HEAD_DIM)),
        'cache_swa_v': nrm(ks[5], (N_SWA_LAYERS, DEC_BATCH, WINDOW, SWA_KV_HEADS, SWA_HEAD_DIM)),
        'cache_sb_k': nrm(ks[6], (N_SB_LAYERS, n_pool, PAGE_SIZE, SB_KV_HEADS, SB_HEAD_DIM)),
        'cache_sb_v': nrm(ks[7], (N_SB_LAYERS, n_pool, PAGE_SIZE, SB_KV_HEADS, SB_HEAD_DIM)),
        'page_table': page_table,
        'p_prompt': nrm(ks[9], (DEPTH, BATCH, SEQ, PLE_DIM)),
        'p_sample': nrm(ks[10], (DEPTH, DEC_BATCH, DEC_SEQ, PLE_DIM)),
        'g_mix': gain(ks[11], (DEPTH, D_MODEL)),
        'g_ffn': gain(ks[12], (DEPTH, D_MODEL)),
        'g_ple': gain(ks[13], (DEPTH, D_MODEL)),
        'g_final': gain(ks[14], (D_MODEL,)),
        'mla_w_in': nrm(ks[15], (N_MLA_LAYERS, D_MODEL, mla_in), D_MODEL ** -0.5),
        'mla_g_q': gain(ks[16], (N_MLA_LAYERS, MLA_Q_RANK)),
        'mla_g_kv': gain(ks[17], (N_MLA_LAYERS, MLA_KV_RANK)),
        'mla_w_uq': nrm(ks[18], (N_MLA_LAYERS, MLA_Q_RANK, MLA_HEADS, MLA_NOPE + MLA_ROPE), MLA_Q_RANK ** -0.5),
        'mla_w_uk': nrm(ks[19], (N_MLA_LAYERS, MLA_KV_RANK, MLA_HEADS, MLA_NOPE), MLA_KV_RANK ** -0.5),
        'mla_w_uv': nrm(ks[20], (N_MLA_LAYERS, MLA_KV_RANK, MLA_HEADS, MLA_V), MLA_KV_RANK ** -0.5),
        'mla_w_o': nrm(ks[21], (N_MLA_LAYERS, MLA_HEADS * MLA_V, D_MODEL), (MLA_HEADS * MLA_V) ** -0.5),
        'swa_w_in': nrm(ks[22], (N_SWA_LAYERS, D_MODEL, swa_in), D_MODEL ** -0.5),
        'swa_sinks': nrm(ks[23], (N_SWA_LAYERS, SWA_HEADS), 0.5),
        'swa_w_o': nrm(ks[24], (N_SWA_LAYERS, SWA_HEADS * SWA_HEAD_DIM, D_MODEL), (SWA_HEADS * SWA_HEAD_DIM) ** -0.5),
        'sb_w_in': nrm(ks[25], (N_SB_LAYERS, D_MODEL, sb_in), D_MODEL ** -0.5),
        'sb_w_o': nrm(ks[26], (N_SB_LAYERS, SB_HEADS * SB_HEAD_DIM, D_MODEL), (SB_HEADS * SB_HEAD_DIM) ** -0.5),
        'moe_w_group': nrm(ks[27], (DEPTH, D_MODEL, N_GROUPS), D_MODEL ** -0.5),
        'moe_b_group': nrm(ks[28], (DEPTH, N_GROUPS), 0.01),
        'moe_w_fine': nrm(ks[29], (DEPTH, D_MODEL, N_EXPERTS), D_MODEL ** -0.5),
        'moe_b_fine': nrm(ks[30], (DEPTH, N_EXPERTS), 0.01),
        'moe_w_gate': nrm(ks[31], (DEPTH, N_EXPERTS, D_MODEL, D_EXPERT), D_MODEL ** -0.5),
        'moe_w_up': nrm(ks[32], (DEPTH, N_EXPERTS, D_MODEL, D_EXPERT), D_MODEL ** -0.5),
        'moe_w_down': nrm(ks[33], (DEPTH, N_EXPERTS, D_EXPERT, D_MODEL), D_EXPERT ** -0.5),
        'ple_w_gate': nrm(ks[34], (DEPTH, D_MODEL, D_MODEL), D_MODEL ** -0.5),
        'ple_w_proj': nrm(ks[35], (DEPTH, PLE_DIM, D_MODEL), PLE_DIM ** -0.5),
    }


def reference(x_prompt, x_sample, cache_mla_ckv, cache_mla_kpe, cache_swa_k, cache_swa_v, cache_sb_k, cache_sb_v,
              page_table, p_prompt, p_sample, g_mix, g_ffn, g_ple, g_final,
              mla_w_in, mla_g_q, mla_g_kv, mla_w_uq, mla_w_uk, mla_w_uv, mla_w_o,
              swa_w_in, swa_sinks, swa_w_o, sb_w_in, sb_w_o,
              moe_w_group, moe_b_group, moe_w_fine, moe_b_fine, moe_w_gate, moe_w_up, moe_w_down,
              ple_w_gate, ple_w_proj):
    xp, xs = x_prompt, x_sample
    p_ckv, p_kpe, s_ckv, s_kpe = [], [], [], []
    p_wk, p_wv, s_wk, s_wv = [], [], [], []
    p_sk, p_sv, s_sk, s_sv = [], [], [], []
    for i in range(DEPTH):
        kind, j = i % N_MIXERS, i // N_MIXERS
        hp, hs = rms_norm(xp, g_mix[i]), rms_norm(xs, g_mix[i])
        if kind == 0:
            mla_w = (mla_w_in[j], mla_g_q[j], mla_g_kv[j], mla_w_uq[j], mla_w_uk[j], mla_w_uv[j], mla_w_o[j])
            op, ckv_p, kpe_p = mla_prompt(hp, *mla_w)
            os_, ckv_s, kpe_s = mla_sample(hs, cache_mla_ckv[j], cache_mla_kpe[j], page_table, *mla_w)
            p_ckv.append(ckv_p); p_kpe.append(kpe_p); s_ckv.append(ckv_s); s_kpe.append(kpe_s)
        elif kind == 1:
            op, wk_p, wv_p = swa_prompt(hp, swa_w_in[j], swa_sinks[j], swa_w_o[j])
            os_, wk_s, wv_s = swa_sample(hs, cache_swa_k[j], cache_swa_v[j], swa_w_in[j], swa_sinks[j], swa_w_o[j])
            p_wk.append(wk_p); p_wv.append(wv_p); s_wk.append(wk_s); s_wv.append(wv_s)
        else:
            op, sk_p, sv_p = sb_prompt(hp, sb_w_in[j], sb_w_o[j])
            os_, sk_s, sv_s = sb_sample(hs, cache_sb_k[j], cache_sb_v[j], page_table, sb_w_in[j], sb_w_o[j])
            p_sk.append(sk_p); p_sv.append(sv_p); s_sk.append(sk_s); s_sv.append(sv_s)
        xp, xs = xp + op, xs + os_
        moe_w = (moe_w_group[i], moe_b_group[i], moe_w_fine[i], moe_b_fine[i], moe_w_gate[i], moe_w_up[i], moe_w_down[i])
        xp = xp + hmoe(rms_norm(xp, g_ffn[i]), *moe_w)
        xs = xs + hmoe(rms_norm(xs, g_ffn[i]), *moe_w)
        xp = per_layer_embed(xp, p_prompt[i], g_ple[i], ple_w_gate[i], ple_w_proj[i])
        xs = per_layer_embed(xs, p_sample[i], g_ple[i], ple_w_gate[i], ple_w_proj[i])
    y_prompt = rms_norm(xp, g_final)
    y_sample = rms_norm(xs, g_final)
    return (y_prompt, y_sample,
            jnp.stack(p_ckv), jnp.stack(p_kpe), jnp.stack(p_wk), jnp.stack(p_wv), jnp.stack(p_sk), jnp.stack(p_sv),
            jnp.stack(s_ckv), jnp.stack(s_kpe), jnp.stack(s_wk), jnp.stack(s_wv), jnp.stack(s_sk), jnp.stack(s_sv))
```

```python
import functools
import math

import jax
import jax.numpy as jnp
from jax import lax
from jax.experimental import pallas as pl
from jax.experimental.pallas import tpu as pltpu

F32 = jnp.float32
BF16 = jnp.bfloat16
I32 = jnp.int32

NORM_EPS = 1e-6
ROPE_THETA = 10000.0
PAGE_SIZE = 128

MLA_HEADS = 8
MLA_NOPE = 128
MLA_ROPE = 64
MLA_V = 128
MLA_Q_RANK = 512
MLA_KV_RANK = 256
MLA_QK_PAD = 256

SWA_HEADS = 16
SWA_KV_HEADS = 4
SWA_HEAD_DIM = 64
WINDOW = 128

SB_HEADS = 8
SB_KV_HEADS = 4
SB_HEAD_DIM = 128
SB_BLOCK = 128
SB_DEAD_LOG = -100.0

N_GROUPS = 4
EXPERTS_PER_GROUP = 8
N_EXPERTS = N_GROUPS * EXPERTS_PER_GROUP
D_EXPERT = 512
MOE_SLOT_BLOCK = 256
ROUTER_LANES = 128

LOG2E = 1.4426950408889634
NEG_BIG = -1e30
VMEM_LIMIT = 56 * 1024 * 1024
PAGES_PER_STEP = 8


def _cparams(sem, vmem=VMEM_LIMIT):
    return pltpu.CompilerParams(dimension_semantics=sem, vmem_limit_bytes=vmem)


def _rms(x, g):
    return x * lax.rsqrt(jnp.mean(x * x, axis=-1, keepdims=True) + NORM_EPS) * g


def _dot(a, b):
    return jnp.dot(a, b, preferred_element_type=F32)


def _dot_nt(a, b):
    return lax.dot_general(a, b, (((1,), (1,)), ((), ())), preferred_element_type=F32)


def _full_spec(shape):
    nd = len(shape)
    return pl.BlockSpec(shape, lambda *_: (0,) * nd)


def _row_tile(n, pref):
    return pref if n % pref == 0 else n


def _rope_tables(pos, half, lanes):
    inv = jnp.exp(-math.log(ROPE_THETA) * jnp.arange(half, dtype=F32) / half)
    ang = pos.astype(F32)[:, None] * inv[None, :]
    cos, sin = jnp.cos(ang), jnp.sin(ang)
    reps = lanes // (2 * half)
    if reps >= 1:
        return jnp.tile(jnp.concatenate([cos, cos], 1), (1, reps)), jnp.tile(jnp.concatenate([sin, sin], 1), (1, reps))
    pad = lanes - 2 * half
    n = pos.shape[0]
    return (jnp.concatenate([cos, cos, jnp.ones((n, pad), F32)], 1),
            jnp.concatenate([sin, sin, jnp.zeros((n, pad), F32)], 1))


def _rot_cols(w, head_dim):
    k, n = w.shape
    w3 = w.reshape(k, n // head_dim, head_dim)
    half = head_dim // 2
    return jnp.concatenate([-w3[..., half:], w3[..., :half]], axis=-1).reshape(k, n)


def _mla_proj_kernel(x_ref, gmix_ref, win_ref, gq_ref, gkv_ref, cos_ref, sin_ref, wq_ref, wqr_ref, wuk_ref, wuv_ref,
                     q_ref, kk_ref, v_ref, ckv_ref, kpe_ref, *, qscale):
    h = _rms(x_ref[...], gmix_ref[...]).astype(BF16)
    proj = _dot(h, win_ref[...])
    c_q = _rms(proj[:, :MLA_Q_RANK], gq_ref[...]).astype(BF16)
    c_kv = _rms(proj[:, MLA_Q_RANK:MLA_Q_RANK + MLA_KV_RANK], gkv_ref[...])
    ckv_ref[...] = c_kv
    cos = cos_ref[...]
    sin = sin_ref[...]
    o = MLA_Q_RANK + MLA_KV_RANK
    kpe = proj[:, o:o + 128] * cos + proj[:, o + 128:o + 256] * sin
    kpe_ref[...] = kpe[:, :MLA_ROPE]
    kpe_b = kpe.astype(BF16)
    ckv_b = c_kv.astype(BF16)
    kn = _dot(ckv_b, wuk_ref[...])
    v_ref[...] = _dot(ckv_b, wuv_ref[...]).astype(BF16)
    qe = _dot(c_q, wq_ref[...])
    qr = _dot(c_q, wqr_ref[...])
    for hd in range(MLA_HEADS):
        a = hd * MLA_QK_PAD
        pe = qe[:, a:a + 128] * cos + qr[:, hd * 128:(hd + 1) * 128] * sin
        q_ref[:, a:a + 128] = (pe * qscale).astype(BF16)
        q_ref[:, a + 128:a + 256] = (qe[:, a + 128:a + 256] * qscale).astype(BF16)
        kk_ref[:, a:a + 128] = kpe_b
        kk_ref[:, a + 128:a + 256] = kn[:, hd * 128:(hd + 1) * 128].astype(BF16)


def _mla_weights(w_in, w_uq, w_uk, w_uv):
    d = w_in.shape[0]
    o = MLA_Q_RANK + MLA_KV_RANK
    w_kpe = w_in[:, o:]
    z64 = jnp.zeros((d, 128 - MLA_ROPE), F32)
    win = jnp.concatenate([w_in[:, :o], w_kpe, z64, _rot_cols(w_kpe, MLA_ROPE), z64], axis=1).astype(BF16)
    r = w_uq.shape[0]
    w_nope = w_uq[:, :, :MLA_NOPE]
    w_pe = w_uq[:, :, MLA_NOPE:]
    zq = jnp.zeros((r, MLA_HEADS, 128 - MLA_ROPE), F32)
    wq = jnp.concatenate([w_pe, zq, w_nope], axis=-1).reshape(r, MLA_HEADS * MLA_QK_PAD).astype(BF16)
    w_pe_rot = _rot_cols(w_pe.reshape(r, MLA_HEADS * MLA_ROPE), MLA_ROPE).reshape(r, MLA_HEADS, MLA_ROPE)
    wqr = jnp.concatenate([w_pe_rot, zq], axis=-1).reshape(r, MLA_HEADS * 128).astype(BF16)
    wuk = w_uk.reshape(MLA_KV_RANK, MLA_HEADS * MLA_NOPE).astype(BF16)
    wuv = w_uv.reshape(MLA_KV_RANK, MLA_HEADS * MLA_V).astype(BF16)
    return win, wq, wqr, wuk, wuv


def _mla_proj(x, g_mix, g_q, g_kv, weights, cos, sin, n_pos_blocks, tm):
    win, wq, wqr, wuk, wuv = weights
    t, d = x.shape
    qscale = (MLA_NOPE + MLA_ROPE) ** -0.5 * LOG2E
    row = lambda i: (i, 0)
    pos = lambda i: (i % n_pos_blocks, 0)
    hq = MLA_HEADS * MLA_QK_PAD
    return pl.pallas_call(
        functools.partial(_mla_proj_kernel, qscale=qscale),
        grid=(t // tm,),
        in_specs=[pl.BlockSpec((tm, d), row), _full_spec((1, d)), _full_spec(win.shape), _full_spec((1, MLA_Q_RANK)),
                  _full_spec((1, MLA_KV_RANK)), pl.BlockSpec((tm, 128), pos), pl.BlockSpec((tm, 128), pos),
                  _full_spec(wq.shape), _full_spec(wqr.shape), _full_spec(wuk.shape), _full_spec(wuv.shape)],
        out_specs=[pl.BlockSpec((tm, hq), row), pl.BlockSpec((tm, hq), row), pl.BlockSpec((tm, MLA_HEADS * MLA_V), row),
                   pl.BlockSpec((tm, MLA_KV_RANK), row), pl.BlockSpec((tm, MLA_ROPE), row)],
        out_shape=[jax.ShapeDtypeStruct((t, hq), BF16), jax.ShapeDtypeStruct((t, hq), BF16),
                   jax.ShapeDtypeStruct((t, MLA_HEADS * MLA_V), BF16), jax.ShapeDtypeStruct((t, MLA_KV_RANK), F32),
                   jax.ShapeDtypeStruct((t, MLA_ROPE), F32)],
        compiler_params=_cparams(("parallel",)),
        name="mla_proj",
    )(x, g_mix.reshape(1, d), win, g_q.reshape(1, -1), g_kv.reshape(1, -1), cos, sin, wq, wqr, wuk, wuv)


def _mla_flash_kernel(q_ref, k_ref, v_ref, o_ref, *, tq):
    qi = pl.program_id(2)
    q = q_ref[...]

    def step(j, carry, masked):
        m, l, acc = carry
        start = pl.multiple_of(j * tq, tq)
        s = _dot_nt(q, k_ref[pl.ds(start, tq), :])
        if masked:
            row = lax.broadcasted_iota(I32, (tq, tq), 0)
            col = lax.broadcasted_iota(I32, (tq, tq), 1)
            s = jnp.where(col <= row, s, NEG_BIG)
        m_new = jnp.maximum(m, s.max(axis=-1, keepdims=True))
        corr = jnp.exp2(m - m_new)
        p = jnp.exp2(s - m_new)
        l = l * corr + p.sum(axis=-1, keepdims=True)
        acc = acc * corr + _dot(p.astype(BF16), v_ref[pl.ds(start, tq), :])
        return m_new, l, acc

    init = (jnp.full((tq, 1), NEG_BIG, F32), jnp.zeros((tq, 1), F32), jnp.zeros((tq, MLA_V), F32))
    carry = lax.fori_loop(0, qi, lambda j, c: step(j, c, False), init)
    _, l, acc = step(qi, carry, True)
    o_ref[...] = (acc / l).astype(BF16)


def _mla_flash(q, kk, v, batch, seq, tq):
    nq = seq // tq
    return pl.pallas_call(
        functools.partial(_mla_flash_kernel, tq=tq),
        grid=(batch, MLA_HEADS, nq),
        in_specs=[pl.BlockSpec((tq, MLA_QK_PAD), lambda b, h, i: (b * nq + i, h)),
                  pl.BlockSpec((seq, MLA_QK_PAD), lambda b, h, i: (b, h)),
                  pl.BlockSpec((seq, MLA_V), lambda b, h, i: (b, h))],
        out_specs=pl.BlockSpec((tq, MLA_V), lambda b, h, i: (b * nq + i, h)),
        out_shape=jax.ShapeDtypeStruct((batch * seq, MLA_HEADS * MLA_V), BF16),
        compiler_params=_cparams(("parallel", "parallel", "arbitrary")),
        name="mla_flash",
    )(q, kk, v)


def _bmm_kernel(x_ref, w_ref, o_ref):
    o_ref[0] = _dot(x_ref[0], w_ref[0]).astype(o_ref.dtype)


def _bmm(x, w, out_dtype):
    h, m, k = x.shape
    n = w.shape[2]
    return pl.pallas_call(
        _bmm_kernel,
        grid=(h,),
        in_specs=[pl.BlockSpec((1, m, k), lambda i: (i, 0, 0)), pl.BlockSpec((1, k, n), lambda i: (i, 0, 0))],
        out_specs=pl.BlockSpec((1, m, n), lambda i: (i, 0, 0)),
        out_shape=jax.ShapeDtypeStruct((h, m, n), out_dtype),
        compiler_params=_cparams(("parallel",)),
        name="head_matmul",
    )(x, w)


def _mla_paged_kernel(pt_ref, qlat_ref, qpe_ref, cnew_ref, knew_ref, *rest, n_pp):
    ckv_refs = rest[:n_pp]
    kpe_refs = rest[n_pp:2 * n_pp]
    o_ref = rest[2 * n_pp]
    m_ref, l_ref, acc_ref = rest[2 * n_pp + 1:]
    g = pl.program_id(1)
    qlat = qlat_ref[0]
    qpe = qpe_ref[0]

    @pl.when(g == 0)
    def _():
        cn = cnew_ref[0]
        s0 = (jnp.sum(qlat.astype(F32) * cn.astype(BF16).astype(F32), axis=-1, keepdims=True)
              + jnp.sum(qpe.astype(F32) * knew_ref[0].astype(BF16).astype(F32), axis=-1, keepdims=True))
        m_ref[...] = jnp.broadcast_to(s0, m_ref.shape)
        l_ref[...] = jnp.ones(l_ref.shape, F32)
        acc_ref[...] = jnp.broadcast_to(cn, acc_ref.shape)

    cs = [r[0].astype(BF16) for r in ckv_refs]
    s = jnp.concatenate([_dot_nt(qlat, c) + _dot_nt(qpe, kr[0].astype(BF16)) for c, kr in zip(cs, kpe_refs)], axis=1)
    m_old = m_ref[:, :1]
    m_new = jnp.maximum(m_old, s.max(axis=-1, keepdims=True))
    corr = jnp.exp2(m_old - m_new)
    p = jnp.exp2(s - m_new)
    l_new = l_ref[:, :1] * corr + p.sum(axis=-1, keepdims=True)
    pb = p.astype(BF16)
    acc = acc_ref[...] * corr
    for j, c in enumerate(cs):
        acc = acc + _dot(pb[:, j * PAGE_SIZE:(j + 1) * PAGE_SIZE], c)
    acc_ref[...] = acc
    m_ref[...] = jnp.broadcast_to(m_new, m_ref.shape)
    l_ref[...] = jnp.broadcast_to(l_new, l_ref.shape)

    @pl.when(g == pl.num_programs(1) - 1)
    def _():
        o_ref[0] = (acc / l_new).astype(o_ref.dtype)


def _mla_paged(page_table, qlat, qpe, c_new, k_new, ckv_pool, kpe_pool):
    bd, n_pages = page_table.shape
    n_pp = math.gcd(PAGES_PER_STEP, n_pages)
    hds = qlat.shape[1]

    def page_map(j):
        return lambda b, g, pt: (pt[b, g * n_pp + j], 0, 0)

    per_b = lambda b, g, pt: (b, 0, 0)
    in_specs = [pl.BlockSpec((1, hds, MLA_KV_RANK), per_b), pl.BlockSpec((1, hds, MLA_ROPE), per_b),
                pl.BlockSpec((1, 1, MLA_KV_RANK), per_b), pl.BlockSpec((1, 1, MLA_ROPE), per_b)]
    in_specs += [pl.BlockSpec((1, PAGE_SIZE, MLA_KV_RANK), page_map(j)) for j in range(n_pp)]
    in_specs += [pl.BlockSpec((1, PAGE_SIZE, MLA_ROPE), page_map(j)) for j in range(n_pp)]
    return pl.pallas_call(
        functools.partial(_mla_paged_kernel, n_pp=n_pp),
        grid_spec=pltpu.PrefetchScalarGridSpec(
            num_scalar_prefetch=1, grid=(bd, n_pages // n_pp), in_specs=in_specs,
            out_specs=pl.BlockSpec((1, hds, MLA_KV_RANK), per_b),
            scratch_shapes=[pltpu.VMEM((hds, 128), F32), pltpu.VMEM((hds, 128), F32), pltpu.VMEM((hds, MLA_KV_RANK), F32)]),
        out_shape=jax.ShapeDtypeStruct((bd, hds, MLA_KV_RANK), BF16),
        compiler_params=_cparams(("parallel", "arbitrary")),
        name="mla_paged",
    )(page_table, qlat, qpe, c_new, k_new, *([ckv_pool] * n_pp), *([kpe_pool] * n_pp))


def _swa_proj_kernel(x_ref, gmix_ref, win_ref, cos_ref, sin_ref, q_ref, k_ref, v_ref, *, qscale):
    nq, nkv = SWA_HEADS * SWA_HEAD_DIM, SWA_KV_HEADS * SWA_HEAD_DIM
    h = _rms(x_ref[...], gmix_ref[...]).astype(BF16)
    proj = _dot(h, win_ref[...])
    cos = cos_ref[...]
    sin = sin_ref[...]
    r0 = nq + 2 * nkv
    for c in range(nq // 128):
        a = c * 128
        q_ref[:, a:a + 128] = ((proj[:, a:a + 128] * cos + proj[:, r0 + a:r0 + a + 128] * sin) * qscale).astype(BF16)
    for c in range(nkv // 128):
        a = c * 128
        k_ref[:, a:a + 128] = proj[:, nq + a:nq + a + 128] * cos + proj[:, r0 + nq + a:r0 + nq + a + 128] * sin
    v_ref[...] = proj[:, nq + nkv:nq + 2 * nkv]


def _swa_proj(x, g_mix, w_in, cos, sin, n_pos_blocks, tm):
    t, d = x.shape
    nq, nkv = SWA_HEADS * SWA_HEAD_DIM, SWA_KV_HEADS * SWA_HEAD_DIM
    win = jnp.concatenate([w_in, _rot_cols(w_in[:, :nq + nkv], SWA_HEAD_DIM)], axis=1).astype(BF16)
    row = lambda i: (i, 0)
    pos = lambda i: (i % n_pos_blocks, 0)
    return pl.pallas_call(
        functools.partial(_swa_proj_kernel, qscale=SWA_HEAD_DIM ** -0.5 * LOG2E),
        grid=(t // tm,),
        in_specs=[pl.BlockSpec((tm, d), row), _full_spec((1, d)), _full_spec(win.shape),
                  pl.BlockSpec((tm, 128), pos), pl.BlockSpec((tm, 128), pos)],
        out_specs=[pl.BlockSpec((tm, nq), row), pl.BlockSpec((tm, nkv), row), pl.BlockSpec((tm, nkv), row)],
        out_shape=[jax.ShapeDtypeStruct((t, nq), BF16), jax.ShapeDtypeStruct((t, nkv), F32),
                   jax.ShapeDtypeStruct((t, nkv), F32)],
        compiler_params=_cparams(("parallel",)),
        name="swa_proj",
    )(x, g_mix.reshape(1, d), win, cos, sin)


def _sink_column(sink_ref, rows_per_head, kvh, n_rep):
    rid = lax.broadcasted_iota(I32, (n_rep * rows_per_head, 1), 0) // rows_per_head
    col = jnp.zeros((n_rep * rows_per_head, 1), F32)
    for g in range(n_rep):
        col = jnp.where(rid == g, sink_ref[kvh * n_rep + g], col)
    return col


def _swa_prompt_kernel(sink_ref, q_ref, kc_ref, kp_ref, vc_ref, vp_ref, o_ref):
    n = pl.program_id(1)
    w = WINDOW
    n_rep = SWA_HEADS // SWA_KV_HEADS
    hd = SWA_HEAD_DIM
    q = q_ref[...]
    kc = kc_ref[...].astype(BF16)
    kp = kp_ref[...].astype(BF16)
    vc = vc_ref[...].astype(BF16)
    vp = vp_ref[...].astype(BF16)
    t = lax.broadcasted_iota(I32, (n_rep * w, 2 * w), 0) % w
    c = lax.broadcasted_iota(I32, (n_rep * w, 2 * w), 1)
    first_key = jnp.where(n > 0, 0, w)
    valid = (c > t) & (c <= t + w) & (c >= first_key)
    for kvh in range(SWA_KV_HEADS):
        ks = slice(kvh * hd, (kvh + 1) * hd)
        kk = jnp.concatenate([kp[:, ks], kc[:, ks]], axis=0)
        vv = jnp.concatenate([vp[:, ks], vc[:, ks]], axis=0)
        qg = jnp.concatenate([q[:, (kvh * n_rep + g) * hd:(kvh * n_rep + g + 1) * hd] for g in range(n_rep)], axis=0)
        s = jnp.where(valid, _dot_nt(qg, kk), NEG_BIG)
        sink = _sink_column(sink_ref, w, kvh, n_rep)
        m = jnp.maximum(s.max(axis=-1, keepdims=True), sink)
        p = jnp.exp2(s - m)
        denom = p.sum(axis=-1, keepdims=True) + jnp.exp2(sink - m)
        o = _dot(p.astype(BF16), vv) / denom
        for g in range(n_rep):
            a = (kvh * n_rep + g) * hd
            o_ref[:, a:a + hd] = o[g * w:(g + 1) * w].astype(BF16)


def _swa_prompt(sinks2, q, k, v, batch, seq):
    nb = seq // WINDOW
    nq, nkv = SWA_HEADS * SWA_HEAD_DIM, SWA_KV_HEADS * SWA_HEAD_DIM
    cur = lambda b, n: (b * nb + n, 0)
    prev = lambda b, n: (b * nb + jnp.maximum(n - 1, 0), 0)
    return pl.pallas_call(
        _swa_prompt_kernel,
        grid=(batch, nb),
        in_specs=[pl.BlockSpec(memory_space=pltpu.SMEM), pl.BlockSpec((WINDOW, nq), cur),
                  pl.BlockSpec((WINDOW, nkv), cur), pl.BlockSpec((WINDOW, nkv), prev),
                  pl.BlockSpec((WINDOW, nkv), cur), pl.BlockSpec((WINDOW, nkv), prev)],
        out_specs=pl.BlockSpec((WINDOW, nq), cur),
        out_shape=jax.ShapeDtypeStruct((batch * seq, nq), BF16),
        compiler_params=_cparams(("parallel", "parallel")),
        name="swa_prompt",
    )(sinks2, q, k, k, v, v)


def _swa_sample_kernel(sink_ref, qbd_ref, kw_ref, vw_ref, kn_ref, vn_ref, o_ref):
    n_rep = SWA_HEADS // SWA_KV_HEADS
    hd = SWA_HEAD_DIM
    qbd = qbd_ref[0]
    kw = kw_ref[0].astype(BF16)
    vw = vw_ref[0].astype(BF16)
    kn = kn_ref[0].astype(BF16).astype(F32)
    vn = vn_ref[0].astype(BF16).astype(F32)
    s = _dot_nt(qbd, kw)
    col = lax.broadcasted_iota(I32, s.shape, 1)
    s = jnp.where(col >= 1, s, NEG_BIG)
    s_new = jnp.sum(qbd.astype(F32) * kn, axis=-1, keepdims=True)
    sink = _sink_column(sink_ref, 1, 0, SWA_HEADS)
    m = jnp.maximum(jnp.maximum(s.max(axis=-1, keepdims=True), s_new), sink)
    p = jnp.exp2(s - m)
    p_new = jnp.exp2(s_new - m)
    denom = p.sum(axis=-1, keepdims=True) + p_new + jnp.exp2(sink - m)
    full = (_dot(p.astype(BF16), vw) + p_new.astype(BF16).astype(F32) * vn) / denom
    rid = lax.broadcasted_iota(I32, (SWA_HEADS, hd), 0) // n_rep
    o = jnp.zeros((SWA_HEADS, hd), F32)
    for kvh in range(SWA_KV_HEADS):
        o = jnp.where(rid == kvh, full[:, kvh * hd:(kvh + 1) * hd], o)
    o_ref[0] = o.astype(BF16)


def _swa_sample(sinks2, qbd, kwin, vwin, k_new, v_new):
    bd = qbd.shape[0]
    nkv = SWA_KV_HEADS * SWA_HEAD_DIM
    per_b = lambda b: (b, 0, 0)
    return pl.pallas_call(
        _swa_sample_kernel,
        grid=(bd,),
        in_specs=[pl.BlockSpec(memory_space=pltpu.SMEM), pl.BlockSpec((1, SWA_HEADS, nkv), per_b),
                  pl.BlockSpec((1, WINDOW, nkv), per_b), pl.BlockSpec((1, WINDOW, nkv), per_b),
                  pl.BlockSpec((1, 1, nkv), per_b), pl.BlockSpec((1, 1, nkv), per_b)],
        out_specs=pl.BlockSpec((1, SWA_HEADS, SWA_HEAD_DIM), per_b),
        out_shape=jax.ShapeDtypeStruct((bd, SWA_HEADS, SWA_HEAD_DIM), BF16),
        compiler_params=_cparams(("parallel",)),
        name="swa_sample",
    )(sinks2, qbd, kwin, vwin, k_new, v_new)


def _block_diag_q(q, n_heads, n_kv, hd):
    b = q.shape[0]
    q3 = q.reshape(b, n_heads, hd)
    owner = (jnp.arange(n_heads) // (n_heads // n_kv))[:, None] == jnp.arange(n_kv)[None, :]
    return jnp.where(owner[None, :, :, None], q3[:, :, None, :], jnp.zeros((), q.dtype)).reshape(b, n_heads, n_kv * hd)


def _sb_proj_kernel(x_ref, gmix_ref, win_ref, q_ref, k_ref, v_ref, kb_ref, vb_ref, *, qscale):
    nq, nkv = SB_HEADS * SB_HEAD_DIM, SB_KV_HEADS * SB_HEAD_DIM
    h = _rms(x_ref[...], gmix_ref[...]).astype(BF16)
    proj = _dot(h, win_ref[...])
    q_ref[...] = (proj[:, :nq] * qscale).astype(BF16)
    k = proj[:, nq:nq + nkv]
    v = proj[:, nq + nkv:]
    k_ref[...] = k
    v_ref[...] = v
    kb_ref[...] = k.astype(BF16)
    vb_ref[...] = v.astype(BF16)


def _sb_proj(x, g_mix, w_in, tm):
    t, d = x.shape
    nq, nkv = SB_HEADS * SB_HEAD_DIM, SB_KV_HEADS * SB_HEAD_DIM
    row = lambda i: (i, 0)
    return pl.pallas_call(
        functools.partial(_sb_proj_kernel, qscale=SB_HEAD_DIM ** -0.5),
        grid=(t // tm,),
        in_specs=[pl.BlockSpec((tm, d), row), _full_spec((1, d)), _full_spec(w_in.shape)],
        out_specs=[pl.BlockSpec((tm, nq), row)] + [pl.BlockSpec((tm, nkv), row)] * 4,
        out_shape=[jax.ShapeDtypeStruct((t, nq), BF16), jax.ShapeDtypeStruct((t, nkv), F32),
                   jax.ShapeDtypeStruct((t, nkv), F32), jax.ShapeDtypeStruct((t, nkv), BF16),
                   jax.ShapeDtypeStruct((t, nkv), BF16)],
        compiler_params=_cparams(("parallel",)),
        name="sb_proj",
    )(x, g_mix.reshape(1, d), w_in.astype(BF16))


def _suffix_matrix():
    r = jnp.arange(SB_BLOCK)
    tri = (r[:, None] > r[None, :]).astype(BF16)
    return jnp.concatenate([tri, jnp.ones((SB_BLOCK, SB_BLOCK), BF16)], axis=1)


def _sb_tile(z, later, kv_v, tri_ones, strict):
    m = z.shape[0]
    log_1mb = -(jnp.maximum(z, 0.0) + jnp.log1p(jnp.exp(-jnp.abs(z))))
    if strict is not None:
        log_1mb = jnp.where(strict, log_1mb, 0.0)
    hi = log_1mb.astype(BF16)
    lo = (log_1mb - hi.astype(F32)).astype(BF16)
    sums = _dot(jnp.concatenate([hi, lo], axis=0), tri_ones)
    sums = sums[:m] + sums[m:]
    suffix = sums[:, :SB_BLOCK]
    total = sums[:, SB_BLOCK:]
    a = jnp.exp(z + log_1mb + suffix + later)
    if strict is not None:
        a = jnp.where(strict, a, 0.0)
    return _dot(a.astype(BF16), kv_v), total


def _sb_prompt_kernel(q_ref, k_ref, v_ref, tri_ref, o_ref):
    qi = pl.program_id(2)
    blk = SB_BLOCK
    n_rep = SB_HEADS // SB_KV_HEADS
    hd = SB_HEAD_DIM
    q = q_ref[...]
    qs = jnp.concatenate([q[:, g * hd:(g + 1) * hd] for g in range(n_rep)], axis=0)
    tri_ones = tri_ref[...]
    rows = n_rep * blk

    def tile(j, later, strict):
        start = pl.multiple_of(j * blk, blk)
        z = _dot_nt(qs, k_ref[pl.ds(start, blk), :])
        return _sb_tile(z, later, v_ref[pl.ds(start, blk), :], tri_ones, strict)

    t = lax.broadcasted_iota(I32, (rows, blk), 0) % blk
    c = lax.broadcasted_iota(I32, (rows, blk), 1)
    out, later = tile(qi, jnp.zeros((rows, blk), F32), c < t)

    def cond(carry):
        j, alive, _, _ = carry
        return (j >= 0) & (alive > SB_DEAD_LOG)

    def body(carry):
        j, _, later, out = carry
        d_out, total = tile(j, later, None)
        later = later + total
        return j - 1, jnp.max(later), later, out + d_out

    _, _, _, out = lax.while_loop(cond, body, (qi - 1, jnp.max(later), later, out))
    o_ref[...] = jnp.concatenate([out[g * blk:(g + 1) * blk] for g in range(n_rep)], axis=1).astype(BF16)


def _sb_prompt(q, kb, vb, batch, seq):
    nq = seq // SB_BLOCK
    n_rep = SB_HEADS // SB_KV_HEADS
    hd = SB_HEAD_DIM
    qmap = lambda b, h, i: (b * nq + i, h)
    return pl.pallas_call(
        _sb_prompt_kernel,
        grid=(batch, SB_KV_HEADS, nq),
        in_specs=[pl.BlockSpec((SB_BLOCK, n_rep * hd), qmap), pl.BlockSpec((seq, hd), lambda b, h, i: (b, h)),
                  pl.BlockSpec((seq, hd), lambda b, h, i: (b, h)), _full_spec((SB_BLOCK, 2 * SB_BLOCK))],
        out_specs=pl.BlockSpec((SB_BLOCK, n_rep * hd), qmap),
        out_shape=jax.ShapeDtypeStruct((batch * seq, SB_HEADS * hd), BF16),
        compiler_params=_cparams(("parallel", "parallel", "arbitrary")),
        name="sb_prompt",
    )(q, kb, vb, _suffix_matrix())


def _sb_paged_kernel(pt_ref, qbd_ref, tri_ref, *rest, n_pp):
    k_refs = rest[:n_pp]
    v_refs = rest[n_pp:2 * n_pp]
    o_ref = rest[2 * n_pp]
    later_ref, acc_ref = rest[2 * n_pp + 1:]
    g = pl.program_id(1)
    n_rep = SB_HEADS // SB_KV_HEADS
    hd = SB_HEAD_DIM

    @pl.when(g == 0)
    def _():
        later_ref[...] = jnp.zeros(later_ref.shape, F32)
        acc_ref[...] = jnp.zeros(acc_ref.shape, F32)

    qbd = qbd_ref[0]
    tri_ones = tri_ref[...]
    later = later_ref[...]
    acc = acc_ref[...]
    for j in range(n_pp):
        z = _dot_nt(qbd, k_refs[j][0].astype(BF16))
        d_out, total = _sb_tile(z, later, v_refs[j][0].astype(BF16), tri_ones, None)
        acc = acc + d_out
        later = later + total
    later_ref[...] = later
    acc_ref[...] = acc

    @pl.when(g == pl.num_programs(1) - 1)
    def _():
        rid = lax.broadcasted_iota(I32, (SB_HEADS, hd), 0) // n_rep
        o = jnp.zeros((SB_HEADS, hd), F32)
        for kvh in range(SB_KV_HEADS):
            o = jnp.where(rid == kvh, acc[:, kvh * hd:(kvh + 1) * hd], o)
        o_ref[0] = o.astype(o_ref.dtype)


def _sb_paged(page_table, qbd, k_pool, v_pool):
    bd, n_pages = page_table.shape
    n_pp = math.gcd(PAGES_PER_STEP, n_pages)
    nkv = SB_KV_HEADS * SB_HEAD_DIM

    def page_map(j):
        return lambda b, g, pt: (pt[b, n_pages - 1 - (g * n_pp + j)], 0, 0)

    per_b = lambda b, g, pt: (b, 0, 0)
    in_specs = [pl.BlockSpec((1, SB_HEADS, nkv), per_b), pl.BlockSpec((SB_BLOCK, 2 * SB_BLOCK), lambda b, g, pt: (0, 0))]
    in_specs += [pl.BlockSpec((1, PAGE_SIZE, nkv), page_map(j)) for j in range(n_pp)] * 2
    return pl.pallas_call(
        functools.partial(_sb_paged_kernel, n_pp=n_pp),
        grid_spec=pltpu.PrefetchScalarGridSpec(
            num_scalar_prefetch=1, grid=(bd, n_pages // n_pp), in_specs=in_specs,
            out_specs=pl.BlockSpec((1, SB_HEADS, SB_HEAD_DIM), per_b),
            scratch_shapes=[pltpu.VMEM((SB_HEADS, SB_BLOCK), F32), pltpu.VMEM((SB_HEADS, nkv), F32)]),
        out_shape=jax.ShapeDtypeStruct((bd, SB_HEADS, SB_HEAD_DIM), BF16),
        compiler_params=_cparams(("parallel", "arbitrary")),
        name="sb_paged",
    )(page_table, qbd, _suffix_matrix(), *([k_pool] * n_pp), *([v_pool] * n_pp))


def _lane_min_where(mask, lane, big):
    return jnp.min(jnp.where(mask, lane, big), axis=-1, keepdims=True)


def _oproj_route_kernel(x_ref, o_ref, wo_ref, gffn_ref, wrh_ref, wrl_ref, br_ref, xmid_ref, hn_ref, eid_ref, gate_ref):
    x_mid = x_ref[...] + _dot(o_ref[...], wo_ref[...])
    xmid_ref[...] = x_mid
    hn = _rms(x_mid, gffn_ref[...])
    hn_ref[...] = hn
    hh = hn.astype(BF16)
    hl = (hn - hh.astype(F32)).astype(BF16)
    wrh = wrh_ref[...]
    logits = _dot(hh, wrh) + _dot(hl, wrh) + _dot(hh, wrl_ref[...]) + br_ref[...]
    lane_i = lax.broadcasted_iota(I32, logits.shape, 1)
    lane = lane_i.astype(F32)
    big = float(ROUTER_LANES)
    is_grp = lane < N_GROUPS
    lg = jnp.where(is_grp, logits, NEG_BIG)
    mg = lg.max(axis=-1, keepdims=True)
    grp = _lane_min_where(is_grp & (lg == mg), lane, big)
    p_grp = 1.0 / jnp.sum(jnp.where(is_grp, jnp.exp(lg - mg), 0.0), axis=-1, keepdims=True)
    lo = N_GROUPS + grp * EXPERTS_PER_GROUP
    in_grp = (lane >= lo) & (lane < lo + EXPERTS_PER_GROUP)
    lf = jnp.where(in_grp, logits, NEG_BIG)
    v1 = lf.max(axis=-1, keepdims=True)
    i1 = _lane_min_where(in_grp & (lf == v1), lane, big)
    rest = in_grp & (lane != i1)
    lf2 = jnp.where(rest, logits, NEG_BIG)
    v2 = lf2.max(axis=-1, keepdims=True)
    i2 = _lane_min_where(rest & (lf2 == v2), lane, big)
    e2 = jnp.exp(v2 - v1)
    g1 = p_grp / (1.0 + e2)
    g2 = p_grp * e2 / (1.0 + e2)
    eid_ref[...] = jnp.where(lane_i == 0, i1 - N_GROUPS, jnp.where(lane_i == 1, i2 - N_GROUPS, 0.0)).astype(I32)
    gate_ref[...] = jnp.where(lane_i == 0, g1, jnp.where(lane_i == 1, g2, 0.0))


def _router_weights(w_group, b_group, w_fine, b_fine):
    d = w_group.shape[0]
    pad = ROUTER_LANES - N_GROUPS - N_EXPERTS
    w = jnp.concatenate([w_group, w_fine, jnp.zeros((d, pad), F32)], axis=1)
    b = jnp.concatenate([b_group, b_fine, jnp.zeros((pad,), F32)]).reshape(1, ROUTER_LANES)
    w_hi = w.astype(BF16)
    w_lo = (w - w_hi.astype(F32)).astype(BF16)
    return w_hi, w_lo, b


def _oproj_route(x, o, w_o, g_ffn, router, tm):
    t, d = x.shape
    kd = o.shape[1]
    w_hi, w_lo, b = router
    row = lambda i: (i, 0)
    return pl.pallas_call(
        _oproj_route_kernel,
        grid=(t // tm,),
        in_specs=[pl.BlockSpec((tm, d), row), pl.BlockSpec((tm, kd), row), _full_spec((kd, d)), _full_spec((1, d)),
                  _full_spec(w_hi.shape), _full_spec(w_lo.shape), _full_spec(b.shape)],
        out_specs=[pl.BlockSpec((tm, d), row), pl.BlockSpec((tm, d), row), pl.BlockSpec((tm, ROUTER_LANES), row),
                   pl.BlockSpec((tm, ROUTER_LANES), row)],
        out_shape=[jax.ShapeDtypeStruct((t, d), F32), jax.ShapeDtypeStruct((t, d), F32),
                   jax.ShapeDtypeStruct((t, ROUTER_LANES), I32), jax.ShapeDtypeStruct((t, ROUTER_LANES), F32)],
        compiler_params=_cparams(("parallel",)),
        name="oproj_route",
    )(x, o, w_o, g_ffn.reshape(1, d), w_hi, w_lo, b)


def _moe_kernel(be_ref, tok_ref, dst_ref, nact_ref, gate_ref, hn_hbm, wg_ref, wu_ref, wd_ref, y_hbm,
                xbuf, obuf, wgb, wub, wdb, gsem, ssem):
    i = pl.program_id(0)
    nb = pl.num_programs(0)
    blk = MOE_SLOT_BLOCK
    n_act = nact_ref[0]
    slot = i % 2

    def gather(block, buf, start):
        def one(r, _):
            cp = pltpu.make_async_copy(hn_hbm.at[pl.ds(tok_ref[block * blk + r], 1)], xbuf.at[buf, pl.ds(r, 1)], gsem.at[buf])
            if start:
                cp.start()
            else:
                cp.wait()
            return 0
        lax.fori_loop(0, blk, one, 0)

    def scatter(block, buf, start):
        def one(r, _):
            dst = dst_ref[block * blk + r]

            @pl.when(dst >= 0)
            def _():
                cp = pltpu.make_async_copy(obuf.at[buf, pl.ds(r, 1)], y_hbm.at[pl.ds(dst, 1)], ssem.at[buf])
                if start:
                    cp.start()
                else:
                    cp.wait()
            return 0
        lax.fori_loop(0, blk, one, 0)

    @pl.when((i == 0) & (n_act > 0))
    def _():
        gather(0, 0, True)

    @pl.when(i + 1 < n_act)
    def _():
        gather(i + 1, 1 - slot, True)

    @pl.when((i >= 2) & (i - 2 < n_act))
    def _():
        scatter(i - 2, slot, False)

    @pl.when(i < n_act)
    def _():
        @pl.when((i == 0) | (be_ref[i] != be_ref[jnp.maximum(i - 1, 0)]))
        def _():
            wgb[...] = wg_ref[0].astype(BF16)
            wub[...] = wu_ref[0].astype(BF16)
            wdb[...] = wd_ref[0].astype(BF16)

        gather(i, slot, False)
        xb = xbuf[slot].astype(BF16)
        gt = _dot(xb, wgb[...])
        up = _dot(xb, wub[...])
        hmid = (gt * (1.0 / (1.0 + jnp.exp(-gt))) * up).astype(BF16)
        obuf[slot] = _dot(hmid, wdb[...]) * gate_ref[...]
        scatter(i, slot, True)

    @pl.when(i == nb - 1)
    def _():
        @pl.when((i - 1 >= 0) & (i - 1 < n_act))
        def _():
            scatter(i - 1, 1 - slot, False)

        @pl.when(i < n_act)
        def _():
            scatter(i, slot, False)


def _moe_experts(block_e, slot_tok, slot_dst, n_act, slot_gate, hn, w_gate, w_up, w_down, n_rows_out):
    n_slots = slot_tok.shape[0]
    blk = MOE_SLOT_BLOCK
    n_blocks = n_slots // blk
    d = hn.shape[1]
    de = w_gate.shape[2]
    wmap = lambda i, be, tok, dst, na: (be[i], 0, 0)
    return pl.pallas_call(
        _moe_kernel,
        grid_spec=pltpu.PrefetchScalarGridSpec(
            num_scalar_prefetch=4, grid=(n_blocks,),
            in_specs=[pl.BlockSpec((blk, 1), lambda i, *_: (i, 0)), pl.BlockSpec(memory_space=pl.ANY),
                      pl.BlockSpec((1, d, de), wmap), pl.BlockSpec((1, d, de), wmap), pl.BlockSpec((1, de, d), wmap)],
            out_specs=pl.BlockSpec(memory_space=pl.ANY),
            scratch_shapes=[pltpu.VMEM((2, blk, d), F32), pltpu.VMEM((2, blk, d), F32), pltpu.VMEM((d, de), BF16),
                            pltpu.VMEM((d, de), BF16), pltpu.VMEM((de, d), BF16),
                            pltpu.SemaphoreType.DMA((2,)), pltpu.SemaphoreType.DMA((2,))]),
        out_shape=jax.ShapeDtypeStruct((n_rows_out, d), F32),
        compiler_params=_cparams(("arbitrary",)),
        name="moe_experts",
    )(block_e, slot_tok, slot_dst, n_act, slot_gate, hn, w_gate, w_up, w_down)


def _moe_slots(eid, gate, t_pad):
    n_tok = eid.shape[0]
    blk = MOE_SLOT_BLOCK
    n_asg = 2 * n_tok
    flat_e = eid.reshape(n_asg)
    onehot = (flat_e[:, None] == jnp.arange(N_EXPERTS, dtype=I32)[None, :]).astype(I32)
    csum = jnp.cumsum(onehot, axis=0)
    counts = csum[-1]
    padded = (counts + blk - 1) // blk * blk
    pad_end = jnp.cumsum(padded)
    pad_start = pad_end - padded
    dest = jnp.sum(onehot * (pad_start[None, :] + csum - 1), axis=1)
    n_blocks = -(-n_asg // blk) + N_EXPERTS
    n_slots = n_blocks * blk
    asg = jnp.arange(n_asg, dtype=I32)
    tok = asg // 2
    slot_tok = jnp.zeros((n_slots,), I32).at[dest].set(tok)
    slot_dst = jnp.full((n_slots,), -1, I32).at[dest].set((asg % 2) * t_pad + tok)
    slot_gate = jnp.zeros((n_slots,), F32).at[dest].set(gate.reshape(n_asg))
    n_act = (pad_end[-1] // blk).astype(I32)
    blk_start = jnp.arange(n_blocks, dtype=I32) * blk
    block_e = jnp.minimum(jnp.searchsorted(pad_end, blk_start, side='right'), N_EXPERTS - 1).astype(I32)
    last_e = block_e[jnp.maximum(n_act - 1, 0)]
    block_e = jnp.where(jnp.arange(n_blocks) < n_act, block_e, last_e)
    return block_e, slot_tok, slot_dst, n_act.reshape(1), slot_gate.reshape(n_slots, 1)


def _combine_ple_kernel(x_ref, ya_ref, yb_ref, p_ref, g_ref, wg_ref, wp_ref, gfin_ref, o_ref, *, final):
    x1 = x_ref[...] + (ya_ref[0] + yb_ref[0])
    gate = _dot(_rms(x1, g_ref[...]).astype(BF16), wg_ref[...])
    gate = 1.0 / (1.0 + jnp.exp(-gate))
    out = x1 + gate * _dot(p_ref[...].astype(BF16), wp_ref[...])
    if final:
        out = _rms(out, gfin_ref[...])
    o_ref[...] = out


def _combine_ple(x_mid, y2, row_block0, p, g_ple, w_gate, w_proj, g_final, final, tm):
    t, d = x_mid.shape
    pd = p.shape[1]
    row = lambda i: (i, 0)
    return pl.pallas_call(
        functools.partial(_combine_ple_kernel, final=final),
        grid=(t // tm,),
        in_specs=[pl.BlockSpec((tm, d), row), pl.BlockSpec((1, tm, d), lambda i: (0, row_block0 + i, 0)),
                  pl.BlockSpec((1, tm, d), lambda i: (1, row_block0 + i, 0)), pl.BlockSpec((tm, pd), row),
                  _full_spec((1, d)), _full_spec((d, d)), _full_spec((pd, d)), _full_spec((1, d))],
        out_specs=pl.BlockSpec((tm, d), row),
        out_shape=jax.ShapeDtypeStruct((t, d), F32),
        compiler_params=_cparams(("parallel",)),
        name="combine_ple",
    )(x_mid, y2, y2, p, g_ple.reshape(1, d), w_gate, w_proj, g_final.reshape(1, d))


def kernel(x_prompt, x_sample, cache_mla_ckv, cache_mla_kpe, cache_swa_k, cache_swa_v, cache_sb_k, cache_sb_v, page_table, p_prompt, p_sample, g_mix, g_ffn, g_ple, g_final, mla_w_in, mla_g_q, mla_g_kv, mla_w_uq, mla_w_uk, mla_w_uv, mla_w_o, swa_w_in, swa_sinks, swa_w_o, sb_w_in, sb_w_o, moe_w_group, moe_b_group, moe_w_fine, moe_b_fine, moe_w_gate, moe_w_up, moe_w_down, ple_w_gate, ple_w_proj):
    batch, seq, d = x_prompt.shape
    bd, dec_seq, _ = x_sample.shape
    assert dec_seq == 1
    depth = g_mix.shape[0]
    n_pages = page_table.shape[1]
    past_len = n_pages * PAGE_SIZE
    n_pool = cache_mla_ckv.shape[1]
    tp = batch * seq
    tm_p = _row_tile(tp, 256)
    tm_s = bd
    assert tp % bd == 0 and seq % tm_p == 0
    t_all = tp + bd
    page_table = page_table.astype(I32)

    xp = x_prompt.reshape(tp, d)
    xs = x_sample.reshape(bd, d)
    pos_p = jnp.arange(seq)
    pos_s = jnp.full((bd,), past_len)
    rope_p = _rope_tables(pos_p, 32, 128)
    rope_s = _rope_tables(pos_s, 32, 128)

    outs = {k: [] for k in ("p_ckv", "p_kpe", "s_ckv", "s_kpe", "p_wk", "p_wv", "s_wk", "s_wv", "p_sk", "p_sv", "s_sk", "s_sv")}
    for i in range(depth):
        kind, j = i % 3, i // 3
        if kind == 0:
            weights = _mla_weights(mla_w_in[j], mla_w_uq[j], mla_w_uk[j], mla_w_uv[j])
            q_p, kk_p, v_p, ckv_p, kpe_p = _mla_proj(xp, g_mix[i], mla_g_q[j], mla_g_kv[j], weights, *rope_p, seq // tm_p, tm_p)
            q_s, _, _, ckv_s, kpe_s = _mla_proj(xs, g_mix[i], mla_g_q[j], mla_g_kv[j], weights, *rope_s, 1, tm_s)
            o_p = _mla_flash(q_p, kk_p, v_p, batch, seq, _row_tile(seq, 512))
            q3 = q_s.reshape(bd, MLA_HEADS, MLA_QK_PAD)
            w_uk_t = mla_w_uk[j].transpose(1, 2, 0).astype(BF16)
            qlat = _bmm(q3[:, :, 128:].transpose(1, 0, 2), w_uk_t, BF16).transpose(1, 0, 2)
            o_lat = _mla_paged(page_table, qlat, q3[:, :, :MLA_ROPE], ckv_s.reshape(bd, 1, -1), kpe_s.reshape(bd, 1, -1),
                               cache_mla_ckv[j], cache_mla_kpe[j])
            w_uv_h = mla_w_uv[j].transpose(1, 0, 2).astype(BF16)
            o_s = _bmm(o_lat.transpose(1, 0, 2), w_uv_h, BF16).transpose(1, 0, 2).reshape(bd, MLA_HEADS * MLA_V)
            w_o = mla_w_o[j]
            outs["p_ckv"].append(ckv_p.reshape(batch, seq, -1)); outs["p_kpe"].append(kpe_p.reshape(batch, seq, -1))
            outs["s_ckv"].append(ckv_s.reshape(bd, 1, -1)); outs["s_kpe"].append(kpe_s.reshape(bd, 1, -1))
        elif kind == 1:
            sinks2 = swa_sinks[j] * LOG2E
            q_p, k_p, v_p = _swa_proj(xp, g_mix[i], swa_w_in[j], *rope_p, seq // tm_p, tm_p)
            q_s, k_s, v_s = _swa_proj(xs, g_mix[i], swa_w_in[j], *rope_s, 1, tm_s)
            o_p = _swa_prompt(sinks2, q_p, k_p, v_p, batch, seq)
            nkv = SWA_KV_HEADS * SWA_HEAD_DIM
            kwin = cache_swa_k[j].reshape(bd, WINDOW, nkv)
            vwin = cache_swa_v[j].reshape(bd, WINDOW, nkv)
            qbd = _block_diag_q(q_s, SWA_HEADS, SWA_KV_HEADS, SWA_HEAD_DIM)
            o_s = _swa_sample(sinks2, qbd, kwin, vwin, k_s.reshape(bd, 1, nkv), v_s.reshape(bd, 1, nkv))
            o_s = o_s.reshape(bd, SWA_HEADS * SWA_HEAD_DIM)
            w_o = swa_w_o[j]
            k4 = k_p.reshape(batch, seq, SWA_KV_HEADS, SWA_HEAD_DIM)
            v4 = v_p.reshape(batch, seq, SWA_KV_HEADS, SWA_HEAD_DIM)
            outs["p_wk"].append(k4[:, seq - WINDOW:]); outs["p_wv"].append(v4[:, seq - WINDOW:])
            outs["s_wk"].append(jnp.concatenate([cache_swa_k[j][:, 1:], k_s.reshape(bd, 1, SWA_KV_HEADS, SWA_HEAD_DIM)], axis=1))
            outs["s_wv"].append(jnp.concatenate([cache_swa_v[j][:, 1:], v_s.reshape(bd, 1, SWA_KV_HEADS, SWA_HEAD_DIM)], axis=1))
        else:
            q_p, k_p, v_p, kb_p, vb_p = _sb_proj(xp, g_mix[i], sb_w_in[j], tm_p)
            q_s, k_s, v_s, _, _ = _sb_proj(xs, g_mix[i], sb_w_in[j], tm_s)
            o_p = _sb_prompt(q_p, kb_p, vb_p, batch, seq)
            nkv = SB_KV_HEADS * SB_HEAD_DIM
            qbd = _block_diag_q(q_s, SB_HEADS, SB_KV_HEADS, SB_HEAD_DIM)
            o_s = _sb_paged(page_table, qbd, cache_sb_k[j].reshape(n_pool, PAGE_SIZE, nkv),
                            cache_sb_v[j].reshape(n_pool, PAGE_SIZE, nkv)).reshape(bd, SB_HEADS * SB_HEAD_DIM)
            w_o = sb_w_o[j]
            outs["p_sk"].append(k_p.reshape(batch, seq, SB_KV_HEADS, SB_HEAD_DIM))
            outs["p_sv"].append(v_p.reshape(batch, seq, SB_KV_HEADS, SB_HEAD_DIM))
            outs["s_sk"].append(k_s.reshape(bd, 1, SB_KV_HEADS, SB_HEAD_DIM))
            outs["s_sv"].append(v_s.reshape(bd, 1, SB_KV_HEADS, SB_HEAD_DIM))

        w_o = w_o.astype(BF16)
        router = _router_weights(moe_w_group[i], moe_b_group[i], moe_w_fine[i], moe_b_fine[i])
        xmid_p, hn_p, eid_p, gate_p = _oproj_route(xp, o_p, w_o, g_ffn[i], router, tm_p)
        xmid_s, hn_s, eid_s, gate_s = _oproj_route(xs, o_s, w_o, g_ffn[i], router, tm_s)
        hn = jnp.concatenate([hn_p, hn_s], axis=0)
        eid = jnp.concatenate([eid_p[:, :2], eid_s[:, :2]], axis=0)
        gate = jnp.concatenate([gate_p[:, :2], gate_s[:, :2]], axis=0)
        slots = _moe_slots(eid, gate, t_all)
        y2 = _moe_experts(*slots, hn, moe_w_gate[i], moe_w_up[i], moe_w_down[i], 2 * t_all).reshape(2, t_all, d)
        wg = ple_w_gate[i].astype(BF16)
        wp = ple_w_proj[i].astype(BF16)
        final = i == depth - 1
        xp = _combine_ple(xmid_p, y2, 0, p_prompt[i].reshape(tp, -1), g_ple[i], wg, wp, g_final, final, tm_p)
        xs = _combine_ple(xmid_s, y2, tp // bd, p_sample[i].reshape(bd, -1), g_ple[i], wg, wp, g_final, final, tm_s)

    st = lambda k: jnp.stack(outs[k])
    return (xp.reshape(batch, seq, d), xs.reshape(bd, 1, d),
            st("p_ckv"), st("p_kpe"), st("p_wk"), st("p_wv"), st("p_sk"), st("p_sv"),
            st("s_ckv"), st("s_kpe"), st("s_wk"), st("s_wv"), st("s_sk"), st("s_sv"))
```

```python
import functools
import math

import jax
import jax.numpy as jnp
from jax import lax
from jax.experimental import pallas as pl
from jax.experimental.pallas import tpu as pltpu

F32 = jnp.float32
BF16 = jnp.bfloat16
I32 = jnp.int32

NORM_EPS = 1e-6
ROPE_THETA = 10000.0
PAGE_SIZE = 128

MLA_HEADS = 8
MLA_NOPE = 128
MLA_ROPE = 64
MLA_V = 128
MLA_Q_RANK = 512
MLA_KV_RANK = 256
MLA_QK_PAD = 256

SWA_HEADS = 16
SWA_KV_HEADS = 4
SWA_HEAD_DIM = 64
WINDOW = 128

SB_HEADS = 8
SB_KV_HEADS = 4
SB_HEAD_DIM = 128
SB_BLOCK = 128
SB_DEAD_LOG = -100.0

N_GROUPS = 4
EXPERTS_PER_GROUP = 8
N_EXPERTS = N_GROUPS * EXPERTS_PER_GROUP
D_EXPERT = 512
MOE_SLOT_BLOCK = 256
ROUTER_LANES = 128

LOG2E = 1.4426950408889634
NEG_BIG = -1e30
VMEM_LIMIT = 56 * 1024 * 1024
PAGES_PER_STEP = 16


def _cparams(sem, vmem=VMEM_LIMIT):
    return pltpu.CompilerParams(dimension_semantics=sem, vmem_limit_bytes=vmem)


def _rms(x, g):
    return x * lax.rsqrt(jnp.mean(x * x, axis=-1, keepdims=True) + NORM_EPS) * g


def _dot(a, b):
    return jnp.dot(a, b, preferred_element_type=F32)


def _dot_nt(a, b):
    return lax.dot_general(a, b, (((1,), (1,)), ((), ())), preferred_element_type=F32)


def _full_spec(shape):
    nd = len(shape)
    return pl.BlockSpec(shape, lambda *_: (0,) * nd)


def _row_tile(n, pref):
    return pref if n % pref == 0 else n


def _rope_tables(pos, half, lanes):
    inv = jnp.exp(-math.log(ROPE_THETA) * jnp.arange(half, dtype=F32) / half)
    ang = pos.astype(F32)[:, None] * inv[None, :]
    cos, sin = jnp.cos(ang), jnp.sin(ang)
    reps = lanes // (2 * half)
    if reps >= 1:
        return jnp.tile(jnp.concatenate([cos, cos], 1), (1, reps)), jnp.tile(jnp.concatenate([sin, sin], 1), (1, reps))
    pad = lanes - 2 * half
    n = pos.shape[0]
    return (jnp.concatenate([cos, cos, jnp.ones((n, pad), F32)], 1),
            jnp.concatenate([sin, sin, jnp.zeros((n, pad), F32)], 1))


def _rot_cols(w, head_dim):
    k, n = w.shape
    w3 = w.reshape(k, n // head_dim, head_dim)
    half = head_dim // 2
    return jnp.concatenate([-w3[..., half:], w3[..., :half]], axis=-1).reshape(k, n)


def _mla_proj_kernel(x_ref, gmix_ref, win_ref, gq_ref, gkv_ref, cos_ref, sin_ref, wq_ref, wqr_ref, wuk_ref, wuv_ref,
                     q_ref, kk_ref, v_ref, ckv_ref, kpe_ref, *, qscale):
    h = _rms(x_ref[...], gmix_ref[...]).astype(BF16)
    proj = _dot(h, win_ref[...])
    c_q = _rms(proj[:, :MLA_Q_RANK], gq_ref[...]).astype(BF16)
    c_kv = _rms(proj[:, MLA_Q_RANK:MLA_Q_RANK + MLA_KV_RANK], gkv_ref[...])
    ckv_ref[...] = c_kv
    cos = cos_ref[...]
    sin = sin_ref[...]
    o = MLA_Q_RANK + MLA_KV_RANK
    kpe = proj[:, o:o + 128] * cos + proj[:, o + 128:o + 256] * sin
    kpe_ref[...] = kpe[:, :MLA_ROPE]
    kpe_b = kpe.astype(BF16)
    ckv_b = c_kv.astype(BF16)
    kn = _dot(ckv_b, wuk_ref[...])
    vv = _dot(ckv_b, wuv_ref[...]).astype(BF16)
    ones = jnp.ones((vv.shape[0], MLA_V), BF16)
    for hd in range(MLA_HEADS):
        v_ref[:, 2 * hd * MLA_V:(2 * hd + 1) * MLA_V] = vv[:, hd * MLA_V:(hd + 1) * MLA_V]
        v_ref[:, (2 * hd + 1) * MLA_V:(2 * hd + 2) * MLA_V] = ones
    qe = _dot(c_q, wq_ref[...])
    qr = _dot(c_q, wqr_ref[...])
    for hd in range(MLA_HEADS):
        a = hd * MLA_QK_PAD
        pe = qe[:, a:a + 128] * cos + qr[:, hd * 128:(hd + 1) * 128] * sin
        q_ref[:, a:a + 128] = (pe * qscale).astype(BF16)
        q_ref[:, a + 128:a + 256] = (qe[:, a + 128:a + 256] * qscale).astype(BF16)
        kk_ref[:, a:a + 128] = kpe_b
        kk_ref[:, a + 128:a + 256] = kn[:, hd * 128:(hd + 1) * 128].astype(BF16)


def _mla_weights(w_in, w_uq, w_uk, w_uv):
    d = w_in.shape[0]
    o = MLA_Q_RANK + MLA_KV_RANK
    w_kpe = w_in[:, o:]
    z64 = jnp.zeros((d, 128 - MLA_ROPE), F32)
    win = jnp.concatenate([w_in[:, :o], w_kpe, z64, _rot_cols(w_kpe, MLA_ROPE), z64], axis=1).astype(BF16)
    r = w_uq.shape[0]
    w_nope = w_uq[:, :, :MLA_NOPE]
    w_pe = w_uq[:, :, MLA_NOPE:]
    zq = jnp.zeros((r, MLA_HEADS, 128 - MLA_ROPE), F32)
    wq = jnp.concatenate([w_pe, zq, w_nope], axis=-1).reshape(r, MLA_HEADS * MLA_QK_PAD).astype(BF16)
    w_pe_rot = _rot_cols(w_pe.reshape(r, MLA_HEADS * MLA_ROPE), MLA_ROPE).reshape(r, MLA_HEADS, MLA_ROPE)
    wqr = jnp.concatenate([w_pe_rot, zq], axis=-1).reshape(r, MLA_HEADS * 128).astype(BF16)
    wuk = w_uk.reshape(MLA_KV_RANK, MLA_HEADS * MLA_NOPE).astype(BF16)
    wuv = w_uv.reshape(MLA_KV_RANK, MLA_HEADS * MLA_V).astype(BF16)
    return win, wq, wqr, wuk, wuv


def _mla_proj(x, g_mix, g_q, g_kv, weights, cos, sin, n_pos_blocks, tm):
    win, wq, wqr, wuk, wuv = weights
    t, d = x.shape
    qscale = (MLA_NOPE + MLA_ROPE) ** -0.5 * LOG2E
    row = lambda i: (i, 0)
    pos = lambda i: (i % n_pos_blocks, 0)
    hq = MLA_HEADS * MLA_QK_PAD
    return pl.pallas_call(
        functools.partial(_mla_proj_kernel, qscale=qscale),
        grid=(t // tm,),
        in_specs=[pl.BlockSpec((tm, d), row), _full_spec((1, d)), _full_spec(win.shape), _full_spec((1, MLA_Q_RANK)),
                  _full_spec((1, MLA_KV_RANK)), pl.BlockSpec((tm, 128), pos), pl.BlockSpec((tm, 128), pos),
                  _full_spec(wq.shape), _full_spec(wqr.shape), _full_spec(wuk.shape), _full_spec(wuv.shape)],
        out_specs=[pl.BlockSpec((tm, hq), row), pl.BlockSpec((tm, hq), row), pl.BlockSpec((tm, 2 * MLA_HEADS * MLA_V), row),
                   pl.BlockSpec((tm, MLA_KV_RANK), row), pl.BlockSpec((tm, MLA_ROPE), row)],
        out_shape=[jax.ShapeDtypeStruct((t, hq), BF16), jax.ShapeDtypeStruct((t, hq), BF16),
                   jax.ShapeDtypeStruct((t, 2 * MLA_HEADS * MLA_V), BF16), jax.ShapeDtypeStruct((t, MLA_KV_RANK), F32),
                   jax.ShapeDtypeStruct((t, MLA_ROPE), F32)],
        compiler_params=_cparams(("parallel",)),
        name="mla_proj",
    )(x, g_mix.reshape(1, d), win, g_q.reshape(1, -1), g_kv.reshape(1, -1), cos, sin, wq, wqr, wuk, wuv)


def _mla_flash_kernel(q_ref, k_ref, v_ref, o_ref, *, tq, tk, nh):
    qi = pl.program_id(2)
    qk, dv = MLA_QK_PAD, MLA_V
    qs = [q_ref[:, h * qk:(h + 1) * qk] for h in range(nh)]

    def scores(h, j):
        return _dot_nt(qs[h], k_ref[pl.ds(pl.multiple_of(j * tk, tk), tk), h * qk:(h + 1) * qk])

    def update(h, j, s, m, acc):
        m_new = jnp.maximum(m, s.max(axis=-1, keepdims=True))
        p = jnp.exp2(s - m_new).astype(BF16)
        vj = v_ref[pl.ds(pl.multiple_of(j * tk, tk), tk), 2 * h * dv:2 * (h + 1) * dv]
        return m_new, acc * jnp.exp2(m - m_new) + _dot(p, vj)

    last = (qi * tq) // tk

    def body(j, carry):
        nxt = [scores(h, j + 1) for h in range(nh)]
        return tuple(update(h, j, carry[h][2], *carry[h][:2]) + (nxt[h],) for h in range(nh))

    init = tuple((jnp.full((tq, 1), NEG_BIG, F32), jnp.zeros((tq, 2 * dv), F32), scores(h, 0)) for h in range(nh))
    carry = lax.fori_loop(0, last, body, init)
    row = qi * tq + lax.broadcasted_iota(I32, (tq, tk), 0)
    col = last * tk + lax.broadcasted_iota(I32, (tq, tk), 1)
    for h in range(nh):
        m, acc, s = carry[h]
        _, acc = update(h, last, jnp.where(col <= row, s, NEG_BIG), m, acc)
        o_ref[:, h * dv:(h + 1) * dv] = (acc[:, :dv] / acc[:, dv:]).astype(BF16)


def _mla_flash(q, kk, v, batch, seq, tq, tk):
    nq = seq // tq
    nh = 2
    assert tk % tq == 0 and seq % tq == 0 and MLA_HEADS % nh == 0
    return pl.pallas_call(
        functools.partial(_mla_flash_kernel, tq=tq, tk=tk, nh=nh),
        grid=(batch, MLA_HEADS // nh, nq),
        in_specs=[pl.BlockSpec((tq, nh * MLA_QK_PAD), lambda b, h, i: (b * nq + i, h)),
                  pl.BlockSpec((seq, nh * MLA_QK_PAD), lambda b, h, i: (b, h)),
                  pl.BlockSpec((seq, 2 * nh * MLA_V), lambda b, h, i: (b, h))],
        out_specs=pl.BlockSpec((tq, nh * MLA_V), lambda b, h, i: (b * nq + i, h)),
        out_shape=jax.ShapeDtypeStruct((batch * seq, MLA_HEADS * MLA_V), BF16),
        compiler_params=_cparams(("parallel", "parallel", "arbitrary")),
        name="mla_flash",
    )(q, kk, v)


def _bmm_kernel(x_ref, w_ref, o_ref):
    o_ref[0] = _dot(x_ref[0], w_ref[0]).astype(o_ref.dtype)


def _bmm(x, w, out_dtype):
    h, m, k = x.shape
    n = w.shape[2]
    return pl.pallas_call(
        _bmm_kernel,
        grid=(h,),
        in_specs=[pl.BlockSpec((1, m, k), lambda i: (i, 0, 0)), pl.BlockSpec((1, k, n), lambda i: (i, 0, 0))],
        out_specs=pl.BlockSpec((1, m, n), lambda i: (i, 0, 0)),
        out_shape=jax.ShapeDtypeStruct((h, m, n), out_dtype),
        compiler_params=_cparams(("parallel",)),
        name="head_matmul",
    )(x, w)


def _mla_paged_kernel(pt_ref, qx_ref, cnew_ref, knew_ref, *rest, n_pp):
    ckv_refs = rest[:n_pp]
    kpe_refs = rest[n_pp:2 * n_pp]
    o_ref = rest[2 * n_pp]
    m_ref, l_ref, acc_ref, stage_ref = rest[2 * n_pp + 1:]
    g = pl.program_id(1)
    r = MLA_KV_RANK
    qx = qx_ref[0]

    @pl.when(g == 0)
    def _():
        cn = cnew_ref[0]
        qf = qx.astype(F32)
        s0 = (jnp.sum(qf[:, :r] * cn.astype(BF16).astype(F32), axis=-1, keepdims=True)
              + jnp.sum(qf[:, r:r + MLA_ROPE] * knew_ref[0].astype(BF16).astype(F32), axis=-1, keepdims=True))
        m_ref[...] = jnp.broadcast_to(s0, m_ref.shape)
        l_ref[...] = jnp.ones(l_ref.shape, F32)
        acc_ref[...] = jnp.broadcast_to(cn, acc_ref.shape)

    zpad = jnp.zeros((PAGE_SIZE, 128 - MLA_ROPE), BF16)
    for j in range(n_pp):
        rows = slice(j * PAGE_SIZE, (j + 1) * PAGE_SIZE)
        stage_ref[rows, :r] = ckv_refs[j][0, 0].astype(BF16)
        stage_ref[rows, r:] = jnp.concatenate([kpe_refs[j][0, 0].astype(BF16), zpad], axis=1)
    s = _dot_nt(qx, stage_ref[...])
    m_old = m_ref[:, :1]
    m_new = jnp.maximum(m_old, s.max(axis=-1, keepdims=True))
    corr = jnp.exp2(m_old - m_new)
    p = jnp.exp2(s - m_new)
    l_new = l_ref[:, :1] * corr + p.sum(axis=-1, keepdims=True)
    acc = acc_ref[...] * corr + _dot(p.astype(BF16), stage_ref[:, :r])
    acc_ref[...] = acc
    m_ref[...] = jnp.broadcast_to(m_new, m_ref.shape)
    l_ref[...] = jnp.broadcast_to(l_new, l_ref.shape)

    @pl.when(g == pl.num_programs(1) - 1)
    def _():
        o_ref[0] = (acc / l_new).astype(o_ref.dtype)


def _mla_paged(page_table, qx, c_new, k_new, ckv_pool, kpe_pool, layer):
    bd, n_pages = page_table.shape
    n_pp = math.gcd(PAGES_PER_STEP, n_pages)
    hds = qx.shape[1]
    kw = MLA_KV_RANK + 128

    def page_map(j):
        return lambda b, g, pt: (layer, pt[b, g * n_pp + j], 0, 0)

    per_b = lambda b, g, pt: (b, 0, 0)
    in_specs = [pl.BlockSpec((1, hds, kw), per_b), pl.BlockSpec((1, 1, MLA_KV_RANK), per_b),
                pl.BlockSpec((1, 1, MLA_ROPE), per_b)]
    in_specs += [pl.BlockSpec((1, 1, PAGE_SIZE, MLA_KV_RANK), page_map(j)) for j in range(n_pp)]
    in_specs += [pl.BlockSpec((1, 1, PAGE_SIZE, MLA_ROPE), page_map(j)) for j in range(n_pp)]
    return pl.pallas_call(
        functools.partial(_mla_paged_kernel, n_pp=n_pp),
        grid_spec=pltpu.PrefetchScalarGridSpec(
            num_scalar_prefetch=1, grid=(bd, n_pages // n_pp), in_specs=in_specs,
            out_specs=pl.BlockSpec((1, hds, MLA_KV_RANK), per_b),
            scratch_shapes=[pltpu.VMEM((hds, 128), F32), pltpu.VMEM((hds, 128), F32), pltpu.VMEM((hds, MLA_KV_RANK), F32),
                            pltpu.VMEM((n_pp * PAGE_SIZE, kw), BF16)]),
        out_shape=jax.ShapeDtypeStruct((bd, hds, MLA_KV_RANK), BF16),
        compiler_params=_cparams(("parallel", "arbitrary")),
        name="mla_paged",
    )(page_table, qx, c_new, k_new, *([ckv_pool] * n_pp), *([kpe_pool] * n_pp))


def _swa_proj_kernel(x_ref, gmix_ref, win_ref, cos_ref, sin_ref, q_ref, k_ref, v_ref, *, qscale):
    nq, nkv = SWA_HEADS * SWA_HEAD_DIM, SWA_KV_HEADS * SWA_HEAD_DIM
    h = _rms(x_ref[...], gmix_ref[...]).astype(BF16)
    proj = _dot(h, win_ref[...])
    cos = cos_ref[...]
    sin = sin_ref[...]
    r0 = nq + 2 * nkv
    for c in range(nq // 128):
        a = c * 128
        q_ref[:, a:a + 128] = ((proj[:, a:a + 128] * cos + proj[:, r0 + a:r0 + a + 128] * sin) * qscale).astype(BF16)
    for c in range(nkv // 128):
        a = c * 128
        k_ref[:, a:a + 128] = proj[:, nq + a:nq + a + 128] * cos + proj[:, r0 + nq + a:r0 + nq + a + 128] * sin
    v_ref[...] = proj[:, nq + nkv:nq + 2 * nkv]


def _swa_proj(x, g_mix, w_in, cos, sin, n_pos_blocks, tm):
    t, d = x.shape
    nq, nkv = SWA_HEADS * SWA_HEAD_DIM, SWA_KV_HEADS * SWA_HEAD_DIM
    win = jnp.concatenate([w_in, _rot_cols(w_in[:, :nq + nkv], SWA_HEAD_DIM)], axis=1).astype(BF16)
    row = lambda i: (i, 0)
    pos = lambda i: (i % n_pos_blocks, 0)
    return pl.pallas_call(
        functools.partial(_swa_proj_kernel, qscale=SWA_HEAD_DIM ** -0.5 * LOG2E),
        grid=(t // tm,),
        in_specs=[pl.BlockSpec((tm, d), row), _full_spec((1, d)), _full_spec(win.shape),
                  pl.BlockSpec((tm, 128), pos), pl.BlockSpec((tm, 128), pos)],
        out_specs=[pl.BlockSpec((tm, nq), row), pl.BlockSpec((tm, nkv), row), pl.BlockSpec((tm, nkv), row)],
        out_shape=[jax.ShapeDtypeStruct((t, nq), BF16), jax.ShapeDtypeStruct((t, nkv), F32),
                   jax.ShapeDtypeStruct((t, nkv), F32)],
        compiler_params=_cparams(("parallel",)),
        name="swa_proj",
    )(x, g_mix.reshape(1, d), win, cos, sin)


def _sink_column(sink_ref, rows_per_head, kvh, n_rep):
    rid = lax.broadcasted_iota(I32, (n_rep * rows_per_head, 1), 0) // rows_per_head
    col = jnp.zeros((n_rep * rows_per_head, 1), F32)
    for g in range(n_rep):
        col = jnp.where(rid == g, sink_ref[kvh * n_rep + g], col)
    return col


def _swa_prompt_kernel(sink_ref, q_ref, kc_ref, kp_ref, vc_ref, vp_ref, o_ref):
    n = pl.program_id(1)
    w = WINDOW
    n_rep = SWA_HEADS // SWA_KV_HEADS
    hd = SWA_HEAD_DIM
    q = q_ref[...]
    kc = kc_ref[...].astype(BF16)
    kp = kp_ref[...].astype(BF16)
    vc = vc_ref[...].astype(BF16)
    vp = vp_ref[...].astype(BF16)
    t = lax.broadcasted_iota(I32, (n_rep * w, 2 * w), 0) % w
    c = lax.broadcasted_iota(I32, (n_rep * w, 2 * w), 1)
    first_key = jnp.where(n > 0, 0, w)
    valid = (c > t) & (c <= t + w) & (c >= first_key)
    for kvh in range(SWA_KV_HEADS):
        ks = slice(kvh * hd, (kvh + 1) * hd)
        kk = jnp.concatenate([kp[:, ks], kc[:, ks]], axis=0)
        vv = jnp.concatenate([vp[:, ks], vc[:, ks]], axis=0)
        qg = jnp.concatenate([q[:, (kvh * n_rep + g) * hd:(kvh * n_rep + g + 1) * hd] for g in range(n_rep)], axis=0)
        s = jnp.where(valid, _dot_nt(qg, kk), NEG_BIG)
        sink = _sink_column(sink_ref, w, kvh, n_rep)
        m = jnp.maximum(s.max(axis=-1, keepdims=True), sink)
        p = jnp.exp2(s - m)
        denom = p.sum(axis=-1, keepdims=True) + jnp.exp2(sink - m)
        o = _dot(p.astype(BF16), vv) / denom
        for g in range(n_rep):
            a = (kvh * n_rep + g) * hd
            o_ref[:, a:a + hd] = o[g * w:(g + 1) * w].astype(BF16)


def _swa_prompt(sinks2, q, k, v, batch, seq):
    nb = seq // WINDOW
    nq, nkv = SWA_HEADS * SWA_HEAD_DIM, SWA_KV_HEADS * SWA_HEAD_DIM
    cur = lambda b, n: (b * nb + n, 0)
    prev = lambda b, n: (b * nb + jnp.maximum(n - 1, 0), 0)
    return pl.pallas_call(
        _swa_prompt_kernel,
        grid=(batch, nb),
        in_specs=[pl.BlockSpec(memory_space=pltpu.SMEM), pl.BlockSpec((WINDOW, nq), cur),
                  pl.BlockSpec((WINDOW, nkv), cur), pl.BlockSpec((WINDOW, nkv), prev),
                  pl.BlockSpec((WINDOW, nkv), cur), pl.BlockSpec((WINDOW, nkv), prev)],
        out_specs=pl.BlockSpec((WINDOW, nq), cur),
        out_shape=jax.ShapeDtypeStruct((batch * seq, nq), BF16),
        compiler_params=_cparams(("parallel", "parallel")),
        name="swa_prompt",
    )(sinks2, q, k, k, v, v)


def _swa_sample_kernel(sink_ref, qbd_ref, kw_ref, vw_ref, kn_ref, vn_ref, o_ref):
    n_rep = SWA_HEADS // SWA_KV_HEADS
    hd = SWA_HEAD_DIM
    qbd = qbd_ref[0]
    kw = kw_ref[0].astype(BF16)
    vw = vw_ref[0].astype(BF16)
    kn = kn_ref[0].astype(BF16).astype(F32)
    vn = vn_ref[0].astype(BF16).astype(F32)
    s = _dot_nt(qbd, kw)
    col = lax.broadcasted_iota(I32, s.shape, 1)
    s = jnp.where(col >= 1, s, NEG_BIG)
    s_new = jnp.sum(qbd.astype(F32) * kn, axis=-1, keepdims=True)
    sink = _sink_column(sink_ref, 1, 0, SWA_HEADS)
    m = jnp.maximum(jnp.maximum(s.max(axis=-1, keepdims=True), s_new), sink)
    p = jnp.exp2(s - m)
    p_new = jnp.exp2(s_new - m)
    denom = p.sum(axis=-1, keepdims=True) + p_new + jnp.exp2(sink - m)
    full = (_dot(p.astype(BF16), vw) + p_new.astype(BF16).astype(F32) * vn) / denom
    rid = lax.broadcasted_iota(I32, (SWA_HEADS, hd), 0) // n_rep
    o = jnp.zeros((SWA_HEADS, hd), F32)
    for kvh in range(SWA_KV_HEADS):
        o = jnp.where(rid == kvh, full[:, kvh * hd:(kvh + 1) * hd], o)
    o_ref[0] = o.astype(BF16)


def _swa_sample(sinks2, qbd, kwin, vwin, k_new, v_new):
    bd = qbd.shape[0]
    nkv = SWA_KV_HEADS * SWA_HEAD_DIM
    per_b = lambda b: (b, 0, 0)
    return pl.pallas_call(
        _swa_sample_kernel,
        grid=(bd,),
        in_specs=[pl.BlockSpec(memory_space=pltpu.SMEM), pl.BlockSpec((1, SWA_HEADS, nkv), per_b),
                  pl.BlockSpec((1, WINDOW, nkv), per_b), pl.BlockSpec((1, WINDOW, nkv), per_b),
                  pl.BlockSpec((1, 1, nkv), per_b), pl.BlockSpec((1, 1, nkv), per_b)],
        out_specs=pl.BlockSpec((1, SWA_HEADS, SWA_HEAD_DIM), per_b),
        out_shape=jax.ShapeDtypeStruct((bd, SWA_HEADS, SWA_HEAD_DIM), BF16),
        compiler_params=_cparams(("parallel",)),
        name="swa_sample",
    )(sinks2, qbd, kwin, vwin, k_new, v_new)


def _block_diag_q(q, n_heads, n_kv, hd):
    b = q.shape[0]
    q3 = q.reshape(b, n_heads, hd)
    owner = (jnp.arange(n_heads) // (n_heads // n_kv))[:, None] == jnp.arange(n_kv)[None, :]
    return jnp.where(owner[None, :, :, None], q3[:, :, None, :], jnp.zeros((), q.dtype)).reshape(b, n_heads, n_kv * hd)


def _sb_proj_kernel(x_ref, gmix_ref, win_ref, q_ref, k_ref, v_ref, kb_ref, vb_ref, *, qscale):
    nq, nkv = SB_HEADS * SB_HEAD_DIM, SB_KV_HEADS * SB_HEAD_DIM
    h = _rms(x_ref[...], gmix_ref[...]).astype(BF16)
    proj = _dot(h, win_ref[...])
    q_ref[...] = (proj[:, :nq] * qscale).astype(BF16)
    k = proj[:, nq:nq + nkv]
    v = proj[:, nq + nkv:]
    k_ref[...] = k
    v_ref[...] = v
    kb_ref[...] = k.astype(BF16)
    vb_ref[...] = v.astype(BF16)


def _sb_proj(x, g_mix, w_in, tm):
    t, d = x.shape
    nq, nkv = SB_HEADS * SB_HEAD_DIM, SB_KV_HEADS * SB_HEAD_DIM
    row = lambda i: (i, 0)
    return pl.pallas_call(
        functools.partial(_sb_proj_kernel, qscale=SB_HEAD_DIM ** -0.5),
        grid=(t // tm,),
        in_specs=[pl.BlockSpec((tm, d), row), _full_spec((1, d)), _full_spec(w_in.shape)],
        out_specs=[pl.BlockSpec((tm, nq), row)] + [pl.BlockSpec((tm, nkv), row)] * 4,
        out_shape=[jax.ShapeDtypeStruct((t, nq), BF16), jax.ShapeDtypeStruct((t, nkv), F32),
                   jax.ShapeDtypeStruct((t, nkv), F32), jax.ShapeDtypeStruct((t, nkv), BF16),
                   jax.ShapeDtypeStruct((t, nkv), BF16)],
        compiler_params=_cparams(("parallel",)),
        name="sb_proj",
    )(x, g_mix.reshape(1, d), w_in.astype(BF16))


def _suffix_matrix():
    r = jnp.arange(SB_BLOCK)
    tri = (r[:, None] > r[None, :]).astype(BF16)
    return jnp.concatenate([tri, jnp.ones((SB_BLOCK, SB_BLOCK), BF16)], axis=1)


def _sb_weights(z, later, tri_ones, strict):
    m = z.shape[0]
    log_1mb = -(jnp.maximum(z, 0.0) + jnp.log(1.0 + jnp.exp(-jnp.abs(z))))
    if strict is not None:
        log_1mb = jnp.where(strict, log_1mb, 0.0)
    hi = log_1mb.astype(BF16)
    lo = (log_1mb - hi.astype(F32)).astype(BF16)
    sums = _dot(jnp.concatenate([hi, lo], axis=0), tri_ones)
    sums = sums[:m] + sums[m:]
    suffix = sums[:, :SB_BLOCK]
    total = sums[:, SB_BLOCK:]
    a = jnp.exp(z + log_1mb + suffix + later)
    if strict is not None:
        a = jnp.where(strict, a, 0.0)
    return a.astype(BF16), total


def _sb_prompt_kernel(q_ref, k_ref, v_ref, tri_ref, o_ref):
    qi = pl.program_id(1)
    blk = SB_BLOCK
    n_rep = SB_HEADS // SB_KV_HEADS
    hd = SB_HEAD_DIM
    q = q_ref[...]
    qs = jnp.concatenate([q[:, h * hd:(h + 1) * hd] for h in range(SB_HEADS)], axis=0)
    tri_ones = tri_ref[...]
    rows = SB_HEADS * blk
    grp = n_rep * blk

    def tile(j, later, strict):
        start = pl.multiple_of(j * blk, blk)
        kt = k_ref[pl.ds(start, blk), :]
        vt = v_ref[pl.ds(start, blk), :]
        z = jnp.concatenate([_dot_nt(qs[kvh * grp:(kvh + 1) * grp], kt[:, kvh * hd:(kvh + 1) * hd])
                             for kvh in range(SB_KV_HEADS)], axis=0)
        a, total = _sb_weights(z, later, tri_ones, strict)
        d_out = jnp.concatenate([_dot(a[kvh * grp:(kvh + 1) * grp], vt[:, kvh * hd:(kvh + 1) * hd])
                                 for kvh in range(SB_KV_HEADS)], axis=0)
        return d_out, total

    t = lax.broadcasted_iota(I32, (rows, blk), 0) % blk
    c = lax.broadcasted_iota(I32, (rows, blk), 1)
    out, later = tile(qi, jnp.zeros((rows, blk), F32), c < t)

    def cond(carry):
        j, alive, _, _ = carry
        return (j >= 0) & (alive > SB_DEAD_LOG)

    def body(carry):
        j, _, later, out = carry
        d_out, total = tile(j, later, None)
        later = later + total
        return j - 1, jnp.max(later), later, out + d_out

    _, _, _, out = lax.while_loop(cond, body, (qi - 1, jnp.max(later), later, out))
    o_ref[...] = jnp.concatenate([out[h * blk:(h + 1) * blk] for h in range(SB_HEADS)], axis=1).astype(BF16)


def _sb_prompt(q, kb, vb, batch, seq):
    nq = seq // SB_BLOCK
    nqd, nkv = SB_HEADS * SB_HEAD_DIM, SB_KV_HEADS * SB_HEAD_DIM
    qmap = lambda b, i: (b * nq + i, 0)
    return pl.pallas_call(
        _sb_prompt_kernel,
        grid=(batch, nq),
        in_specs=[pl.BlockSpec((SB_BLOCK, nqd), qmap), pl.BlockSpec((seq, nkv), lambda b, i: (b, 0)),
                  pl.BlockSpec((seq, nkv), lambda b, i: (b, 0)), _full_spec((SB_BLOCK, 2 * SB_BLOCK))],
        out_specs=pl.BlockSpec((SB_BLOCK, nqd), qmap),
        out_shape=jax.ShapeDtypeStruct((batch * seq, nqd), BF16),
        compiler_params=_cparams(("parallel", "arbitrary")),
        name="sb_prompt",
    )(q, kb, vb, _suffix_matrix())


def _sb_paged_kernel(pt_ref, qbd_ref, tri_ref, k_hbm, v_hbm, o_ref, kbuf, vbuf, sem, *, layer, n_pages):
    b = pl.program_id(0)
    n_rep = SB_HEADS // SB_KV_HEADS
    hd = SB_HEAD_DIM
    base = (b % 2) * 2

    def copies(row, step, slot):
        page = pt_ref[row, n_pages - 1 - step]
        cps = []
        for kvh in range(SB_KV_HEADS):
            cps.append(pltpu.make_async_copy(k_hbm.at[layer, page, :, kvh, :], kbuf.at[slot, kvh], sem.at[0, slot]))
            cps.append(pltpu.make_async_copy(v_hbm.at[layer, page, :, kvh, :], vbuf.at[slot, kvh], sem.at[1, slot]))
        return cps

    def start(row, step, slot):
        for cp in copies(row, step, slot):
            cp.start()

    def wait(row, step, slot):
        for cp in copies(row, step, slot):
            cp.wait()

    @pl.when(b == 0)
    def _():
        start(0, 0, 0)

    @pl.when(b + 1 < pl.num_programs(0))
    def _():
        start(b + 1, 0, 2 - base)

    qbd = qbd_ref[0]
    tri_ones = tri_ref[...]

    def cond(carry):
        i, alive, _, _ = carry
        return (i < n_pages) & (alive > SB_DEAD_LOG)

    def body(carry):
        i, _, later, acc = carry
        slot = base + i % 2

        @pl.when(i + 1 < n_pages)
        def _():
            start(b, i + 1, base + (i + 1) % 2)

        wait(b, i, slot)
        z = _dot_nt(qbd[:, :hd], kbuf[slot, 0].astype(BF16))
        for kvh in range(1, SB_KV_HEADS):
            z = z + _dot_nt(qbd[:, kvh * hd:(kvh + 1) * hd], kbuf[slot, kvh].astype(BF16))
        a, total = _sb_weights(z, later, tri_ones, None)
        d_acc = jnp.concatenate([_dot(a, vbuf[slot, kvh].astype(BF16)) for kvh in range(SB_KV_HEADS)], axis=1)
        later = later + total
        return i + 1, jnp.max(later), later, acc + d_acc

    init = (jnp.int32(0), jnp.float32(0.0), jnp.zeros((SB_HEADS, SB_BLOCK), F32),
            jnp.zeros((SB_HEADS, SB_KV_HEADS * hd), F32))
    n_done, _, _, acc = lax.while_loop(cond, body, init)

    @pl.when(n_done < n_pages)
    def _():
        wait(b, n_done, base + n_done % 2)

    rid = lax.broadcasted_iota(I32, (SB_HEADS, hd), 0) // n_rep
    o = jnp.zeros((SB_HEADS, hd), F32)
    for kvh in range(SB_KV_HEADS):
        o = jnp.where(rid == kvh, acc[:, kvh * hd:(kvh + 1) * hd], o)
    o_ref[0] = o.astype(o_ref.dtype)


def _sb_paged(page_table, qbd, k_pool, v_pool, layer):
    bd, n_pages = page_table.shape
    nkv = SB_KV_HEADS * SB_HEAD_DIM
    page_shape = (4, SB_KV_HEADS, PAGE_SIZE, SB_HEAD_DIM)
    per_b = lambda b, pt: (b, 0, 0)
    return pl.pallas_call(
        functools.partial(_sb_paged_kernel, layer=layer, n_pages=n_pages),
        grid_spec=pltpu.PrefetchScalarGridSpec(
            num_scalar_prefetch=1, grid=(bd,),
            in_specs=[pl.BlockSpec((1, SB_HEADS, nkv), per_b), pl.BlockSpec((SB_BLOCK, 2 * SB_BLOCK), lambda b, pt: (0, 0)),
                      pl.BlockSpec(memory_space=pl.ANY), pl.BlockSpec(memory_space=pl.ANY)],
            out_specs=pl.BlockSpec((1, SB_HEADS, SB_HEAD_DIM), per_b),
            scratch_shapes=[pltpu.VMEM(page_shape, F32), pltpu.VMEM(page_shape, F32), pltpu.SemaphoreType.DMA((2, 4))]),
        out_shape=jax.ShapeDtypeStruct((bd, SB_HEADS, SB_HEAD_DIM), BF16),
        compiler_params=_cparams(("arbitrary",)),
        name="sb_paged",
    )(page_table, qbd, _suffix_matrix(), k_pool, v_pool)


def _lane_min_where(mask, lane, big):
    return jnp.min(jnp.where(mask, lane, big), axis=-1, keepdims=True)


def _oproj_route_kernel(x_ref, o_ref, wo_ref, gffn_ref, wrh_ref, wrl_ref, br_ref, base_ref, tri_ref,
                        xmid_ref, hn_ref, route_ref, gate_ref, cnt_out_ref, cnt_ref):
    @pl.when(pl.program_id(0) == 0)
    def _():
        cnt_ref[...] = base_ref[...]

    x_mid = x_ref[...] + _dot(o_ref[...], wo_ref[...])
    xmid_ref[...] = x_mid
    hn = _rms(x_mid, gffn_ref[...])
    hn_ref[...] = hn
    hh = hn.astype(BF16)
    hl = (hn - hh.astype(F32)).astype(BF16)
    wrh = wrh_ref[...]
    logits = _dot(hh, wrh) + _dot(hl, wrh) + _dot(hh, wrl_ref[...]) + br_ref[...]
    lane_i = lax.broadcasted_iota(I32, logits.shape, 1)
    lane = lane_i.astype(F32)
    big = float(ROUTER_LANES)
    is_grp = lane < N_GROUPS
    lg = jnp.where(is_grp, logits, NEG_BIG)
    mg = lg.max(axis=-1, keepdims=True)
    grp = _lane_min_where(is_grp & (lg == mg), lane, big)
    p_grp = 1.0 / jnp.sum(jnp.where(is_grp, jnp.exp(lg - mg), 0.0), axis=-1, keepdims=True)
    lo = N_GROUPS + grp * EXPERTS_PER_GROUP
    in_grp = (lane >= lo) & (lane < lo + EXPERTS_PER_GROUP)
    lf = jnp.where(in_grp, logits, NEG_BIG)
    v1 = lf.max(axis=-1, keepdims=True)
    i1 = _lane_min_where(in_grp & (lf == v1), lane, big)
    rest = in_grp & (lane != i1)
    lf2 = jnp.where(rest, logits, NEG_BIG)
    v2 = lf2.max(axis=-1, keepdims=True)
    i2 = _lane_min_where(rest & (lf2 == v2), lane, big)
    e2 = jnp.exp(v2 - v1)
    g1 = p_grp / (1.0 + e2)
    g2 = p_grp * e2 / (1.0 + e2)
    gate_ref[...] = jnp.where(lane_i == 0, g1, jnp.where(lane_i == 1, g2, 0.0))
    ex1 = i1 - N_GROUPS
    ex2 = i2 - N_GROUPS
    oh1 = lane == ex1
    oh2 = lane == ex2
    oh = jnp.where(oh1, 1.0, 0.0) + jnp.where(oh2, 1.0, 0.0)
    before = _dot(tri_ref[...], oh.astype(BF16)) + cnt_ref[...]
    r1 = jnp.sum(jnp.where(oh1, before, 0.0), axis=-1, keepdims=True)
    r2 = jnp.sum(jnp.where(oh2, before, 0.0), axis=-1, keepdims=True)
    cnt = cnt_ref[...] + jnp.sum(oh, axis=0, keepdims=True)
    cnt_ref[...] = cnt
    cnt_out_ref[...] = cnt
    route_ref[...] = jnp.where(lane_i == 0, ex1, jnp.where(lane_i == 1, ex2, jnp.where(
        lane_i == 2, r1, jnp.where(lane_i == 3, r2, 0.0)))).astype(I32)


def _router_weights(w_group, b_group, w_fine, b_fine):
    d = w_group.shape[0]
    pad = ROUTER_LANES - N_GROUPS - N_EXPERTS
    w = jnp.concatenate([w_group, w_fine, jnp.zeros((d, pad), F32)], axis=1)
    b = jnp.concatenate([b_group, b_fine, jnp.zeros((pad,), F32)]).reshape(1, ROUTER_LANES)
    w_hi = w.astype(BF16)
    w_lo = (w - w_hi.astype(F32)).astype(BF16)
    return w_hi, w_lo, b


def _oproj_route(x, o, w_o, g_ffn, router, base_counts, tm):
    t, d = x.shape
    kd = o.shape[1]
    w_hi, w_lo, b = router
    row = lambda i: (i, 0)
    r = jnp.arange(tm)
    tri = (r[None, :] < r[:, None]).astype(BF16)
    lanes = ROUTER_LANES
    return pl.pallas_call(
        _oproj_route_kernel,
        grid=(t // tm,),
        in_specs=[pl.BlockSpec((tm, d), row), pl.BlockSpec((tm, kd), row), _full_spec((kd, d)), _full_spec((1, d)),
                  _full_spec(w_hi.shape), _full_spec(w_lo.shape), _full_spec(b.shape), _full_spec((1, lanes)),
                  _full_spec((tm, tm))],
        out_specs=[pl.BlockSpec((tm, d), row), pl.BlockSpec((tm, d), row), pl.BlockSpec((tm, lanes), row),
                   pl.BlockSpec((tm, lanes), row), _full_spec((1, lanes))],
        out_shape=[jax.ShapeDtypeStruct((t, d), F32), jax.ShapeDtypeStruct((t, d), F32),
                   jax.ShapeDtypeStruct((t, lanes), I32), jax.ShapeDtypeStruct((t, lanes), F32),
                   jax.ShapeDtypeStruct((1, lanes), F32)],
        scratch_shapes=[pltpu.VMEM((1, lanes), F32)],
        compiler_params=_cparams(("arbitrary",)),
        name="oproj_route",
    )(x, o, w_o, g_ffn.reshape(1, d), w_hi, w_lo, b, base_counts, tri)


def _moe_kernel(be_ref, asg_ref, nact_ref, hn_hbm, wg_ref, wu_ref, wd_ref, y_hbm,
                xbuf, obuf, wgb, wub, wdb, gsem, ssem, *, t_all, t_pad):
    i = pl.program_id(0)
    nb = pl.num_programs(0)
    blk = MOE_SLOT_BLOCK
    unroll = 8
    n_act = nact_ref[0]
    slot = i % 2

    def gather(block, buf, start):
        if not start:
            pltpu.make_async_copy(hn_hbm.at[pl.ds(0, blk)], xbuf.at[buf], gsem.at[buf]).wait()
            return

        def group(r8, _):
            for u in range(unroll):
                r = r8 * unroll + u
                tok = jnp.maximum(asg_ref[block * blk + r], 0) >> 1
                pltpu.make_async_copy(hn_hbm.at[pl.ds(tok, 1)], xbuf.at[buf, pl.ds(r, 1)], gsem.at[buf]).start()
            return 0
        lax.fori_loop(0, blk // unroll, group, 0)

    def scatter(block, buf, start):
        if not start:
            pltpu.make_async_copy(obuf.at[buf], y_hbm.at[pl.ds(0, blk)], ssem.at[buf]).wait()
            return

        def group(r8, _):
            for u in range(unroll):
                r = r8 * unroll + u
                asg = asg_ref[block * blk + r]
                dst = jnp.where(asg >= 0, (asg & 1) * t_pad + (asg >> 1), 2 * t_pad + buf * blk + r)
                pltpu.make_async_copy(obuf.at[buf, pl.ds(r, 1)], y_hbm.at[pl.ds(dst, 1)], ssem.at[buf]).start()
            return 0
        lax.fori_loop(0, blk // unroll, group, 0)

    @pl.when(i == 0)
    def _():
        obuf[1] = jnp.zeros(obuf.shape[1:], F32)
        gap = t_pad - t_all
        fills = [(2 * t_pad, blk), (2 * t_pad + blk, blk)] + ([(t_all, gap), (t_pad + t_all, gap)] if gap else [])
        cps = [pltpu.make_async_copy(obuf.at[1, pl.ds(0, n)], y_hbm.at[pl.ds(r0, n)], ssem.at[1]) for r0, n in fills]
        for cp in cps:
            cp.start()
        for cp in cps:
            cp.wait()

    @pl.when((i == 0) & (n_act > 0))
    def _():
        gather(0, 0, True)

    @pl.when(i + 1 < n_act)
    def _():
        gather(i + 1, 1 - slot, True)

    @pl.when((i >= 2) & (i - 2 < n_act))
    def _():
        scatter(i - 2, slot, False)

    @pl.when(i < n_act)
    def _():
        @pl.when((i == 0) | (be_ref[i] != be_ref[jnp.maximum(i - 1, 0)]))
        def _():
            wgb[...] = wg_ref[0, 0].astype(BF16)
            wub[...] = wu_ref[0, 0].astype(BF16)
            wdb[...] = wd_ref[0, 0].astype(BF16)

        gather(i, slot, False)
        xb = xbuf[slot].astype(BF16)
        gt = _dot(xb, wgb[...])
        up = _dot(xb, wub[...])
        hmid = (gt * (1.0 / (1.0 + jnp.exp(-gt))) * up).astype(BF16)
        obuf[slot] = _dot(hmid, wdb[...])
        scatter(i, slot, True)

    @pl.when(i == nb - 1)
    def _():
        @pl.when((i - 1 >= 0) & (i - 1 < n_act))
        def _():
            scatter(i - 1, 1 - slot, False)

        @pl.when(i < n_act)
        def _():
            scatter(i, slot, False)


def _moe_experts(block_e, slot_asg, n_act, hn, w_gate, w_up, w_down, layer, t_pad):
    n_slots = slot_asg.shape[0]
    blk = MOE_SLOT_BLOCK
    n_blocks = n_slots // blk
    d = hn.shape[1]
    de = w_gate.shape[3]
    assert 0 <= t_pad - hn.shape[0] <= blk
    wmap = lambda i, be, asg, na: (layer, be[i], 0, 0)
    return pl.pallas_call(
        functools.partial(_moe_kernel, t_all=hn.shape[0], t_pad=t_pad),
        grid_spec=pltpu.PrefetchScalarGridSpec(
            num_scalar_prefetch=3, grid=(n_blocks,),
            in_specs=[pl.BlockSpec(memory_space=pl.ANY), pl.BlockSpec((1, 1, d, de), wmap),
                      pl.BlockSpec((1, 1, d, de), wmap), pl.BlockSpec((1, 1, de, d), wmap)],
            out_specs=pl.BlockSpec(memory_space=pl.ANY),
            scratch_shapes=[pltpu.VMEM((2, blk, d), F32), pltpu.VMEM((2, blk, d), F32), pltpu.VMEM((d, de), BF16),
                            pltpu.VMEM((d, de), BF16), pltpu.VMEM((de, d), BF16),
                            pltpu.SemaphoreType.DMA((2,)), pltpu.SemaphoreType.DMA((2,))]),
        out_shape=jax.ShapeDtypeStruct((2 * t_pad + 2 * blk, d), F32),
        compiler_params=_cparams(("arbitrary",)),
        name="moe_experts",
    )(block_e, slot_asg, n_act, hn, w_gate, w_up, w_down)


def _moe_slots(route, counts):
    n_tok = route.shape[0]
    blk = MOE_SLOT_BLOCK
    n_asg = 2 * n_tok
    experts = jnp.arange(N_EXPERTS, dtype=I32)
    padded = (counts + blk - 1) // blk * blk
    pad_end = jnp.cumsum(padded)
    pad_start = pad_end - padded
    e = route[:, :2]
    rank = route[:, 2:4]
    dest = jnp.sum(jnp.where(e[..., None] == experts, pad_start, 0), axis=-1) + rank
    n_blocks = -(-n_asg // blk) + N_EXPERTS
    slot_asg = jnp.full((n_blocks * blk,), -1, I32).at[dest.reshape(n_asg)].set(jnp.arange(n_asg, dtype=I32))
    n_act = (pad_end[-1] // blk).astype(I32)
    blk_start = jnp.arange(n_blocks, dtype=I32) * blk
    block_e = jnp.minimum(jnp.sum((pad_end[None, :] <= blk_start[:, None]).astype(I32), axis=1), N_EXPERTS - 1)
    last_e = jnp.sum(jnp.where(jnp.arange(n_blocks) == jnp.maximum(n_act - 1, 0), block_e, 0))
    block_e = jnp.where(jnp.arange(n_blocks) < n_act, block_e, last_e).astype(I32)
    return block_e, slot_asg, n_act.reshape(1)


def _combine_ple_kernel(x_ref, ya_ref, yb_ref, mg_ref, p_ref, g_ref, wg_ref, wp_ref, gfin_ref, o_ref, *, final):
    mg = mg_ref[...]
    x1 = x_ref[...] + (mg[:, 0:1] * ya_ref[...] + mg[:, 1:2] * yb_ref[...])
    gate = _dot(_rms(x1, g_ref[...]).astype(BF16), wg_ref[...])
    gate = 1.0 / (1.0 + jnp.exp(-gate))
    out = x1 + gate * _dot(p_ref[...].astype(BF16), wp_ref[...])
    if final:
        out = _rms(out, gfin_ref[...])
    o_ref[...] = out


def _combine_ple(x_mid, y2, row0, t_pad, moe_gate, p, g_ple, w_gate, w_proj, g_final, final, tm):
    t, d = x_mid.shape
    pd = p.shape[1]
    assert row0 % tm == 0 and t_pad % tm == 0
    row = lambda i: (i, 0)
    return pl.pallas_call(
        functools.partial(_combine_ple_kernel, final=final),
        grid=(t // tm,),
        in_specs=[pl.BlockSpec((tm, d), row), pl.BlockSpec((tm, d), lambda i: (row0 // tm + i, 0)),
                  pl.BlockSpec((tm, d), lambda i: ((t_pad + row0) // tm + i, 0)), pl.BlockSpec((tm, ROUTER_LANES), row),
                  pl.BlockSpec((tm, pd), row), _full_spec((1, d)), _full_spec((d, d)), _full_spec((pd, d)),
                  _full_spec((1, d))],
        out_specs=pl.BlockSpec((tm, d), row),
        out_shape=jax.ShapeDtypeStruct((t, d), F32),
        compiler_params=_cparams(("parallel",)),
        name="combine_ple",
    )(x_mid, y2, y2, moe_gate, p, g_ple.reshape(1, d), w_gate, w_proj, g_final.reshape(1, d))


def kernel(x_prompt, x_sample, cache_mla_ckv, cache_mla_kpe, cache_swa_k, cache_swa_v, cache_sb_k, cache_sb_v, page_table, p_prompt, p_sample, g_mix, g_ffn, g_ple, g_final, mla_w_in, mla_g_q, mla_g_kv, mla_w_uq, mla_w_uk, mla_w_uv, mla_w_o, swa_w_in, swa_sinks, swa_w_o, sb_w_in, sb_w_o, moe_w_group, moe_b_group, moe_w_fine, moe_b_fine, moe_w_gate, moe_w_up, moe_w_down, ple_w_gate, ple_w_proj):
    batch, seq, d = x_prompt.shape
    bd, dec_seq, _ = x_sample.shape
    assert dec_seq == 1
    depth = g_mix.shape[0]
    n_pages = page_table.shape[1]
    past_len = n_pages * PAGE_SIZE
    tp = batch * seq
    tm_p = _row_tile(tp, 256)
    tm_s = bd
    assert tp % bd == 0 and seq % tm_p == 0
    t_pad = -(-(tp + bd) // tm_p) * tm_p
    page_table = page_table.astype(I32)

    xp = x_prompt.reshape(tp, d)
    xs = x_sample.reshape(bd, d)
    pos_p = jnp.arange(seq)
    pos_s = jnp.full((bd,), past_len)
    rope_p = _rope_tables(pos_p, 32, 128)
    rope_s = _rope_tables(pos_s, 32, 128)

    outs = {k: [] for k in ("p_ckv", "p_kpe", "s_ckv", "s_kpe", "p_wk", "p_wv", "s_wk", "s_wv", "p_sk", "p_sv", "s_sk", "s_sv")}
    for i in range(depth):
        kind, j = i % 3, i // 3
        if kind == 0:
            weights = _mla_weights(mla_w_in[j], mla_w_uq[j], mla_w_uk[j], mla_w_uv[j])
            q_p, kk_p, v_p, ckv_p, kpe_p = _mla_proj(xp, g_mix[i], mla_g_q[j], mla_g_kv[j], weights, *rope_p, seq // tm_p, tm_p)
            q_s, _, _, ckv_s, kpe_s = _mla_proj(xs, g_mix[i], mla_g_q[j], mla_g_kv[j], weights, *rope_s, 1, tm_s)
            o_p = _mla_flash(q_p, kk_p, v_p, batch, seq, _row_tile(seq, 256), _row_tile(seq, 512))
            q3 = q_s.reshape(bd, MLA_HEADS, MLA_QK_PAD)
            w_uk_t = mla_w_uk[j].transpose(1, 2, 0).astype(BF16)
            qlat = _bmm(q3[:, :, 128:].transpose(1, 0, 2), w_uk_t, BF16).transpose(1, 0, 2)
            qx = jnp.concatenate([qlat, q3[:, :, :128]], axis=-1)
            o_lat = _mla_paged(page_table, qx, ckv_s.reshape(bd, 1, -1), kpe_s.reshape(bd, 1, -1),
                               cache_mla_ckv, cache_mla_kpe, j)
            w_uv_h = mla_w_uv[j].transpose(1, 0, 2).astype(BF16)
            o_s = _bmm(o_lat.transpose(1, 0, 2), w_uv_h, BF16).transpose(1, 0, 2).reshape(bd, MLA_HEADS * MLA_V)
            w_o = mla_w_o[j]
            outs["p_ckv"].append(ckv_p.reshape(batch, seq, -1)); outs["p_kpe"].append(kpe_p.reshape(batch, seq, -1))
            outs["s_ckv"].append(ckv_s.reshape(bd, 1, -1)); outs["s_kpe"].append(kpe_s.reshape(bd, 1, -1))
        elif kind == 1:
            sinks2 = swa_sinks[j] * LOG2E
            q_p, k_p, v_p = _swa_proj(xp, g_mix[i], swa_w_in[j], *rope_p, seq // tm_p, tm_p)
            q_s, k_s, v_s = _swa_proj(xs, g_mix[i], swa_w_in[j], *rope_s, 1, tm_s)
            o_p = _swa_prompt(sinks2, q_p, k_p, v_p, batch, seq)
            nkv = SWA_KV_HEADS * SWA_HEAD_DIM
            kwin = cache_swa_k[j].reshape(bd, WINDOW, nkv)
            vwin = cache_swa_v[j].reshape(bd, WINDOW, nkv)
            qbd = _block_diag_q(q_s, SWA_HEADS, SWA_KV_HEADS, SWA_HEAD_DIM)
            o_s = _swa_sample(sinks2, qbd, kwin, vwin, k_s.reshape(bd, 1, nkv), v_s.reshape(bd, 1, nkv))
            o_s = o_s.reshape(bd, SWA_HEADS * SWA_HEAD_DIM)
            w_o = swa_w_o[j]
            k4 = k_p.reshape(batch, seq, SWA_KV_HEADS, SWA_HEAD_DIM)
            v4 = v_p.reshape(batch, seq, SWA_KV_HEADS, SWA_HEAD_DIM)
            outs["p_wk"].append(k4[:, seq - WINDOW:]); outs["p_wv"].append(v4[:, seq - WINDOW:])
            outs["s_wk"].append(jnp.concatenate([cache_swa_k[j][:, 1:], k_s.reshape(bd, 1, SWA_KV_HEADS, SWA_HEAD_DIM)], axis=1))
            outs["s_wv"].append(jnp.concatenate([cache_swa_v[j][:, 1:], v_s.reshape(bd, 1, SWA_KV_HEADS, SWA_HEAD_DIM)], axis=1))
        else:
            q_p, k_p, v_p, kb_p, vb_p = _sb_proj(xp, g_mix[i], sb_w_in[j], tm_p)
            q_s, k_s, v_s, _, _ = _sb_proj(xs, g_mix[i], sb_w_in[j], tm_s)
            o_p = _sb_prompt(q_p, kb_p, vb_p, batch, seq)
            qbd = _block_diag_q(q_s, SB_HEADS, SB_KV_HEADS, SB_HEAD_DIM)
            o_s = _sb_paged(page_table, qbd, cache_sb_k, cache_sb_v, j).reshape(bd, SB_HEADS * SB_HEAD_DIM)
            w_o = sb_w_o[j]
            outs["p_sk"].append(k_p.reshape(batch, seq, SB_KV_HEADS, SB_HEAD_DIM))
            outs["p_sv"].append(v_p.reshape(batch, seq, SB_KV_HEADS, SB_HEAD_DIM))
            outs["s_sk"].append(k_s.reshape(bd, 1, SB_KV_HEADS, SB_HEAD_DIM))
            outs["s_sv"].append(v_s.reshape(bd, 1, SB_KV_HEADS, SB_HEAD_DIM))

        w_o = w_o.astype(BF16)
        router = _router_weights(moe_w_group[i], moe_b_group[i], moe_w_fine[i], moe_b_fine[i])
        zero_counts = jnp.zeros((1, ROUTER_LANES), F32)
        xmid_p, hn_p, route_p, gate_p, cnt_p = _oproj_route(xp, o_p, w_o, g_ffn[i], router, zero_counts, tm_p)
        xmid_s, hn_s, route_s, gate_s, cnt = _oproj_route(xs, o_s, w_o, g_ffn[i], router, cnt_p, tm_s)
        hn = jnp.concatenate([hn_p, hn_s], axis=0)
        route = jnp.concatenate([route_p[:, :4], route_s[:, :4]], axis=0)
        block_e, slot_asg, n_act = _moe_slots(route, cnt[0, :N_EXPERTS].astype(I32))
        y2 = _moe_experts(block_e, slot_asg, n_act, hn, moe_w_gate, moe_w_up, moe_w_down, i, t_pad)
        wg = ple_w_gate[i].astype(BF16)
        wp = ple_w_proj[i].astype(BF16)
        final = i == depth - 1
        xp = _combine_ple(xmid_p, y2, 0, t_pad, gate_p, p_prompt[i].reshape(tp, -1), g_ple[i], wg, wp, g_final, final, tm_p)
        xs = _combine_ple(xmid_s, y2, tp, t_pad, gate_s, p_sample[i].reshape(bd, -1), g_ple[i], wg, wp, g_final, final, tm_s)

    st = lambda k: jnp.stack(outs[k])
    return (xp.reshape(batch, seq, d), xs.reshape(bd, 1, d),
            st("p_ckv"), st("p_kpe"), st("p_wk"), st("p_wv"), st("p_sk"), st("p_sv"),
            st("s_ckv"), st("s_kpe"), st("s_wk"), st("s_wv"), st("s_sk"), st("s_sv"))
```

```python
import functools
import math

import jax
import jax.numpy as jnp
from jax import lax
from jax.experimental import pallas as pl
from jax.experimental.pallas import tpu as pltpu

F32 = jnp.float32
BF16 = jnp.bfloat16
I32 = jnp.int32

NORM_EPS = 1e-6
ROPE_THETA = 10000.0
PAGE_SIZE = 128

MLA_HEADS = 8
MLA_NOPE = 128
MLA_ROPE = 64
MLA_V = 128
MLA_Q_RANK = 512
MLA_KV_RANK = 256
MLA_QK_PAD = 256

SWA_HEADS = 16
SWA_KV_HEADS = 4
SWA_HEAD_DIM = 64
WINDOW = 128

SB_HEADS = 8
SB_KV_HEADS = 4
SB_HEAD_DIM = 128
SB_BLOCK = 128
SB_DEAD_LOG = -100.0

N_GROUPS = 4
EXPERTS_PER_GROUP = 8
N_EXPERTS = N_GROUPS * EXPERTS_PER_GROUP
D_EXPERT = 512
MOE_SLOT_BLOCK = 256
ROUTER_LANES = 128

LOG2E = 1.4426950408889634
NEG_BIG = -1e30
VMEM_LIMIT = 56 * 1024 * 1024
PAGES_PER_STEP = 16
FLASH_Q_TILE = 256
FLASH_K_TILE = 1024
FLASH_HEADS_PER_STEP = 2


def _cparams(sem, vmem=VMEM_LIMIT):
    return pltpu.CompilerParams(dimension_semantics=sem, vmem_limit_bytes=vmem)


def _rms(x, g):
    return x * lax.rsqrt(jnp.mean(x * x, axis=-1, keepdims=True) + NORM_EPS) * g


def _dot(a, b):
    return jnp.dot(a, b, preferred_element_type=F32)


def _dot_nt(a, b):
    return lax.dot_general(a, b, (((1,), (1,)), ((), ())), preferred_element_type=F32)


def _full_spec(shape):
    nd = len(shape)
    return pl.BlockSpec(shape, lambda *_: (0,) * nd)


def _row_tile(n, pref):
    return pref if n % pref == 0 else n


def _rope_tables(pos, half, lanes):
    inv = jnp.exp(-math.log(ROPE_THETA) * jnp.arange(half, dtype=F32) / half)
    ang = pos.astype(F32)[:, None] * inv[None, :]
    cos, sin = jnp.cos(ang), jnp.sin(ang)
    reps = lanes // (2 * half)
    if reps >= 1:
        return jnp.tile(jnp.concatenate([cos, cos], 1), (1, reps)), jnp.tile(jnp.concatenate([sin, sin], 1), (1, reps))
    pad = lanes - 2 * half
    n = pos.shape[0]
    return (jnp.concatenate([cos, cos, jnp.ones((n, pad), F32)], 1),
            jnp.concatenate([sin, sin, jnp.zeros((n, pad), F32)], 1))


def _rot_cols(w, head_dim):
    k, n = w.shape
    w3 = w.reshape(k, n // head_dim, head_dim)
    half = head_dim // 2
    return jnp.concatenate([-w3[..., half:], w3[..., :half]], axis=-1).reshape(k, n)


def _mla_proj_kernel(x_ref, gmix_ref, win_ref, gq_ref, gkv_ref, cos_ref, sin_ref, wq_ref, wqr_ref, wuk_ref, wuv_ref,
                     q_ref, kk_ref, v_ref, ckv_ref, kpe_ref, *, qscale):
    h = _rms(x_ref[...], gmix_ref[...]).astype(BF16)
    proj = _dot(h, win_ref[...])
    c_q = _rms(proj[:, :MLA_Q_RANK], gq_ref[...]).astype(BF16)
    c_kv = _rms(proj[:, MLA_Q_RANK:MLA_Q_RANK + MLA_KV_RANK], gkv_ref[...])
    ckv_ref[...] = c_kv
    cos = cos_ref[...]
    sin = sin_ref[...]
    o = MLA_Q_RANK + MLA_KV_RANK
    kpe = proj[:, o:o + 128] * cos + proj[:, o + 128:o + 256] * sin
    kpe_ref[...] = kpe[:, :MLA_ROPE]
    kpe_b = kpe.astype(BF16)
    ckv_b = c_kv.astype(BF16)
    kn = _dot(ckv_b, wuk_ref[...])
    vv = _dot(ckv_b, wuv_ref[...]).astype(BF16)
    ones = jnp.ones((vv.shape[0], MLA_V), BF16)
    for hd in range(MLA_HEADS):
        v_ref[:, 2 * hd * MLA_V:(2 * hd + 1) * MLA_V] = vv[:, hd * MLA_V:(hd + 1) * MLA_V]
        v_ref[:, (2 * hd + 1) * MLA_V:(2 * hd + 2) * MLA_V] = ones
    qe = _dot(c_q, wq_ref[...])
    qr = _dot(c_q, wqr_ref[...])
    for hd in range(MLA_HEADS):
        a = hd * MLA_QK_PAD
        pe = qe[:, a:a + 128] * cos + qr[:, hd * 128:(hd + 1) * 128] * sin
        q_ref[:, a:a + 128] = (pe * qscale).astype(BF16)
        q_ref[:, a + 128:a + 256] = (qe[:, a + 128:a + 256] * qscale).astype(BF16)
        kk_ref[:, a:a + 128] = kpe_b
        kk_ref[:, a + 128:a + 256] = kn[:, hd * 128:(hd + 1) * 128].astype(BF16)


def _mla_weights(w_in, w_uq, w_uk, w_uv):
    d = w_in.shape[0]
    o = MLA_Q_RANK + MLA_KV_RANK
    w_kpe = w_in[:, o:]
    z64 = jnp.zeros((d, 128 - MLA_ROPE), F32)
    win = jnp.concatenate([w_in[:, :o], w_kpe, z64, _rot_cols(w_kpe, MLA_ROPE), z64], axis=1).astype(BF16)
    r = w_uq.shape[0]
    w_nope = w_uq[:, :, :MLA_NOPE]
    w_pe = w_uq[:, :, MLA_NOPE:]
    zq = jnp.zeros((r, MLA_HEADS, 128 - MLA_ROPE), F32)
    wq = jnp.concatenate([w_pe, zq, w_nope], axis=-1).reshape(r, MLA_HEADS * MLA_QK_PAD).astype(BF16)
    w_pe_rot = _rot_cols(w_pe.reshape(r, MLA_HEADS * MLA_ROPE), MLA_ROPE).reshape(r, MLA_HEADS, MLA_ROPE)
    wqr = jnp.concatenate([w_pe_rot, zq], axis=-1).reshape(r, MLA_HEADS * 128).astype(BF16)
    wuk = w_uk.reshape(MLA_KV_RANK, MLA_HEADS * MLA_NOPE).astype(BF16)
    wuv = w_uv.reshape(MLA_KV_RANK, MLA_HEADS * MLA_V).astype(BF16)
    return win, wq, wqr, wuk, wuv


def _mla_proj(x, g_mix, g_q, g_kv, weights, cos, sin, n_pos_blocks, tm):
    win, wq, wqr, wuk, wuv = weights
    t, d = x.shape
    qscale = (MLA_NOPE + MLA_ROPE) ** -0.5 * LOG2E
    row = lambda i: (i, 0)
    pos = lambda i: (i % n_pos_blocks, 0)
    hq = MLA_HEADS * MLA_QK_PAD
    return pl.pallas_call(
        functools.partial(_mla_proj_kernel, qscale=qscale),
        grid=(t // tm,),
        in_specs=[pl.BlockSpec((tm, d), row), _full_spec((1, d)), _full_spec(win.shape), _full_spec((1, MLA_Q_RANK)),
                  _full_spec((1, MLA_KV_RANK)), pl.BlockSpec((tm, 128), pos), pl.BlockSpec((tm, 128), pos),
                  _full_spec(wq.shape), _full_spec(wqr.shape), _full_spec(wuk.shape), _full_spec(wuv.shape)],
        out_specs=[pl.BlockSpec((tm, hq), row), pl.BlockSpec((tm, hq), row), pl.BlockSpec((tm, 2 * MLA_HEADS * MLA_V), row),
                   pl.BlockSpec((tm, MLA_KV_RANK), row), pl.BlockSpec((tm, MLA_ROPE), row)],
        out_shape=[jax.ShapeDtypeStruct((t, hq), BF16), jax.ShapeDtypeStruct((t, hq), BF16),
                   jax.ShapeDtypeStruct((t, 2 * MLA_HEADS * MLA_V), BF16), jax.ShapeDtypeStruct((t, MLA_KV_RANK), F32),
                   jax.ShapeDtypeStruct((t, MLA_ROPE), F32)],
        compiler_params=_cparams(("parallel",)),
        name="mla_proj",
    )(x, g_mix.reshape(1, d), win, g_q.reshape(1, -1), g_kv.reshape(1, -1), cos, sin, wq, wqr, wuk, wuv)


def _mla_flash_kernel(q_ref, k_ref, v_ref, o_ref, *, tq, tk, nh):
    qi = pl.program_id(2)
    qk, dv = MLA_QK_PAD, MLA_V
    qs = [q_ref[:, h * qk:(h + 1) * qk] for h in range(nh)]

    def scores(h, j):
        return _dot_nt(qs[h], k_ref[pl.ds(pl.multiple_of(j * tk, tk), tk), h * qk:(h + 1) * qk])

    def update(h, j, s, m, acc):
        m_new = jnp.maximum(m, s.max(axis=-1, keepdims=True))
        p = jnp.exp2(s - m_new).astype(BF16)
        vj = v_ref[pl.ds(pl.multiple_of(j * tk, tk), tk), 2 * h * dv:2 * (h + 1) * dv]
        return m_new, acc * jnp.exp2(m - m_new) + _dot(p, vj)

    last = (qi * tq) // tk

    def body(j, carry):
        nxt = [scores(h, j + 1) for h in range(nh)]
        return tuple(update(h, j, carry[h][2], *carry[h][:2]) + (nxt[h],) for h in range(nh))

    init = tuple((jnp.full((tq, 1), NEG_BIG, F32), jnp.zeros((tq, 2 * dv), F32), scores(h, 0)) for h in range(nh))
    carry = lax.fori_loop(0, last, body, init)
    row = qi * tq + lax.broadcasted_iota(I32, (tq, tk), 0)
    col = last * tk + lax.broadcasted_iota(I32, (tq, tk), 1)
    for h in range(nh):
        m, acc, s = carry[h]
        _, acc = update(h, last, jnp.where(col <= row, s, NEG_BIG), m, acc)
        o_ref[:, h * dv:(h + 1) * dv] = (acc[:, :dv] / acc[:, dv:]).astype(BF16)


def _mla_flash(q, kk, v, batch, seq, tq, tk):
    nq = seq // tq
    nh = FLASH_HEADS_PER_STEP
    assert tk % tq == 0 and seq % tq == 0 and MLA_HEADS % nh == 0
    return pl.pallas_call(
        functools.partial(_mla_flash_kernel, tq=tq, tk=tk, nh=nh),
        grid=(batch, MLA_HEADS // nh, nq),
        in_specs=[pl.BlockSpec((tq, nh * MLA_QK_PAD), lambda b, h, i: (b * nq + i, h)),
                  pl.BlockSpec((seq, nh * MLA_QK_PAD), lambda b, h, i: (b, h)),
                  pl.BlockSpec((seq, 2 * nh * MLA_V), lambda b, h, i: (b, h))],
        out_specs=pl.BlockSpec((tq, nh * MLA_V), lambda b, h, i: (b * nq + i, h)),
        out_shape=jax.ShapeDtypeStruct((batch * seq, MLA_HEADS * MLA_V), BF16),
        compiler_params=_cparams(("parallel", "parallel", "arbitrary")),
        name="mla_flash",
    )(q, kk, v)


def _bmm_kernel(x_ref, w_ref, o_ref):
    o_ref[0] = _dot(x_ref[0], w_ref[0]).astype(o_ref.dtype)


def _bmm(x, w, out_dtype):
    h, m, k = x.shape
    n = w.shape[2]
    return pl.pallas_call(
        _bmm_kernel,
        grid=(h,),
        in_specs=[pl.BlockSpec((1, m, k), lambda i: (i, 0, 0)), pl.BlockSpec((1, k, n), lambda i: (i, 0, 0))],
        out_specs=pl.BlockSpec((1, m, n), lambda i: (i, 0, 0)),
        out_shape=jax.ShapeDtypeStruct((h, m, n), out_dtype),
        compiler_params=_cparams(("parallel",)),
        name="head_matmul",
    )(x, w)


def _mla_paged_kernel(pt_ref, qx_ref, cnew_ref, knew_ref, *rest, n_pp):
    ckv_refs = rest[:n_pp]
    kpe_refs = rest[n_pp:2 * n_pp]
    o_ref = rest[2 * n_pp]
    m_ref, l_ref, acc_ref, stage_ref, kstage_ref = rest[2 * n_pp + 1:]
    g = pl.program_id(1)
    r = MLA_KV_RANK
    qx = qx_ref[0]

    @pl.when(g == 0)
    def _():
        cn = cnew_ref[0]
        qf = qx.astype(F32)
        s0 = (jnp.sum(qf[:, :r] * cn.astype(BF16).astype(F32), axis=-1, keepdims=True)
              + jnp.sum(qf[:, r:r + MLA_ROPE] * knew_ref[0].astype(BF16).astype(F32), axis=-1, keepdims=True))
        m_ref[...] = jnp.broadcast_to(s0, m_ref.shape)
        l_ref[...] = jnp.ones(l_ref.shape, F32)
        acc_ref[...] = jnp.broadcast_to(cn, acc_ref.shape)

    for j in range(n_pp):
        rows = slice(j * PAGE_SIZE, (j + 1) * PAGE_SIZE)
        stage_ref[rows, :] = ckv_refs[j][0, 0].astype(BF16)
        kstage_ref[:, rows] = kpe_refs[j][0, 0].astype(BF16)
    s = _dot_nt(qx[:, :r], stage_ref[...]) + _dot(qx[:, r:r + MLA_ROPE], kstage_ref[...])
    m_old = m_ref[:, :1]
    m_new = jnp.maximum(m_old, s.max(axis=-1, keepdims=True))
    corr = jnp.exp2(m_old - m_new)
    p = jnp.exp2(s - m_new)
    l_new = l_ref[:, :1] * corr + p.sum(axis=-1, keepdims=True)
    acc = acc_ref[...] * corr + _dot(p.astype(BF16), stage_ref[...])
    acc_ref[...] = acc
    m_ref[...] = jnp.broadcast_to(m_new, m_ref.shape)
    l_ref[...] = jnp.broadcast_to(l_new, l_ref.shape)

    @pl.when(g == pl.num_programs(1) - 1)
    def _():
        o_ref[0] = (acc / l_new).astype(o_ref.dtype)


def _mla_paged(page_table, qx, c_new, k_new, ckv_pool, kpe_pool_t, layer):
    bd, n_pages = page_table.shape
    n_pp = math.gcd(PAGES_PER_STEP, n_pages)
    hds = qx.shape[1]
    kw = MLA_KV_RANK + 128

    def page_map(j):
        return lambda b, g, pt: (layer, pt[b, g * n_pp + j], 0, 0)

    per_b = lambda b, g, pt: (b, 0, 0)
    in_specs = [pl.BlockSpec((1, hds, kw), per_b), pl.BlockSpec((1, 1, MLA_KV_RANK), per_b),
                pl.BlockSpec((1, 1, MLA_ROPE), per_b)]
    in_specs += [pl.BlockSpec((1, 1, PAGE_SIZE, MLA_KV_RANK), page_map(j)) for j in range(n_pp)]
    in_specs += [pl.BlockSpec((1, 1, MLA_ROPE, PAGE_SIZE), page_map(j)) for j in range(n_pp)]
    return pl.pallas_call(
        functools.partial(_mla_paged_kernel, n_pp=n_pp),
        grid_spec=pltpu.PrefetchScalarGridSpec(
            num_scalar_prefetch=1, grid=(bd, n_pages // n_pp), in_specs=in_specs,
            out_specs=pl.BlockSpec((1, hds, MLA_KV_RANK), per_b),
            scratch_shapes=[pltpu.VMEM((hds, 128), F32), pltpu.VMEM((hds, 128), F32), pltpu.VMEM((hds, MLA_KV_RANK), F32),
                            pltpu.VMEM((n_pp * PAGE_SIZE, MLA_KV_RANK), BF16),
                            pltpu.VMEM((MLA_ROPE, n_pp * PAGE_SIZE), BF16)]),
        out_shape=jax.ShapeDtypeStruct((bd, hds, MLA_KV_RANK), BF16),
        compiler_params=_cparams(("parallel", "arbitrary")),
        name="mla_paged",
    )(page_table, qx, c_new, k_new, *([ckv_pool] * n_pp), *([kpe_pool_t] * n_pp))


def _swa_proj_kernel(x_ref, gmix_ref, win_ref, cos_ref, sin_ref, q_ref, k_ref, v_ref, *, qscale):
    nq, nkv = SWA_HEADS * SWA_HEAD_DIM, SWA_KV_HEADS * SWA_HEAD_DIM
    h = _rms(x_ref[...], gmix_ref[...]).astype(BF16)
    proj = _dot(h, win_ref[...])
    cos = cos_ref[...]
    sin = sin_ref[...]
    r0 = nq + 2 * nkv
    for c in range(nq // 128):
        a = c * 128
        q_ref[:, a:a + 128] = ((proj[:, a:a + 128] * cos + proj[:, r0 + a:r0 + a + 128] * sin) * qscale).astype(BF16)
    for c in range(nkv // 128):
        a = c * 128
        k_ref[:, a:a + 128] = proj[:, nq + a:nq + a + 128] * cos + proj[:, r0 + nq + a:r0 + nq + a + 128] * sin
    v_ref[...] = proj[:, nq + nkv:nq + 2 * nkv]


def _swa_proj(x, g_mix, w_in, cos, sin, n_pos_blocks, tm):
    t, d = x.shape
    nq, nkv = SWA_HEADS * SWA_HEAD_DIM, SWA_KV_HEADS * SWA_HEAD_DIM
    win = jnp.concatenate([w_in, _rot_cols(w_in[:, :nq + nkv], SWA_HEAD_DIM)], axis=1).astype(BF16)
    row = lambda i: (i, 0)
    pos = lambda i: (i % n_pos_blocks, 0)
    return pl.pallas_call(
        functools.partial(_swa_proj_kernel, qscale=SWA_HEAD_DIM ** -0.5 * LOG2E),
        grid=(t // tm,),
        in_specs=[pl.BlockSpec((tm, d), row), _full_spec((1, d)), _full_spec(win.shape),
                  pl.BlockSpec((tm, 128), pos), pl.BlockSpec((tm, 128), pos)],
        out_specs=[pl.BlockSpec((tm, nq), row), pl.BlockSpec((tm, nkv), row), pl.BlockSpec((tm, nkv), row)],
        out_shape=[jax.ShapeDtypeStruct((t, nq), BF16), jax.ShapeDtypeStruct((t, nkv), F32),
                   jax.ShapeDtypeStruct((t, nkv), F32)],
        compiler_params=_cparams(("parallel",)),
        name="swa_proj",
    )(x, g_mix.reshape(1, d), win, cos, sin)


def _sink_column(sink_ref, rows_per_head, kvh, n_rep):
    rid = lax.broadcasted_iota(I32, (n_rep * rows_per_head, 1), 0) // rows_per_head
    col = jnp.zeros((n_rep * rows_per_head, 1), F32)
    for g in range(n_rep):
        col = jnp.where(rid == g, sink_ref[kvh * n_rep + g], col)
    return col


def _swa_prompt_kernel(sink_ref, q_ref, kc_ref, kp_ref, vc_ref, vp_ref, o_ref):
    n = pl.program_id(1)
    w = WINDOW
    n_rep = SWA_HEADS // SWA_KV_HEADS
    hd = SWA_HEAD_DIM
    q = q_ref[...]
    kc = kc_ref[...].astype(BF16)
    kp = kp_ref[...].astype(BF16)
    vc = vc_ref[...].astype(BF16)
    vp = vp_ref[...].astype(BF16)
    t = lax.broadcasted_iota(I32, (n_rep * w, 2 * w), 0) % w
    c = lax.broadcasted_iota(I32, (n_rep * w, 2 * w), 1)
    first_key = jnp.where(n > 0, 0, w)
    valid = (c > t) & (c <= t + w) & (c >= first_key)
    for kvh in range(SWA_KV_HEADS):
        ks = slice(kvh * hd, (kvh + 1) * hd)
        kk = jnp.concatenate([kp[:, ks], kc[:, ks]], axis=0)
        vv = jnp.concatenate([vp[:, ks], vc[:, ks]], axis=0)
        qg = jnp.concatenate([q[:, (kvh * n_rep + g) * hd:(kvh * n_rep + g + 1) * hd] for g in range(n_rep)], axis=0)
        s = jnp.where(valid, _dot_nt(qg, kk), NEG_BIG)
        sink = _sink_column(sink_ref, w, kvh, n_rep)
        m = jnp.maximum(s.max(axis=-1, keepdims=True), sink)
        p = jnp.exp2(s - m)
        denom = p.sum(axis=-1, keepdims=True) + jnp.exp2(sink - m)
        o = _dot(p.astype(BF16), vv) / denom
        for g in range(n_rep):
            a = (kvh * n_rep + g) * hd
            o_ref[:, a:a + hd] = o[g * w:(g + 1) * w].astype(BF16)


def _swa_prompt(sinks2, q, k, v, batch, seq):
    nb = seq // WINDOW
    nq, nkv = SWA_HEADS * SWA_HEAD_DIM, SWA_KV_HEADS * SWA_HEAD_DIM
    cur = lambda b, n: (b * nb + n, 0)
    prev = lambda b, n: (b * nb + jnp.maximum(n - 1, 0), 0)
    return pl.pallas_call(
        _swa_prompt_kernel,
        grid=(batch, nb),
        in_specs=[pl.BlockSpec(memory_space=pltpu.SMEM), pl.BlockSpec((WINDOW, nq), cur),
                  pl.BlockSpec((WINDOW, nkv), cur), pl.BlockSpec((WINDOW, nkv), prev),
                  pl.BlockSpec((WINDOW, nkv), cur), pl.BlockSpec((WINDOW, nkv), prev)],
        out_specs=pl.BlockSpec((WINDOW, nq), cur),
        out_shape=jax.ShapeDtypeStruct((batch * seq, nq), BF16),
        compiler_params=_cparams(("parallel", "parallel")),
        name="swa_prompt",
    )(sinks2, q, k, k, v, v)


def _swa_sample_kernel(sink_ref, qbd_ref, kw_ref, vw_ref, kn_ref, vn_ref, o_ref):
    n_rep = SWA_HEADS // SWA_KV_HEADS
    hd = SWA_HEAD_DIM
    qbd = qbd_ref[0]
    kw = kw_ref[0].astype(BF16)
    vw = vw_ref[0].astype(BF16)
    kn = kn_ref[0].astype(BF16).astype(F32)
    vn = vn_ref[0].astype(BF16).astype(F32)
    s = _dot_nt(qbd, kw)
    col = lax.broadcasted_iota(I32, s.shape, 1)
    s = jnp.where(col >= 1, s, NEG_BIG)
    s_new = jnp.sum(qbd.astype(F32) * kn, axis=-1, keepdims=True)
    sink = _sink_column(sink_ref, 1, 0, SWA_HEADS)
    m = jnp.maximum(jnp.maximum(s.max(axis=-1, keepdims=True), s_new), sink)
    p = jnp.exp2(s - m)
    p_new = jnp.exp2(s_new - m)
    denom = p.sum(axis=-1, keepdims=True) + p_new + jnp.exp2(sink - m)
    full = (_dot(p.astype(BF16), vw) + p_new.astype(BF16).astype(F32) * vn) / denom
    rid = lax.broadcasted_iota(I32, (SWA_HEADS, hd), 0) // n_rep
    o = jnp.zeros((SWA_HEADS, hd), F32)
    for kvh in range(SWA_KV_HEADS):
        o = jnp.where(rid == kvh, full[:, kvh * hd:(kvh + 1) * hd], o)
    o_ref[0] = o.astype(BF16)


def _swa_sample(sinks2, qbd, kwin, vwin, k_new, v_new):
    bd = qbd.shape[0]
    nkv = SWA_KV_HEADS * SWA_HEAD_DIM
    per_b = lambda b: (b, 0, 0)
    return pl.pallas_call(
        _swa_sample_kernel,
        grid=(bd,),
        in_specs=[pl.BlockSpec(memory_space=pltpu.SMEM), pl.BlockSpec((1, SWA_HEADS, nkv), per_b),
                  pl.BlockSpec((1, WINDOW, nkv), per_b), pl.BlockSpec((1, WINDOW, nkv), per_b),
                  pl.BlockSpec((1, 1, nkv), per_b), pl.BlockSpec((1, 1, nkv), per_b)],
        out_specs=pl.BlockSpec((1, SWA_HEADS, SWA_HEAD_DIM), per_b),
        out_shape=jax.ShapeDtypeStruct((bd, SWA_HEADS, SWA_HEAD_DIM), BF16),
        compiler_params=_cparams(("parallel",)),
        name="swa_sample",
    )(sinks2, qbd, kwin, vwin, k_new, v_new)


def _block_diag_q(q, n_heads, n_kv, hd):
    b = q.shape[0]
    q3 = q.reshape(b, n_heads, hd)
    owner = (jnp.arange(n_heads) // (n_heads // n_kv))[:, None] == jnp.arange(n_kv)[None, :]
    return jnp.where(owner[None, :, :, None], q3[:, :, None, :], jnp.zeros((), q.dtype)).reshape(b, n_heads, n_kv * hd)


def _sb_proj_kernel(x_ref, gmix_ref, win_ref, q_ref, k_ref, v_ref, kb_ref, vb_ref, *, qscale):
    nq, nkv = SB_HEADS * SB_HEAD_DIM, SB_KV_HEADS * SB_HEAD_DIM
    h = _rms(x_ref[...], gmix_ref[...]).astype(BF16)
    proj = _dot(h, win_ref[...])
    q_ref[...] = (proj[:, :nq] * qscale).astype(BF16)
    k = proj[:, nq:nq + nkv]
    v = proj[:, nq + nkv:]
    k_ref[...] = k
    v_ref[...] = v
    kb_ref[...] = k.astype(BF16)
    vb_ref[...] = v.astype(BF16)


def _sb_proj(x, g_mix, w_in, tm):
    t, d = x.shape
    nq, nkv = SB_HEADS * SB_HEAD_DIM, SB_KV_HEADS * SB_HEAD_DIM
    row = lambda i: (i, 0)
    return pl.pallas_call(
        functools.partial(_sb_proj_kernel, qscale=SB_HEAD_DIM ** -0.5),
        grid=(t // tm,),
        in_specs=[pl.BlockSpec((tm, d), row), _full_spec((1, d)), _full_spec(w_in.shape)],
        out_specs=[pl.BlockSpec((tm, nq), row)] + [pl.BlockSpec((tm, nkv), row)] * 4,
        out_shape=[jax.ShapeDtypeStruct((t, nq), BF16), jax.ShapeDtypeStruct((t, nkv), F32),
                   jax.ShapeDtypeStruct((t, nkv), F32), jax.ShapeDtypeStruct((t, nkv), BF16),
                   jax.ShapeDtypeStruct((t, nkv), BF16)],
        compiler_params=_cparams(("parallel",)),
        name="sb_proj",
    )(x, g_mix.reshape(1, d), w_in.astype(BF16))


def _suffix_matrix():
    r = jnp.arange(SB_BLOCK)
    tri = (r[:, None] > r[None, :]).astype(BF16)
    return jnp.concatenate([tri, jnp.ones((SB_BLOCK, SB_BLOCK), BF16)], axis=1)


def _sb_weights(z, later, tri_ones, strict):
    m = z.shape[0]
    log_1mb = -(jnp.maximum(z, 0.0) + jnp.log(1.0 + jnp.exp(-jnp.abs(z))))
    if strict is not None:
        log_1mb = jnp.where(strict, log_1mb, 0.0)
    hi = log_1mb.astype(BF16)
    lo = (log_1mb - hi.astype(F32)).astype(BF16)
    sums = _dot(jnp.concatenate([hi, lo], axis=0), tri_ones)
    sums = sums[:m] + sums[m:]
    suffix = sums[:, :SB_BLOCK]
    total = sums[:, SB_BLOCK:]
    a = jnp.exp(z + log_1mb + suffix + later)
    if strict is not None:
        a = jnp.where(strict, a, 0.0)
    return a.astype(BF16), total


def _sb_prompt_kernel(q_ref, k_ref, v_ref, tri_ref, o_ref):
    qi = pl.program_id(1)
    blk = SB_BLOCK
    n_rep = SB_HEADS // SB_KV_HEADS
    hd = SB_HEAD_DIM
    q = q_ref[...]
    qs = jnp.concatenate([q[:, h * hd:(h + 1) * hd] for h in range(SB_HEADS)], axis=0)
    tri_ones = tri_ref[...]
    rows = SB_HEADS * blk
    grp = n_rep * blk

    def tile(j, later, strict):
        start = pl.multiple_of(j * blk, blk)
        kt = k_ref[pl.ds(start, blk), :]
        vt = v_ref[pl.ds(start, blk), :]
        z = jnp.concatenate([_dot_nt(qs[kvh * grp:(kvh + 1) * grp], kt[:, kvh * hd:(kvh + 1) * hd])
                             for kvh in range(SB_KV_HEADS)], axis=0)
        a, total = _sb_weights(z, later, tri_ones, strict)
        d_out = jnp.concatenate([_dot(a[kvh * grp:(kvh + 1) * grp], vt[:, kvh * hd:(kvh + 1) * hd])
                                 for kvh in range(SB_KV_HEADS)], axis=0)
        return d_out, total

    t = lax.broadcasted_iota(I32, (rows, blk), 0) % blk
    c = lax.broadcasted_iota(I32, (rows, blk), 1)
    out, later = tile(qi, jnp.zeros((rows, blk), F32), c < t)

    def cond(carry):
        j, alive, _, _ = carry
        return (j >= 0) & (alive > SB_DEAD_LOG)

    def body(carry):
        j, _, later, out = carry
        d_out, total = tile(j, later, None)
        later = later + total
        return j - 1, jnp.max(later), later, out + d_out

    _, _, _, out = lax.while_loop(cond, body, (qi - 1, jnp.max(later), later, out))
    o_ref[...] = jnp.concatenate([out[h * blk:(h + 1) * blk] for h in range(SB_HEADS)], axis=1).astype(BF16)


def _sb_prompt(q, kb, vb, batch, seq):
    nq = seq // SB_BLOCK
    nqd, nkv = SB_HEADS * SB_HEAD_DIM, SB_KV_HEADS * SB_HEAD_DIM
    qmap = lambda b, i: (b * nq + i, 0)
    return pl.pallas_call(
        _sb_prompt_kernel,
        grid=(batch, nq),
        in_specs=[pl.BlockSpec((SB_BLOCK, nqd), qmap), pl.BlockSpec((seq, nkv), lambda b, i: (b, 0)),
                  pl.BlockSpec((seq, nkv), lambda b, i: (b, 0)), _full_spec((SB_BLOCK, 2 * SB_BLOCK))],
        out_specs=pl.BlockSpec((SB_BLOCK, nqd), qmap),
        out_shape=jax.ShapeDtypeStruct((batch * seq, nqd), BF16),
        compiler_params=_cparams(("parallel", "arbitrary")),
        name="sb_prompt",
    )(q, kb, vb, _suffix_matrix())


def _sb_paged_kernel(pt_ref, qbd_ref, tri_ref, k_hbm, v_hbm, o_ref, kbuf, vbuf, sem, *, layer, n_pages):
    b = pl.program_id(0)
    n_rep = SB_HEADS // SB_KV_HEADS
    hd = SB_HEAD_DIM
    base = (b % 2) * 2

    def copies(row, step, slot):
        page = pt_ref[row, n_pages - 1 - step]
        cps = []
        for kvh in range(SB_KV_HEADS):
            cps.append(pltpu.make_async_copy(k_hbm.at[layer, page, :, kvh, :], kbuf.at[slot, kvh], sem.at[0, slot]))
            cps.append(pltpu.make_async_copy(v_hbm.at[layer, page, :, kvh, :], vbuf.at[slot, kvh], sem.at[1, slot]))
        return cps

    def start(row, step, slot):
        for cp in copies(row, step, slot):
            cp.start()

    def wait(row, step, slot):
        for cp in copies(row, step, slot):
            cp.wait()

    @pl.when(b == 0)
    def _():
        start(0, 0, 0)

    @pl.when(b + 1 < pl.num_programs(0))
    def _():
        start(b + 1, 0, 2 - base)

    qbd = qbd_ref[0]
    tri_ones = tri_ref[...]

    def cond(carry):
        i, alive, _, _ = carry
        return (i < n_pages) & (alive > SB_DEAD_LOG)

    def body(carry):
        i, _, later, acc = carry
        slot = base + i % 2

        @pl.when(i + 1 < n_pages)
        def _():
            start(b, i + 1, base + (i + 1) % 2)

        wait(b, i, slot)
        z = _dot_nt(qbd[:, :hd], kbuf[slot, 0].astype(BF16))
        for kvh in range(1, SB_KV_HEADS):
            z = z + _dot_nt(qbd[:, kvh * hd:(kvh + 1) * hd], kbuf[slot, kvh].astype(BF16))
        a, total = _sb_weights(z, later, tri_ones, None)
        d_acc = jnp.concatenate([_dot(a, vbuf[slot, kvh].astype(BF16)) for kvh in range(SB_KV_HEADS)], axis=1)
        later = later + total
        return i + 1, jnp.max(later), later, acc + d_acc

    init = (jnp.int32(0), jnp.float32(0.0), jnp.zeros((SB_HEADS, SB_BLOCK), F32),
            jnp.zeros((SB_HEADS, SB_KV_HEADS * hd), F32))
    n_done, _, _, acc = lax.while_loop(cond, body, init)

    @pl.when(n_done < n_pages)
    def _():
        wait(b, n_done, base + n_done % 2)

    rid = lax.broadcasted_iota(I32, (SB_HEADS, hd), 0) // n_rep
    o = jnp.zeros((SB_HEADS, hd), F32)
    for kvh in range(SB_KV_HEADS):
        o = jnp.where(rid == kvh, acc[:, kvh * hd:(kvh + 1) * hd], o)
    o_ref[0] = o.astype(o_ref.dtype)


def _sb_paged(page_table, qbd, k_pool, v_pool, layer):
    bd, n_pages = page_table.shape
    nkv = SB_KV_HEADS * SB_HEAD_DIM
    page_shape = (4, SB_KV_HEADS, PAGE_SIZE, SB_HEAD_DIM)
    per_b = lambda b, pt: (b, 0, 0)
    return pl.pallas_call(
        functools.partial(_sb_paged_kernel, layer=layer, n_pages=n_pages),
        grid_spec=pltpu.PrefetchScalarGridSpec(
            num_scalar_prefetch=1, grid=(bd,),
            in_specs=[pl.BlockSpec((1, SB_HEADS, nkv), per_b), pl.BlockSpec((SB_BLOCK, 2 * SB_BLOCK), lambda b, pt: (0, 0)),
                      pl.BlockSpec(memory_space=pl.ANY), pl.BlockSpec(memory_space=pl.ANY)],
            out_specs=pl.BlockSpec((1, SB_HEADS, SB_HEAD_DIM), per_b),
            scratch_shapes=[pltpu.VMEM(page_shape, F32), pltpu.VMEM(page_shape, F32), pltpu.SemaphoreType.DMA((2, 4))]),
        out_shape=jax.ShapeDtypeStruct((bd, SB_HEADS, SB_HEAD_DIM), BF16),
        compiler_params=_cparams(("arbitrary",)),
        name="sb_paged",
    )(page_table, qbd, _suffix_matrix(), k_pool, v_pool)


def _lane_min_where(mask, lane, big):
    return jnp.min(jnp.where(mask, lane, big), axis=-1, keepdims=True)


def _oproj_route_kernel(x_ref, o_ref, wo_ref, gffn_ref, wrh_ref, wrl_ref, br_ref, base_ref, tri_ref,
                        xmid_ref, hn_ref, route_ref, gate_ref, cnt_out_ref, cnt_ref):
    @pl.when(pl.program_id(0) == 0)
    def _():
        cnt_ref[...] = base_ref[...]

    x_mid = x_ref[...] + _dot(o_ref[...], wo_ref[...])
    xmid_ref[...] = x_mid
    hn = _rms(x_mid, gffn_ref[...])
    hn_ref[...] = hn
    hh = hn.astype(BF16)
    hl = (hn - hh.astype(F32)).astype(BF16)
    wrh = wrh_ref[...]
    logits = _dot(hh, wrh) + _dot(hl, wrh) + _dot(hh, wrl_ref[...]) + br_ref[...]
    lane_i = lax.broadcasted_iota(I32, logits.shape, 1)
    lane = lane_i.astype(F32)
    big = float(ROUTER_LANES)
    is_grp = lane < N_GROUPS
    lg = jnp.where(is_grp, logits, NEG_BIG)
    mg = lg.max(axis=-1, keepdims=True)
    grp = _lane_min_where(is_grp & (lg == mg), lane, big)
    p_grp = 1.0 / jnp.sum(jnp.where(is_grp, jnp.exp(lg - mg), 0.0), axis=-1, keepdims=True)
    lo = N_GROUPS + grp * EXPERTS_PER_GROUP
    in_grp = (lane >= lo) & (lane < lo + EXPERTS_PER_GROUP)
    lf = jnp.where(in_grp, logits, NEG_BIG)
    v1 = lf.max(axis=-1, keepdims=True)
    i1 = _lane_min_where(in_grp & (lf == v1), lane, big)
    rest = in_grp & (lane != i1)
    lf2 = jnp.where(rest, logits, NEG_BIG)
    v2 = lf2.max(axis=-1, keepdims=True)
    i2 = _lane_min_where(rest & (lf2 == v2), lane, big)
    e2 = jnp.exp(v2 - v1)
    g1 = p_grp / (1.0 + e2)
    g2 = p_grp * e2 / (1.0 + e2)
    gate_ref[...] = jnp.where(lane_i == 0, g1, jnp.where(lane_i == 1, g2, 0.0))
    ex1 = i1 - N_GROUPS
    ex2 = i2 - N_GROUPS
    oh1 = lane == ex1
    oh2 = lane == ex2
    oh = jnp.where(oh1, 1.0, 0.0) + jnp.where(oh2, 1.0, 0.0)
    before = _dot(tri_ref[...], oh.astype(BF16)) + cnt_ref[...]
    r1 = jnp.sum(jnp.where(oh1, before, 0.0), axis=-1, keepdims=True)
    r2 = jnp.sum(jnp.where(oh2, before, 0.0), axis=-1, keepdims=True)
    cnt = cnt_ref[...] + jnp.sum(oh, axis=0, keepdims=True)
    cnt_ref[...] = cnt
    cnt_out_ref[...] = cnt
    route_ref[...] = jnp.where(lane_i == 0, ex1, jnp.where(lane_i == 1, ex2, jnp.where(
        lane_i == 2, r1, jnp.where(lane_i == 3, r2, 0.0)))).astype(I32)


def _router_weights(w_group, b_group, w_fine, b_fine):
    d = w_group.shape[0]
    pad = ROUTER_LANES - N_GROUPS - N_EXPERTS
    w = jnp.concatenate([w_group, w_fine, jnp.zeros((d, pad), F32)], axis=1)
    b = jnp.concatenate([b_group, b_fine, jnp.zeros((pad,), F32)]).reshape(1, ROUTER_LANES)
    w_hi = w.astype(BF16)
    w_lo = (w - w_hi.astype(F32)).astype(BF16)
    return w_hi, w_lo, b


def _oproj_route(x, o, w_o, g_ffn, router, base_counts, tm):
    t, d = x.shape
    kd = o.shape[1]
    w_hi, w_lo, b = router
    row = lambda i: (i, 0)
    r = jnp.arange(tm)
    tri = (r[None, :] < r[:, None]).astype(BF16)
    lanes = ROUTER_LANES
    return pl.pallas_call(
        _oproj_route_kernel,
        grid=(t // tm,),
        in_specs=[pl.BlockSpec((tm, d), row), pl.BlockSpec((tm, kd), row), _full_spec((kd, d)), _full_spec((1, d)),
                  _full_spec(w_hi.shape), _full_spec(w_lo.shape), _full_spec(b.shape), _full_spec((1, lanes)),
                  _full_spec((tm, tm))],
        out_specs=[pl.BlockSpec((tm, d), row), pl.BlockSpec((tm, d), row), pl.BlockSpec((tm, lanes), row),
                   pl.BlockSpec((tm, lanes), row), _full_spec((1, lanes))],
        out_shape=[jax.ShapeDtypeStruct((t, d), F32), jax.ShapeDtypeStruct((t, d), F32),
                   jax.ShapeDtypeStruct((t, lanes), I32), jax.ShapeDtypeStruct((t, lanes), F32),
                   jax.ShapeDtypeStruct((1, lanes), F32)],
        scratch_shapes=[pltpu.VMEM((1, lanes), F32)],
        compiler_params=_cparams(("arbitrary",)),
        name="oproj_route",
    )(x, o, w_o, g_ffn.reshape(1, d), w_hi, w_lo, b, base_counts, tri)


def _moe_kernel(be_ref, tok_ref, dst_ref, nact_ref, hn_hbm, wg_ref, wu_ref, wd_ref, y_hbm,
                xbuf, obuf, wgb, wub, wdb, gsem, ssem, *, t_all, t_pad):
    i = pl.program_id(0)
    nb = pl.num_programs(0)
    blk = MOE_SLOT_BLOCK
    unroll = 8
    n_act = nact_ref[0]
    slot = i % 2

    def gather(block, buf, start):
        if not start:
            pltpu.make_async_copy(hn_hbm.at[pl.ds(0, blk)], xbuf.at[buf], gsem.at[buf]).wait()
            return

        def group(r8, _):
            for u in range(unroll):
                r = r8 * unroll + u
                tok = tok_ref[block * blk + r]
                pltpu.make_async_copy(hn_hbm.at[pl.ds(tok, 1)], xbuf.at[buf, pl.ds(r, 1)], gsem.at[buf]).start()
            return 0
        lax.fori_loop(0, blk // unroll, group, 0)

    def scatter(block, buf, start):
        if not start:
            pltpu.make_async_copy(obuf.at[buf], y_hbm.at[pl.ds(0, blk)], ssem.at[buf]).wait()
            return

        def group(r8, _):
            for u in range(unroll):
                r = r8 * unroll + u
                dst = dst_ref[block * blk + r]
                pltpu.make_async_copy(obuf.at[buf, pl.ds(r, 1)], y_hbm.at[pl.ds(dst, 1)], ssem.at[buf]).start()
            return 0
        lax.fori_loop(0, blk // unroll, group, 0)

    @pl.when(i == 0)
    def _():
        obuf[1] = jnp.zeros(obuf.shape[1:], F32)
        gap = t_pad - t_all
        fills = [(2 * t_pad, blk), (2 * t_pad + blk, blk)] + ([(t_all, gap), (t_pad + t_all, gap)] if gap else [])
        cps = [pltpu.make_async_copy(obuf.at[1, pl.ds(0, n)], y_hbm.at[pl.ds(r0, n)], ssem.at[1]) for r0, n in fills]
        for cp in cps:
            cp.start()
        for cp in cps:
            cp.wait()

    @pl.when((i == 0) & (n_act > 0))
    def _():
        gather(0, 0, True)

    @pl.when(i + 1 < n_act)
    def _():
        gather(i + 1, 1 - slot, True)

    @pl.when((i >= 2) & (i - 2 < n_act))
    def _():
        scatter(i - 2, slot, False)

    @pl.when(i < n_act)
    def _():
        @pl.when((i == 0) | (be_ref[i] != be_ref[jnp.maximum(i - 1, 0)]))
        def _():
            wgb[...] = wg_ref[0, 0].astype(BF16)
            wub[...] = wu_ref[0, 0].astype(BF16)
            wdb[...] = wd_ref[0, 0].astype(BF16)

        gather(i, slot, False)
        xb = xbuf[slot].astype(BF16)
        gt = _dot(xb, wgb[...])
        up = _dot(xb, wub[...])
        hmid = (gt * (1.0 / (1.0 + jnp.exp(-gt))) * up).astype(BF16)
        obuf[slot] = _dot(hmid, wdb[...])
        scatter(i, slot, True)

    @pl.when(i == nb - 1)
    def _():
        @pl.when((i - 1 >= 0) & (i - 1 < n_act))
        def _():
            scatter(i - 1, 1 - slot, False)

        @pl.when(i < n_act)
        def _():
            scatter(i, slot, False)


def _moe_experts(block_e, slot_tok, slot_dst, n_act, hn, w_gate, w_up, w_down, layer, t_pad):
    n_slots = slot_tok.shape[0]
    blk = MOE_SLOT_BLOCK
    n_blocks = n_slots // blk
    d = hn.shape[1]
    de = w_gate.shape[3]
    assert 0 <= t_pad - hn.shape[0] <= blk
    wmap = lambda i, be, tok, dst, na: (layer, be[i], 0, 0)
    return pl.pallas_call(
        functools.partial(_moe_kernel, t_all=hn.shape[0], t_pad=t_pad),
        grid_spec=pltpu.PrefetchScalarGridSpec(
            num_scalar_prefetch=4, grid=(n_blocks,),
            in_specs=[pl.BlockSpec(memory_space=pl.ANY), pl.BlockSpec((1, 1, d, de), wmap),
                      pl.BlockSpec((1, 1, d, de), wmap), pl.BlockSpec((1, 1, de, d), wmap)],
            out_specs=pl.BlockSpec(memory_space=pl.ANY),
            scratch_shapes=[pltpu.VMEM((2, blk, d), F32), pltpu.VMEM((2, blk, d), F32), pltpu.VMEM((d, de), BF16),
                            pltpu.VMEM((d, de), BF16), pltpu.VMEM((de, d), BF16),
                            pltpu.SemaphoreType.DMA((2,)), pltpu.SemaphoreType.DMA((2,))]),
        out_shape=jax.ShapeDtypeStruct((2 * t_pad + 2 * blk, d), F32),
        compiler_params=_cparams(("arbitrary",)),
        name="moe_experts",
    )(block_e, slot_tok, slot_dst, n_act, hn, w_gate, w_up, w_down)


def _moe_slots(route, counts, t_pad):
    n_tok = route.shape[0]
    blk = MOE_SLOT_BLOCK
    n_asg = 2 * n_tok
    experts = jnp.arange(N_EXPERTS, dtype=I32)
    padded = (counts + blk - 1) // blk * blk
    pad_end = jnp.cumsum(padded)
    pad_start = pad_end - padded
    e = route[:, :2]
    rank = route[:, 2:4]
    dest = jnp.sum(jnp.where(e[..., None] == experts, pad_start, 0), axis=-1) + rank
    n_blocks = -(-n_asg // blk) + N_EXPERTS
    slot_asg = jnp.full((n_blocks * blk,), -1, I32).at[dest.reshape(n_asg)].set(jnp.arange(n_asg, dtype=I32))
    slot = jnp.arange(n_blocks * blk, dtype=I32)
    slot_tok = jnp.maximum(slot_asg, 0) >> 1
    spare = 2 * t_pad + ((slot // blk) % 2) * blk + slot % blk
    slot_dst = jnp.where(slot_asg >= 0, (slot_asg & 1) * t_pad + (slot_asg >> 1), spare)
    n_act = (pad_end[-1] // blk).astype(I32)
    blk_start = jnp.arange(n_blocks, dtype=I32) * blk
    block_e = jnp.minimum(jnp.sum((pad_end[None, :] <= blk_start[:, None]).astype(I32), axis=1), N_EXPERTS - 1)
    last_e = jnp.sum(jnp.where(jnp.arange(n_blocks) == jnp.maximum(n_act - 1, 0), block_e, 0))
    block_e = jnp.where(jnp.arange(n_blocks) < n_act, block_e, last_e).astype(I32)
    return block_e, slot_tok, slot_dst, n_act.reshape(1)


def _combine_ple_kernel(x_ref, ya_ref, yb_ref, mg_ref, p_ref, g_ref, wg_ref, wp_ref, gfin_ref, o_ref, *, final):
    mg = mg_ref[...]
    x1 = x_ref[...] + (mg[:, 0:1] * ya_ref[...] + mg[:, 1:2] * yb_ref[...])
    gate = _dot(_rms(x1, g_ref[...]).astype(BF16), wg_ref[...])
    gate = 1.0 / (1.0 + jnp.exp(-gate))
    out = x1 + gate * _dot(p_ref[...].astype(BF16), wp_ref[...])
    if final:
        out = _rms(out, gfin_ref[...])
    o_ref[...] = out


def _combine_ple(x_mid, y2, row0, t_pad, moe_gate, p, g_ple, w_gate, w_proj, g_final, final, tm):
    t, d = x_mid.shape
    pd = p.shape[1]
    assert row0 % tm == 0 and t_pad % tm == 0
    row = lambda i: (i, 0)
    return pl.pallas_call(
        functools.partial(_combine_ple_kernel, final=final),
        grid=(t // tm,),
        in_specs=[pl.BlockSpec((tm, d), row), pl.BlockSpec((tm, d), lambda i: (row0 // tm + i, 0)),
                  pl.BlockSpec((tm, d), lambda i: ((t_pad + row0) // tm + i, 0)), pl.BlockSpec((tm, ROUTER_LANES), row),
                  pl.BlockSpec((tm, pd), row), _full_spec((1, d)), _full_spec((d, d)), _full_spec((pd, d)),
                  _full_spec((1, d))],
        out_specs=pl.BlockSpec((tm, d), row),
        out_shape=jax.ShapeDtypeStruct((t, d), F32),
        compiler_params=_cparams(("parallel",)),
        name="combine_ple",
    )(x_mid, y2, y2, moe_gate, p, g_ple.reshape(1, d), w_gate, w_proj, g_final.reshape(1, d))


def kernel(x_prompt, x_sample, cache_mla_ckv, cache_mla_kpe, cache_swa_k, cache_swa_v, cache_sb_k, cache_sb_v, page_table, p_prompt, p_sample, g_mix, g_ffn, g_ple, g_final, mla_w_in, mla_g_q, mla_g_kv, mla_w_uq, mla_w_uk, mla_w_uv, mla_w_o, swa_w_in, swa_sinks, swa_w_o, sb_w_in, sb_w_o, moe_w_group, moe_b_group, moe_w_fine, moe_b_fine, moe_w_gate, moe_w_up, moe_w_down, ple_w_gate, ple_w_proj):
    batch, seq, d = x_prompt.shape
    bd, dec_seq, _ = x_sample.shape
    assert dec_seq == 1
    depth = g_mix.shape[0]
    n_pages = page_table.shape[1]
    past_len = n_pages * PAGE_SIZE
    tp = batch * seq
    tm_p = _row_tile(tp, 256)
    tm_s = bd
    assert tp % bd == 0 and seq % tm_p == 0
    t_pad = -(-(tp + bd) // tm_p) * tm_p
    page_table = page_table.astype(I32)
    kpe_pool_t = cache_mla_kpe.transpose(0, 1, 3, 2)

    xp = x_prompt.reshape(tp, d)
    xs = x_sample.reshape(bd, d)
    pos_p = jnp.arange(seq)
    pos_s = jnp.full((bd,), past_len)
    rope_p = _rope_tables(pos_p, 32, 128)
    rope_s = _rope_tables(pos_s, 32, 128)

    outs = {k: [] for k in ("p_ckv", "p_kpe", "s_ckv", "s_kpe", "p_wk", "p_wv", "s_wk", "s_wv", "p_sk", "p_sv", "s_sk", "s_sv")}
    for i in range(depth):
        kind, j = i % 3, i // 3
        if kind == 0:
            weights = _mla_weights(mla_w_in[j], mla_w_uq[j], mla_w_uk[j], mla_w_uv[j])
            q_p, kk_p, v_p, ckv_p, kpe_p = _mla_proj(xp, g_mix[i], mla_g_q[j], mla_g_kv[j], weights, *rope_p, seq // tm_p, tm_p)
            q_s, _, _, ckv_s, kpe_s = _mla_proj(xs, g_mix[i], mla_g_q[j], mla_g_kv[j], weights, *rope_s, 1, tm_s)
            o_p = _mla_flash(q_p, kk_p, v_p, batch, seq, _row_tile(seq, FLASH_Q_TILE), _row_tile(seq, FLASH_K_TILE))
            q3 = q_s.reshape(bd, MLA_HEADS, MLA_QK_PAD)
            w_uk_t = mla_w_uk[j].transpose(1, 2, 0).astype(BF16)
            qlat = _bmm(q3[:, :, 128:].transpose(1, 0, 2), w_uk_t, BF16).transpose(1, 0, 2)
            qx = jnp.concatenate([qlat, q3[:, :, :128]], axis=-1)
            o_lat = _mla_paged(page_table, qx, ckv_s.reshape(bd, 1, -1), kpe_s.reshape(bd, 1, -1),
                               cache_mla_ckv, kpe_pool_t, j)
            w_uv_h = mla_w_uv[j].transpose(1, 0, 2).astype(BF16)
            o_s = _bmm(o_lat.transpose(1, 0, 2), w_uv_h, BF16).transpose(1, 0, 2).reshape(bd, MLA_HEADS * MLA_V)
            w_o = mla_w_o[j]
            outs["p_ckv"].append(ckv_p.reshape(batch, seq, -1)); outs["p_kpe"].append(kpe_p.reshape(batch, seq, -1))
            outs["s_ckv"].append(ckv_s.reshape(bd, 1, -1)); outs["s_kpe"].append(kpe_s.reshape(bd, 1, -1))
        elif kind == 1:
            sinks2 = swa_sinks[j] * LOG2E
            q_p, k_p, v_p = _swa_proj(xp, g_mix[i], swa_w_in[j], *rope_p, seq // tm_p, tm_p)
            q_s, k_s, v_s = _swa_proj(xs, g_mix[i], swa_w_in[j], *rope_s, 1, tm_s)
            o_p = _swa_prompt(sinks2, q_p, k_p, v_p, batch, seq)
            nkv = SWA_KV_HEADS * SWA_HEAD_DIM
            kwin = cache_swa_k[j].reshape(bd, WINDOW, nkv)
            vwin = cache_swa_v[j].reshape(bd, WINDOW, nkv)
            qbd = _block_diag_q(q_s, SWA_HEADS, SWA_KV_HEADS, SWA_HEAD_DIM)
            o_s = _swa_sample(sinks2, qbd, kwin, vwin, k_s.reshape(bd, 1, nkv), v_s.reshape(bd, 1, nkv))
            o_s = o_s.reshape(bd, SWA_HEADS * SWA_HEAD_DIM)
            w_o = swa_w_o[j]
            k4 = k_p.reshape(batch, seq, SWA_KV_HEADS, SWA_HEAD_DIM)
            v4 = v_p.reshape(batch, seq, SWA_KV_HEADS, SWA_HEAD_DIM)
            outs["p_wk"].append(k4[:, seq - WINDOW:]); outs["p_wv"].append(v4[:, seq - WINDOW:])
            outs["s_wk"].append(jnp.concatenate([cache_swa_k[j][:, 1:], k_s.reshape(bd, 1, SWA_KV_HEADS, SWA_HEAD_DIM)], axis=1))
            outs["s_wv"].append(jnp.concatenate([cache_swa_v[j][:, 1:], v_s.reshape(bd, 1, SWA_KV_HEADS, SWA_HEAD_DIM)], axis=1))
        else:
            q_p, k_p, v_p, kb_p, vb_p = _sb_proj(xp, g_mix[i], sb_w_in[j], tm_p)
            q_s, k_s, v_s, _, _ = _sb_proj(xs, g_mix[i], sb_w_in[j], tm_s)
            o_p = _sb_prompt(q_p, kb_p, vb_p, batch, seq)
            qbd = _block_diag_q(q_s, SB_HEADS, SB_KV_HEADS, SB_HEAD_DIM)
            o_s = _sb_paged(page_table, qbd, cache_sb_k, cache_sb_v, j).reshape(bd, SB_HEADS * SB_HEAD_DIM)
            w_o = sb_w_o[j]
            outs["p_sk"].append(k_p.reshape(batch, seq, SB_KV_HEADS, SB_HEAD_DIM))
            outs["p_sv"].append(v_p.reshape(batch, seq, SB_KV_HEADS, SB_HEAD_DIM))
            outs["s_sk"].append(k_s.reshape(bd, 1, SB_KV_HEADS, SB_HEAD_DIM))
            outs["s_sv"].append(v_s.reshape(bd, 1, SB_KV_HEADS, SB_HEAD_DIM))

        w_o = w_o.astype(BF16)
        router = _router_weights(moe_w_group[i], moe_b_group[i], moe_w_fine[i], moe_b_fine[i])
        zero_counts = jnp.zeros((1, ROUTER_LANES), F32)
        xmid_p, hn_p, route_p, gate_p, cnt_p = _oproj_route(xp, o_p, w_o, g_ffn[i], router, zero_counts, tm_p)
        xmid_s, hn_s, route_s, gate_s, cnt = _oproj_route(xs, o_s, w_o, g_ffn[i], router, cnt_p, tm_s)
        hn = jnp.concatenate([hn_p, hn_s], axis=0)
        route = jnp.concatenate([route_p[:, :4], route_s[:, :4]], axis=0)
        block_e, slot_tok, slot_dst, n_act = _moe_slots(route, cnt[0, :N_EXPERTS].astype(I32), t_pad)
        y2 = _moe_experts(block_e, slot_tok, slot_dst, n_act, hn, moe_w_gate, moe_w_up, moe_w_down, i, t_pad)
        wg = ple_w_gate[i].astype(BF16)
        wp = ple_w_proj[i].astype(BF16)
        final = i == depth - 1
        xp = _combine_ple(xmid_p, y2, 0, t_pad, gate_p, p_prompt[i].reshape(tp, -1), g_ple[i], wg, wp, g_final, final, tm_p)
        xs = _combine_ple(xmid_s, y2, tp, t_pad, gate_s, p_sample[i].reshape(bd, -1), g_ple[i], wg, wp, g_final, final, tm_s)

    st = lambda k: jnp.stack(outs[k])
    return (xp.reshape(batch, seq, d), xs.reshape(bd, 1, d),
            st("p_ckv"), st("p_kpe"), st("p_wk"), st("p_wv"), st("p_sk"), st("p_sv"),
            st("s_ckv"), st("s_kpe"), st("s_wk"), st("s_wv"), st("s_sk"), st("s_sv"))
```

```python
import functools
import math

import jax
import jax.numpy as jnp
from jax import lax
from jax.experimental import pallas as pl
from jax.experimental.pallas import tpu as pltpu

F32 = jnp.float32
BF16 = jnp.bfloat16
I32 = jnp.int32

NORM_EPS = 1e-6
ROPE_THETA = 10000.0
PAGE_SIZE = 128

MLA_HEADS = 8
MLA_NOPE = 128
MLA_ROPE = 64
MLA_V = 128
MLA_Q_RANK = 512
MLA_KV_RANK = 256
MLA_QK_PAD = 256

SWA_HEADS = 16
SWA_KV_HEADS = 4
SWA_HEAD_DIM = 64
WINDOW = 128

SB_HEADS = 8
SB_KV_HEADS = 4
SB_HEAD_DIM = 128
SB_BLOCK = 128
SB_DEAD_LOG = -100.0

N_GROUPS = 4
EXPERTS_PER_GROUP = 8
N_EXPERTS = N_GROUPS * EXPERTS_PER_GROUP
D_EXPERT = 512
MOE_SLOT_BLOCK = 256
ROUTER_LANES = 128

LOG2E = 1.4426950408889634
NEG_BIG = -1e30
VMEM_LIMIT = 56 * 1024 * 1024
PAGES_PER_STEP = 16
FLASH_Q_TILE = 256
FLASH_K_TILE = 1024
FLASH_HEADS_PER_STEP = 2


def _cparams(sem, vmem=VMEM_LIMIT):
    return pltpu.CompilerParams(dimension_semantics=sem, vmem_limit_bytes=vmem)


def _rms(x, g):
    return x * lax.rsqrt(jnp.mean(x * x, axis=-1, keepdims=True) + NORM_EPS) * g


def _dot(a, b):
    return jnp.dot(a, b, preferred_element_type=F32)


def _dot_nt(a, b):
    return lax.dot_general(a, b, (((1,), (1,)), ((), ())), preferred_element_type=F32)


def _full_spec(shape):
    nd = len(shape)
    return pl.BlockSpec(shape, lambda *_: (0,) * nd)


def _row_tile(n, pref):
    return pref if n % pref == 0 else n


def _rope_tables(pos, half, lanes):
    inv = jnp.exp(-math.log(ROPE_THETA) * jnp.arange(half, dtype=F32) / half)
    ang = pos.astype(F32)[:, None] * inv[None, :]
    cos, sin = jnp.cos(ang), jnp.sin(ang)
    reps = lanes // (2 * half)
    if reps >= 1:
        return jnp.tile(jnp.concatenate([cos, cos], 1), (1, reps)), jnp.tile(jnp.concatenate([sin, sin], 1), (1, reps))
    pad = lanes - 2 * half
    n = pos.shape[0]
    return (jnp.concatenate([cos, cos, jnp.ones((n, pad), F32)], 1),
            jnp.concatenate([sin, sin, jnp.zeros((n, pad), F32)], 1))


def _rot_cols(w, head_dim):
    k, n = w.shape
    w3 = w.reshape(k, n // head_dim, head_dim)
    half = head_dim // 2
    return jnp.concatenate([-w3[..., half:], w3[..., :half]], axis=-1).reshape(k, n)


def _mla_proj_kernel(x_ref, gmix_ref, win_ref, gq_ref, gkv_ref, cos_ref, sin_ref, wq_ref, wqr_ref, wuk_ref, wuv_ref,
                     q_ref, kk_ref, v_ref, ckv_ref, kpe_ref, *, qscale):
    h = _rms(x_ref[...], gmix_ref[...]).astype(BF16)
    proj = _dot(h, win_ref[...])
    c_q = _rms(proj[:, :MLA_Q_RANK], gq_ref[...]).astype(BF16)
    c_kv = _rms(proj[:, MLA_Q_RANK:MLA_Q_RANK + MLA_KV_RANK], gkv_ref[...])
    ckv_ref[...] = c_kv
    cos = cos_ref[...]
    sin = sin_ref[...]
    o = MLA_Q_RANK + MLA_KV_RANK
    kpe = proj[:, o:o + 128] * cos + proj[:, o + 128:o + 256] * sin
    kpe_ref[...] = kpe[:, :MLA_ROPE]
    kpe_b = kpe.astype(BF16)
    ckv_b = c_kv.astype(BF16)
    kn = _dot(ckv_b, wuk_ref[...])
    vv = _dot(ckv_b, wuv_ref[...]).astype(BF16)
    ones = jnp.ones((vv.shape[0], MLA_V), BF16)
    for hd in range(MLA_HEADS):
        v_ref[:, 2 * hd * MLA_V:(2 * hd + 1) * MLA_V] = vv[:, hd * MLA_V:(hd + 1) * MLA_V]
        v_ref[:, (2 * hd + 1) * MLA_V:(2 * hd + 2) * MLA_V] = ones
    qe = _dot(c_q, wq_ref[...])
    qr = _dot(c_q, wqr_ref[...])
    for hd in range(MLA_HEADS):
        a = hd * MLA_QK_PAD
        pe = qe[:, a:a + 128] * cos + qr[:, hd * 128:(hd + 1) * 128] * sin
        q_ref[:, a:a + 128] = (pe * qscale).astype(BF16)
        q_ref[:, a + 128:a + 256] = (qe[:, a + 128:a + 256] * qscale).astype(BF16)
        kk_ref[:, a:a + 128] = kpe_b
        kk_ref[:, a + 128:a + 256] = kn[:, hd * 128:(hd + 1) * 128].astype(BF16)


def _mla_weights(w_in, w_uq, w_uk, w_uv):
    d = w_in.shape[0]
    o = MLA_Q_RANK + MLA_KV_RANK
    w_kpe = w_in[:, o:]
    z64 = jnp.zeros((d, 128 - MLA_ROPE), F32)
    win = jnp.concatenate([w_in[:, :o], w_kpe, z64, _rot_cols(w_kpe, MLA_ROPE), z64], axis=1).astype(BF16)
    r = w_uq.shape[0]
    w_nope = w_uq[:, :, :MLA_NOPE]
    w_pe = w_uq[:, :, MLA_NOPE:]
    zq = jnp.zeros((r, MLA_HEADS, 128 - MLA_ROPE), F32)
    wq = jnp.concatenate([w_pe, zq, w_nope], axis=-1).reshape(r, MLA_HEADS * MLA_QK_PAD).astype(BF16)
    w_pe_rot = _rot_cols(w_pe.reshape(r, MLA_HEADS * MLA_ROPE), MLA_ROPE).reshape(r, MLA_HEADS, MLA_ROPE)
    wqr = jnp.concatenate([w_pe_rot, zq], axis=-1).reshape(r, MLA_HEADS * 128).astype(BF16)
    wuk = w_uk.reshape(MLA_KV_RANK, MLA_HEADS * MLA_NOPE).astype(BF16)
    wuv = w_uv.reshape(MLA_KV_RANK, MLA_HEADS * MLA_V).astype(BF16)
    return win, wq, wqr, wuk, wuv


def _mla_proj(x, g_mix, g_q, g_kv, weights, cos, sin, n_pos_blocks, tm):
    win, wq, wqr, wuk, wuv = weights
    t, d = x.shape
    qscale = (MLA_NOPE + MLA_ROPE) ** -0.5 * LOG2E
    row = lambda i: (i, 0)
    pos = lambda i: (i % n_pos_blocks, 0)
    hq = MLA_HEADS * MLA_QK_PAD
    return pl.pallas_call(
        functools.partial(_mla_proj_kernel, qscale=qscale),
        grid=(t // tm,),
        in_specs=[pl.BlockSpec((tm, d), row), _full_spec((1, d)), _full_spec(win.shape), _full_spec((1, MLA_Q_RANK)),
                  _full_spec((1, MLA_KV_RANK)), pl.BlockSpec((tm, 128), pos), pl.BlockSpec((tm, 128), pos),
                  _full_spec(wq.shape), _full_spec(wqr.shape), _full_spec(wuk.shape), _full_spec(wuv.shape)],
        out_specs=[pl.BlockSpec((tm, hq), row), pl.BlockSpec((tm, hq), row), pl.BlockSpec((tm, 2 * MLA_HEADS * MLA_V), row),
                   pl.BlockSpec((tm, MLA_KV_RANK), row), pl.BlockSpec((tm, MLA_ROPE), row)],
        out_shape=[jax.ShapeDtypeStruct((t, hq), BF16), jax.ShapeDtypeStruct((t, hq), BF16),
                   jax.ShapeDtypeStruct((t, 2 * MLA_HEADS * MLA_V), BF16), jax.ShapeDtypeStruct((t, MLA_KV_RANK), F32),
                   jax.ShapeDtypeStruct((t, MLA_ROPE), F32)],
        compiler_params=_cparams(("parallel",)),
        name="mla_proj",
    )(x, g_mix.reshape(1, d), win, g_q.reshape(1, -1), g_kv.reshape(1, -1), cos, sin, wq, wqr, wuk, wuv)


def _mla_flash_kernel(q_ref, k_ref, v_ref, o_ref, *, tq, tk, nh):
    qi = pl.program_id(2)
    qk, dv = MLA_QK_PAD, MLA_V
    qs = [q_ref[:, h * qk:(h + 1) * qk] for h in range(nh)]

    def scores(h, j):
        return _dot_nt(qs[h], k_ref[pl.ds(pl.multiple_of(j * tk, tk), tk), h * qk:(h + 1) * qk])

    def update(h, j, s, m, acc):
        m_new = jnp.maximum(m, s.max(axis=-1, keepdims=True))
        p = jnp.exp2(s - m_new).astype(BF16)
        vj = v_ref[pl.ds(pl.multiple_of(j * tk, tk), tk), 2 * h * dv:2 * (h + 1) * dv]
        return m_new, acc * jnp.exp2(m - m_new) + _dot(p, vj)

    last = (qi * tq) // tk

    def body(j, carry):
        nxt = [scores(h, j + 1) for h in range(nh)]
        return tuple(update(h, j, carry[h][2], *carry[h][:2]) + (nxt[h],) for h in range(nh))

    init = tuple((jnp.full((tq, 1), NEG_BIG, F32), jnp.zeros((tq, 2 * dv), F32), scores(h, 0)) for h in range(nh))
    carry = lax.fori_loop(0, last, body, init)
    row = qi * tq + lax.broadcasted_iota(I32, (tq, tk), 0)
    col = last * tk + lax.broadcasted_iota(I32, (tq, tk), 1)
    for h in range(nh):
        m, acc, s = carry[h]
        _, acc = update(h, last, jnp.where(col <= row, s, NEG_BIG), m, acc)
        o_ref[:, h * dv:(h + 1) * dv] = (acc[:, :dv] / acc[:, dv:]).astype(BF16)


def _mla_flash(q, kk, v, batch, seq, tq, tk):
    nq = seq // tq
    nh = FLASH_HEADS_PER_STEP
    assert tk % tq == 0 and seq % tq == 0 and MLA_HEADS % nh == 0
    return pl.pallas_call(
        functools.partial(_mla_flash_kernel, tq=tq, tk=tk, nh=nh),
        grid=(batch, MLA_HEADS // nh, nq),
        in_specs=[pl.BlockSpec((tq, nh * MLA_QK_PAD), lambda b, h, i: (b * nq + i, h)),
                  pl.BlockSpec((seq, nh * MLA_QK_PAD), lambda b, h, i: (b, h)),
                  pl.BlockSpec((seq, 2 * nh * MLA_V), lambda b, h, i: (b, h))],
        out_specs=pl.BlockSpec((tq, nh * MLA_V), lambda b, h, i: (b * nq + i, h)),
        out_shape=jax.ShapeDtypeStruct((batch * seq, MLA_HEADS * MLA_V), BF16),
        compiler_params=_cparams(("parallel", "parallel", "arbitrary")),
        name="mla_flash",
    )(q, kk, v)


def _bmm_kernel(x_ref, w_ref, o_ref):
    o_ref[0] = _dot(x_ref[0], w_ref[0]).astype(o_ref.dtype)


def _bmm(x, w, out_dtype):
    h, m, k = x.shape
    n = w.shape[2]
    return pl.pallas_call(
        _bmm_kernel,
        grid=(h,),
        in_specs=[pl.BlockSpec((1, m, k), lambda i: (i, 0, 0)), pl.BlockSpec((1, k, n), lambda i: (i, 0, 0))],
        out_specs=pl.BlockSpec((1, m, n), lambda i: (i, 0, 0)),
        out_shape=jax.ShapeDtypeStruct((h, m, n), out_dtype),
        compiler_params=_cparams(("parallel",)),
        name="head_matmul",
    )(x, w)


def _mla_paged_kernel(pt_ref, qx_ref, cnew_ref, knew_ref, ckv_hbm, kpe_hbm, o_ref, cbuf, kbuf, stage, kstage, sem,
                      *, layer, n_pages, g_pages):
    b = pl.program_id(0)
    n_groups = n_pages // g_pages
    r = MLA_KV_RANK
    qx = qx_ref[0]
    qlat = qx[:, :r]
    qpe = qx[:, r:r + MLA_ROPE]

    def copies(row, grp, slot):
        cps = []
        for j in range(g_pages):
            page = pt_ref[row, grp * g_pages + j]
            cols = pl.ds(j * PAGE_SIZE, PAGE_SIZE)
            cps.append(pltpu.make_async_copy(ckv_hbm.at[layer, page], cbuf.at[slot, cols], sem.at[0, slot]))
            cps.append(pltpu.make_async_copy(kpe_hbm.at[layer, page], kbuf.at[slot, :, cols], sem.at[1, slot]))
        return cps

    def start(row, grp, slot):
        for cp in copies(row, grp, slot):
            cp.start()

    def wait(row, grp, slot):
        for cp in copies(row, grp, slot):
            cp.wait()

    @pl.when(b == 0)
    def _():
        start(0, 0, 0)

    cn = cnew_ref[0]
    qf = qx.astype(F32)
    m = (jnp.sum(qf[:, :r] * cn.astype(BF16).astype(F32), axis=-1, keepdims=True)
         + jnp.sum(qf[:, r:r + MLA_ROPE] * knew_ref[0].astype(BF16).astype(F32), axis=-1, keepdims=True))
    l = jnp.ones_like(m)
    acc = jnp.broadcast_to(cn, (qx.shape[0], r))
    pending = None
    for grp in range(n_groups):
        slot = grp % 2
        if grp + 1 < n_groups:
            start(b, grp + 1, 1 - slot)
        else:
            @pl.when(b + 1 < pl.num_programs(0))
            def _():
                start(b + 1, 0, 1 - slot)
        wait(b, grp, slot)
        stage[slot] = cbuf[slot].astype(BF16)
        kstage[slot] = kbuf[slot].astype(BF16)
        s = _dot_nt(qlat, stage[slot]) + _dot(qpe, kstage[slot])
        if pending is not None:
            acc = acc * pending[1] + _dot(pending[0], stage[1 - slot])
        m_new = jnp.maximum(m, s.max(axis=-1, keepdims=True))
        corr = jnp.exp2(m - m_new)
        p = jnp.exp2(s - m_new)
        l = l * corr + p.sum(axis=-1, keepdims=True)
        m = m_new
        pending = (p.astype(BF16), corr)
    acc = acc * pending[1] + _dot(pending[0], stage[(n_groups - 1) % 2])
    o_ref[0] = (acc / l).astype(o_ref.dtype)


def _mla_paged(page_table, qx, c_new, k_new, ckv_pool, kpe_pool_t, layer):
    bd, n_pages = page_table.shape
    assert n_pages % 2 == 0
    g_pages = math.gcd(PAGES_PER_STEP, n_pages // 2)
    hds = qx.shape[1]
    kw = MLA_KV_RANK + 128
    rows = g_pages * PAGE_SIZE
    per_b = lambda b, pt: (b, 0, 0)
    return pl.pallas_call(
        functools.partial(_mla_paged_kernel, layer=layer, n_pages=n_pages, g_pages=g_pages),
        grid_spec=pltpu.PrefetchScalarGridSpec(
            num_scalar_prefetch=1, grid=(bd,),
            in_specs=[pl.BlockSpec((1, hds, kw), per_b), pl.BlockSpec((1, 1, MLA_KV_RANK), per_b),
                      pl.BlockSpec((1, 1, MLA_ROPE), per_b), pl.BlockSpec(memory_space=pl.ANY),
                      pl.BlockSpec(memory_space=pl.ANY)],
            out_specs=pl.BlockSpec((1, hds, MLA_KV_RANK), per_b),
            scratch_shapes=[pltpu.VMEM((2, rows, MLA_KV_RANK), F32), pltpu.VMEM((2, MLA_ROPE, rows), F32),
                            pltpu.VMEM((2, rows, MLA_KV_RANK), BF16), pltpu.VMEM((2, MLA_ROPE, rows), BF16),
                            pltpu.SemaphoreType.DMA((2, 2))]),
        out_shape=jax.ShapeDtypeStruct((bd, hds, MLA_KV_RANK), BF16),
        compiler_params=_cparams(("arbitrary",)),
        name="mla_paged",
    )(page_table, qx, c_new, k_new, ckv_pool, kpe_pool_t)


def _swa_proj_kernel(x_ref, gmix_ref, win_ref, cos_ref, sin_ref, q_ref, k_ref, v_ref, *, qscale):
    nq, nkv = SWA_HEADS * SWA_HEAD_DIM, SWA_KV_HEADS * SWA_HEAD_DIM
    h = _rms(x_ref[...], gmix_ref[...]).astype(BF16)
    proj = _dot(h, win_ref[...])
    cos = cos_ref[...]
    sin = sin_ref[...]
    r0 = nq + 2 * nkv
    for c in range(nq // 128):
        a = c * 128
        q_ref[:, a:a + 128] = ((proj[:, a:a + 128] * cos + proj[:, r0 + a:r0 + a + 128] * sin) * qscale).astype(BF16)
    for c in range(nkv // 128):
        a = c * 128
        k_ref[:, a:a + 128] = proj[:, nq + a:nq + a + 128] * cos + proj[:, r0 + nq + a:r0 + nq + a + 128] * sin
    v_ref[...] = proj[:, nq + nkv:nq + 2 * nkv]


def _swa_proj(x, g_mix, w_in, cos, sin, n_pos_blocks, tm):
    t, d = x.shape
    nq, nkv = SWA_HEADS * SWA_HEAD_DIM, SWA_KV_HEADS * SWA_HEAD_DIM
    win = jnp.concatenate([w_in, _rot_cols(w_in[:, :nq + nkv], SWA_HEAD_DIM)], axis=1).astype(BF16)
    row = lambda i: (i, 0)
    pos = lambda i: (i % n_pos_blocks, 0)
    return pl.pallas_call(
        functools.partial(_swa_proj_kernel, qscale=SWA_HEAD_DIM ** -0.5 * LOG2E),
        grid=(t // tm,),
        in_specs=[pl.BlockSpec((tm, d), row), _full_spec((1, d)), _full_spec(win.shape),
                  pl.BlockSpec((tm, 128), pos), pl.BlockSpec((tm, 128), pos)],
        out_specs=[pl.BlockSpec((tm, nq), row), pl.BlockSpec((tm, nkv), row), pl.BlockSpec((tm, nkv), row)],
        out_shape=[jax.ShapeDtypeStruct((t, nq), BF16), jax.ShapeDtypeStruct((t, nkv), F32),
                   jax.ShapeDtypeStruct((t, nkv), F32)],
        compiler_params=_cparams(("parallel",)),
        name="swa_proj",
    )(x, g_mix.reshape(1, d), win, cos, sin)


def _sink_column(sink_ref, rows_per_head, kvh, n_rep):
    rid = lax.broadcasted_iota(I32, (n_rep * rows_per_head, 1), 0) // rows_per_head
    col = jnp.zeros((n_rep * rows_per_head, 1), F32)
    for g in range(n_rep):
        col = jnp.where(rid == g, sink_ref[kvh * n_rep + g], col)
    return col


def _swa_prompt_kernel(sink_ref, q_ref, kc_ref, kp_ref, vc_ref, vp_ref, o_ref):
    n = pl.program_id(1)
    w = WINDOW
    n_rep = SWA_HEADS // SWA_KV_HEADS
    hd = SWA_HEAD_DIM
    q = q_ref[...]
    kc = kc_ref[...].astype(BF16)
    kp = kp_ref[...].astype(BF16)
    vc = vc_ref[...].astype(BF16)
    vp = vp_ref[...].astype(BF16)
    t = lax.broadcasted_iota(I32, (n_rep * w, 2 * w), 0) % w
    c = lax.broadcasted_iota(I32, (n_rep * w, 2 * w), 1)
    first_key = jnp.where(n > 0, 0, w)
    valid = (c > t) & (c <= t + w) & (c >= first_key)
    for kvh in range(SWA_KV_HEADS):
        ks = slice(kvh * hd, (kvh + 1) * hd)
        kk = jnp.concatenate([kp[:, ks], kc[:, ks]], axis=0)
        vv = jnp.concatenate([vp[:, ks], vc[:, ks]], axis=0)
        qg = jnp.concatenate([q[:, (kvh * n_rep + g) * hd:(kvh * n_rep + g + 1) * hd] for g in range(n_rep)], axis=0)
        s = jnp.where(valid, _dot_nt(qg, kk), NEG_BIG)
        sink = _sink_column(sink_ref, w, kvh, n_rep)
        m = jnp.maximum(s.max(axis=-1, keepdims=True), sink)
        p = jnp.exp2(s - m)
        denom = p.sum(axis=-1, keepdims=True) + jnp.exp2(sink - m)
        o = _dot(p.astype(BF16), vv) / denom
        for g in range(n_rep):
            a = (kvh * n_rep + g) * hd
            o_ref[:, a:a + hd] = o[g * w:(g + 1) * w].astype(BF16)


def _swa_prompt(sinks2, q, k, v, batch, seq):
    nb = seq // WINDOW
    nq, nkv = SWA_HEADS * SWA_HEAD_DIM, SWA_KV_HEADS * SWA_HEAD_DIM
    cur = lambda b, n: (b * nb + n, 0)
    prev = lambda b, n: (b * nb + jnp.maximum(n - 1, 0), 0)
    return pl.pallas_call(
        _swa_prompt_kernel,
        grid=(batch, nb),
        in_specs=[pl.BlockSpec(memory_space=pltpu.SMEM), pl.BlockSpec((WINDOW, nq), cur),
                  pl.BlockSpec((WINDOW, nkv), cur), pl.BlockSpec((WINDOW, nkv), prev),
                  pl.BlockSpec((WINDOW, nkv), cur), pl.BlockSpec((WINDOW, nkv), prev)],
        out_specs=pl.BlockSpec((WINDOW, nq), cur),
        out_shape=jax.ShapeDtypeStruct((batch * seq, nq), BF16),
        compiler_params=_cparams(("parallel", "parallel")),
        name="swa_prompt",
    )(sinks2, q, k, k, v, v)


def _swa_sample_kernel(sink_ref, qbd_ref, kw_ref, vw_ref, kn_ref, vn_ref, o_ref):
    n_rep = SWA_HEADS // SWA_KV_HEADS
    hd = SWA_HEAD_DIM
    qbd = qbd_ref[0]
    kw = kw_ref[0].astype(BF16)
    vw = vw_ref[0].astype(BF16)
    kn = kn_ref[0].astype(BF16).astype(F32)
    vn = vn_ref[0].astype(BF16).astype(F32)
    s = _dot_nt(qbd, kw)
    col = lax.broadcasted_iota(I32, s.shape, 1)
    s = jnp.where(col >= 1, s, NEG_BIG)
    s_new = jnp.sum(qbd.astype(F32) * kn, axis=-1, keepdims=True)
    sink = _sink_column(sink_ref, 1, 0, SWA_HEADS)
    m = jnp.maximum(jnp.maximum(s.max(axis=-1, keepdims=True), s_new), sink)
    p = jnp.exp2(s - m)
    p_new = jnp.exp2(s_new - m)
    denom = p.sum(axis=-1, keepdims=True) + p_new + jnp.exp2(sink - m)
    full = (_dot(p.astype(BF16), vw) + p_new.astype(BF16).astype(F32) * vn) / denom
    rid = lax.broadcasted_iota(I32, (SWA_HEADS, hd), 0) // n_rep
    o = jnp.zeros((SWA_HEADS, hd), F32)
    for kvh in range(SWA_KV_HEADS):
        o = jnp.where(rid == kvh, full[:, kvh * hd:(kvh + 1) * hd], o)
    o_ref[0] = o.astype(BF16)


def _swa_sample(sinks2, qbd, kwin, vwin, k_new, v_new):
    bd = qbd.shape[0]
    nkv = SWA_KV_HEADS * SWA_HEAD_DIM
    per_b = lambda b: (b, 0, 0)
    return pl.pallas_call(
        _swa_sample_kernel,
        grid=(bd,),
        in_specs=[pl.BlockSpec(memory_space=pltpu.SMEM), pl.BlockSpec((1, SWA_HEADS, nkv), per_b),
                  pl.BlockSpec((1, WINDOW, nkv), per_b), pl.BlockSpec((1, WINDOW, nkv), per_b),
                  pl.BlockSpec((1, 1, nkv), per_b), pl.BlockSpec((1, 1, nkv), per_b)],
        out_specs=pl.BlockSpec((1, SWA_HEADS, SWA_HEAD_DIM), per_b),
        out_shape=jax.ShapeDtypeStruct((bd, SWA_HEADS, SWA_HEAD_DIM), BF16),
        compiler_params=_cparams(("parallel",)),
        name="swa_sample",
    )(sinks2, qbd, kwin, vwin, k_new, v_new)


def _block_diag_q(q, n_heads, n_kv, hd):
    b = q.shape[0]
    q3 = q.reshape(b, n_heads, hd)
    owner = (jnp.arange(n_heads) // (n_heads // n_kv))[:, None] == jnp.arange(n_kv)[None, :]
    return jnp.where(owner[None, :, :, None], q3[:, :, None, :], jnp.zeros((), q.dtype)).reshape(b, n_heads, n_kv * hd)


def _sb_proj_kernel(x_ref, gmix_ref, win_ref, q_ref, k_ref, v_ref, kb_ref, vb_ref, *, qscale):
    nq, nkv = SB_HEADS * SB_HEAD_DIM, SB_KV_HEADS * SB_HEAD_DIM
    h = _rms(x_ref[...], gmix_ref[...]).astype(BF16)
    proj = _dot(h, win_ref[...])
    q_ref[...] = (proj[:, :nq] * qscale).astype(BF16)
    k = proj[:, nq:nq + nkv]
    v = proj[:, nq + nkv:]
    k_ref[...] = k
    v_ref[...] = v
    kb_ref[...] = k.astype(BF16)
    vb_ref[...] = v.astype(BF16)


def _sb_proj(x, g_mix, w_in, tm):
    t, d = x.shape
    nq, nkv = SB_HEADS * SB_HEAD_DIM, SB_KV_HEADS * SB_HEAD_DIM
    row = lambda i: (i, 0)
    return pl.pallas_call(
        functools.partial(_sb_proj_kernel, qscale=SB_HEAD_DIM ** -0.5),
        grid=(t // tm,),
        in_specs=[pl.BlockSpec((tm, d), row), _full_spec((1, d)), _full_spec(w_in.shape)],
        out_specs=[pl.BlockSpec((tm, nq), row)] + [pl.BlockSpec((tm, nkv), row)] * 4,
        out_shape=[jax.ShapeDtypeStruct((t, nq), BF16), jax.ShapeDtypeStruct((t, nkv), F32),
                   jax.ShapeDtypeStruct((t, nkv), F32), jax.ShapeDtypeStruct((t, nkv), BF16),
                   jax.ShapeDtypeStruct((t, nkv), BF16)],
        compiler_params=_cparams(("parallel",)),
        name="sb_proj",
    )(x, g_mix.reshape(1, d), w_in.astype(BF16))


def _suffix_matrix():
    r = jnp.arange(SB_BLOCK)
    tri = (r[:, None] > r[None, :]).astype(BF16)
    return jnp.concatenate([tri, jnp.ones((SB_BLOCK, SB_BLOCK), BF16)], axis=1)


def _sb_weights(z, later, tri_ones, strict):
    m = z.shape[0]
    log_1mb = -(jnp.maximum(z, 0.0) + jnp.log(1.0 + jnp.exp(-jnp.abs(z))))
    if strict is not None:
        log_1mb = jnp.where(strict, log_1mb, 0.0)
    hi = log_1mb.astype(BF16)
    lo = (log_1mb - hi.astype(F32)).astype(BF16)
    sums = _dot(jnp.concatenate([hi, lo], axis=0), tri_ones)
    sums = sums[:m] + sums[m:]
    suffix = sums[:, :SB_BLOCK]
    total = sums[:, SB_BLOCK:]
    a = jnp.exp(z + log_1mb + suffix + later)
    if strict is not None:
        a = jnp.where(strict, a, 0.0)
    return a.astype(BF16), total


def _sb_prompt_kernel(q_ref, k_ref, v_ref, tri_ref, o_ref):
    qi = pl.program_id(1)
    blk = SB_BLOCK
    n_rep = SB_HEADS // SB_KV_HEADS
    hd = SB_HEAD_DIM
    q = q_ref[...]
    qs = jnp.concatenate([q[:, h * hd:(h + 1) * hd] for h in range(SB_HEADS)], axis=0)
    tri_ones = tri_ref[...]
    rows = SB_HEADS * blk
    grp = n_rep * blk

    def tile(j, later, strict):
        start = pl.multiple_of(j * blk, blk)
        kt = k_ref[pl.ds(start, blk), :]
        vt = v_ref[pl.ds(start, blk), :]
        z = jnp.concatenate([_dot_nt(qs[kvh * grp:(kvh + 1) * grp], kt[:, kvh * hd:(kvh + 1) * hd])
                             for kvh in range(SB_KV_HEADS)], axis=0)
        a, total = _sb_weights(z, later, tri_ones, strict)
        d_out = jnp.concatenate([_dot(a[kvh * grp:(kvh + 1) * grp], vt[:, kvh * hd:(kvh + 1) * hd])
                                 for kvh in range(SB_KV_HEADS)], axis=0)
        return d_out, total

    t = lax.broadcasted_iota(I32, (rows, blk), 0) % blk
    c = lax.broadcasted_iota(I32, (rows, blk), 1)
    out, later = tile(qi, jnp.zeros((rows, blk), F32), c < t)

    def cond(carry):
        j, alive, _, _ = carry
        return (j >= 0) & (alive > SB_DEAD_LOG)

    def body(carry):
        j, _, later, out = carry
        d_out, total = tile(j, later, None)
        later = later + total
        return j - 1, jnp.max(later), later, out + d_out

    _, _, _, out = lax.while_loop(cond, body, (qi - 1, jnp.max(later), later, out))
    o_ref[...] = jnp.concatenate([out[h * blk:(h + 1) * blk] for h in range(SB_HEADS)], axis=1).astype(BF16)


def _sb_prompt(q, kb, vb, batch, seq):
    nq = seq // SB_BLOCK
    nqd, nkv = SB_HEADS * SB_HEAD_DIM, SB_KV_HEADS * SB_HEAD_DIM
    qmap = lambda b, i: (b * nq + i, 0)
    return pl.pallas_call(
        _sb_prompt_kernel,
        grid=(batch, nq),
        in_specs=[pl.BlockSpec((SB_BLOCK, nqd), qmap), pl.BlockSpec((seq, nkv), lambda b, i: (b, 0)),
                  pl.BlockSpec((seq, nkv), lambda b, i: (b, 0)), _full_spec((SB_BLOCK, 2 * SB_BLOCK))],
        out_specs=pl.BlockSpec((SB_BLOCK, nqd), qmap),
        out_shape=jax.ShapeDtypeStruct((batch * seq, nqd), BF16),
        compiler_params=_cparams(("parallel", "arbitrary")),
        name="sb_prompt",
    )(q, kb, vb, _suffix_matrix())


def _sb_paged_kernel(pt_ref, qbd_ref, tri_ref, k_hbm, v_hbm, o_ref, kbuf, vbuf, sem, *, layer, n_pages):
    b = pl.program_id(0)
    n_rep = SB_HEADS // SB_KV_HEADS
    hd = SB_HEAD_DIM
    base = (b % 2) * 2

    def copies(row, step, slot):
        page = pt_ref[row, n_pages - 1 - step]
        cps = []
        for kvh in range(SB_KV_HEADS):
            cps.append(pltpu.make_async_copy(k_hbm.at[layer, page, :, kvh, :], kbuf.at[slot, kvh], sem.at[0, slot]))
            cps.append(pltpu.make_async_copy(v_hbm.at[layer, page, :, kvh, :], vbuf.at[slot, kvh], sem.at[1, slot]))
        return cps

    def start(row, step, slot):
        for cp in copies(row, step, slot):
            cp.start()

    def wait(row, step, slot):
        for cp in copies(row, step, slot):
            cp.wait()

    @pl.when(b == 0)
    def _():
        start(0, 0, 0)

    @pl.when(b + 1 < pl.num_programs(0))
    def _():
        start(b + 1, 0, 2 - base)

    qbd = qbd_ref[0]
    tri_ones = tri_ref[...]

    def cond(carry):
        i, alive, _, _ = carry
        return (i < n_pages) & (alive > SB_DEAD_LOG)

    def body(carry):
        i, _, later, acc = carry
        slot = base + i % 2

        @pl.when(i + 1 < n_pages)
        def _():
            start(b, i + 1, base + (i + 1) % 2)

        wait(b, i, slot)
        z = _dot_nt(qbd[:, :hd], kbuf[slot, 0].astype(BF16))
        for kvh in range(1, SB_KV_HEADS):
            z = z + _dot_nt(qbd[:, kvh * hd:(kvh + 1) * hd], kbuf[slot, kvh].astype(BF16))
        a, total = _sb_weights(z, later, tri_ones, None)
        d_acc = jnp.concatenate([_dot(a, vbuf[slot, kvh].astype(BF16)) for kvh in range(SB_KV_HEADS)], axis=1)
        later = later + total
        return i + 1, jnp.max(later), later, acc + d_acc

    init = (jnp.int32(0), jnp.float32(0.0), jnp.zeros((SB_HEADS, SB_BLOCK), F32),
            jnp.zeros((SB_HEADS, SB_KV_HEADS * hd), F32))
    n_done, _, _, acc = lax.while_loop(cond, body, init)

    @pl.when(n_done < n_pages)
    def _():
        wait(b, n_done, base + n_done % 2)

    rid = lax.broadcasted_iota(I32, (SB_HEADS, hd), 0) // n_rep
    o = jnp.zeros((SB_HEADS, hd), F32)
    for kvh in range(SB_KV_HEADS):
        o = jnp.where(rid == kvh, acc[:, kvh * hd:(kvh + 1) * hd], o)
    o_ref[0] = o.astype(o_ref.dtype)


def _sb_paged(page_table, qbd, k_pool, v_pool, layer):
    bd, n_pages = page_table.shape
    nkv = SB_KV_HEADS * SB_HEAD_DIM
    page_shape = (4, SB_KV_HEADS, PAGE_SIZE, SB_HEAD_DIM)
    per_b = lambda b, pt: (b, 0, 0)
    return pl.pallas_call(
        functools.partial(_sb_paged_kernel, layer=layer, n_pages=n_pages),
        grid_spec=pltpu.PrefetchScalarGridSpec(
            num_scalar_prefetch=1, grid=(bd,),
            in_specs=[pl.BlockSpec((1, SB_HEADS, nkv), per_b), pl.BlockSpec((SB_BLOCK, 2 * SB_BLOCK), lambda b, pt: (0, 0)),
                      pl.BlockSpec(memory_space=pl.ANY), pl.BlockSpec(memory_space=pl.ANY)],
            out_specs=pl.BlockSpec((1, SB_HEADS, SB_HEAD_DIM), per_b),
            scratch_shapes=[pltpu.VMEM(page_shape, F32), pltpu.VMEM(page_shape, F32), pltpu.SemaphoreType.DMA((2, 4))]),
        out_shape=jax.ShapeDtypeStruct((bd, SB_HEADS, SB_HEAD_DIM), BF16),
        compiler_params=_cparams(("arbitrary",)),
        name="sb_paged",
    )(page_table, qbd, _suffix_matrix(), k_pool, v_pool)


def _lane_min_where(mask, lane, big):
    return jnp.min(jnp.where(mask, lane, big), axis=-1, keepdims=True)


def _oproj_route_kernel(x_ref, o_ref, wo_ref, gffn_ref, wrh_ref, wrl_ref, br_ref, base_ref, tri_ref,
                        xmid_ref, hn_ref, route_ref, gate_ref, cnt_out_ref, cnt_ref):
    @pl.when(pl.program_id(0) == 0)
    def _():
        cnt_ref[...] = base_ref[...]

    x_mid = x_ref[...] + _dot(o_ref[...], wo_ref[...])
    xmid_ref[...] = x_mid
    hn = _rms(x_mid, gffn_ref[...])
    hn_ref[...] = hn
    hh = hn.astype(BF16)
    hl = (hn - hh.astype(F32)).astype(BF16)
    wrh = wrh_ref[...]
    logits = _dot(hh, wrh) + _dot(hl, wrh) + _dot(hh, wrl_ref[...]) + br_ref[...]
    lane_i = lax.broadcasted_iota(I32, logits.shape, 1)
    lane = lane_i.astype(F32)
    big = float(ROUTER_LANES)
    is_grp = lane < N_GROUPS
    lg = jnp.where(is_grp, logits, NEG_BIG)
    mg = lg.max(axis=-1, keepdims=True)
    grp = _lane_min_where(is_grp & (lg == mg), lane, big)
    p_grp = 1.0 / jnp.sum(jnp.where(is_grp, jnp.exp(lg - mg), 0.0), axis=-1, keepdims=True)
    lo = N_GROUPS + grp * EXPERTS_PER_GROUP
    in_grp = (lane >= lo) & (lane < lo + EXPERTS_PER_GROUP)
    lf = jnp.where(in_grp, logits, NEG_BIG)
    v1 = lf.max(axis=-1, keepdims=True)
    i1 = _lane_min_where(in_grp & (lf == v1), lane, big)
    rest = in_grp & (lane != i1)
    lf2 = jnp.where(rest, logits, NEG_BIG)
    v2 = lf2.max(axis=-1, keepdims=True)
    i2 = _lane_min_where(rest & (lf2 == v2), lane, big)
    e2 = jnp.exp(v2 - v1)
    g1 = p_grp / (1.0 + e2)
    g2 = p_grp * e2 / (1.0 + e2)
    gate_ref[...] = jnp.where(lane_i == 0, g1, jnp.where(lane_i == 1, g2, 0.0))
    ex1 = i1 - N_GROUPS
    ex2 = i2 - N_GROUPS
    oh1 = lane == ex1
    oh2 = lane == ex2
    oh = jnp.where(oh1, 1.0, 0.0) + jnp.where(oh2, 1.0, 0.0)
    before = _dot(tri_ref[...], oh.astype(BF16)) + cnt_ref[...]
    r1 = jnp.sum(jnp.where(oh1, before, 0.0), axis=-1, keepdims=True)
    r2 = jnp.sum(jnp.where(oh2, before, 0.0), axis=-1, keepdims=True)
    cnt = cnt_ref[...] + jnp.sum(oh, axis=0, keepdims=True)
    cnt_ref[...] = cnt
    cnt_out_ref[...] = cnt
    route_ref[...] = jnp.where(lane_i == 0, ex1, jnp.where(lane_i == 1, ex2, jnp.where(
        lane_i == 2, r1, jnp.where(lane_i == 3, r2, 0.0)))).astype(I32)


def _router_weights(w_group, b_group, w_fine, b_fine):
    d = w_group.shape[0]
    pad = ROUTER_LANES - N_GROUPS - N_EXPERTS
    w = jnp.concatenate([w_group, w_fine, jnp.zeros((d, pad), F32)], axis=1)
    b = jnp.concatenate([b_group, b_fine, jnp.zeros((pad,), F32)]).reshape(1, ROUTER_LANES)
    w_hi = w.astype(BF16)
    w_lo = (w - w_hi.astype(F32)).astype(BF16)
    return w_hi, w_lo, b


def _oproj_route(x, o, w_o, g_ffn, router, base_counts, tm):
    t, d = x.shape
    kd = o.shape[1]
    w_hi, w_lo, b = router
    row = lambda i: (i, 0)
    r = jnp.arange(tm)
    tri = (r[None, :] < r[:, None]).astype(BF16)
    lanes = ROUTER_LANES
    return pl.pallas_call(
        _oproj_route_kernel,
        grid=(t // tm,),
        in_specs=[pl.BlockSpec((tm, d), row), pl.BlockSpec((tm, kd), row), _full_spec((kd, d)), _full_spec((1, d)),
                  _full_spec(w_hi.shape), _full_spec(w_lo.shape), _full_spec(b.shape), _full_spec((1, lanes)),
                  _full_spec((tm, tm))],
        out_specs=[pl.BlockSpec((tm, d), row), pl.BlockSpec((tm, d), row), pl.BlockSpec((tm, lanes), row),
                   pl.BlockSpec((tm, lanes), row), _full_spec((1, lanes))],
        out_shape=[jax.ShapeDtypeStruct((t, d), F32), jax.ShapeDtypeStruct((t, d), F32),
                   jax.ShapeDtypeStruct((t, lanes), I32), jax.ShapeDtypeStruct((t, lanes), F32),
                   jax.ShapeDtypeStruct((1, lanes), F32)],
        scratch_shapes=[pltpu.VMEM((1, lanes), F32)],
        compiler_params=_cparams(("arbitrary",)),
        name="oproj_route",
    )(x, o, w_o, g_ffn.reshape(1, d), w_hi, w_lo, b, base_counts, tri)


def _moe_kernel(be_ref, tok_ref, dst_ref, nact_ref, hn_hbm, wg_ref, wu_ref, wd_ref, y_hbm,
                xbuf, obuf, wgb, wub, wdb, gsem, ssem, *, t_all, t_pad):
    i = pl.program_id(0)
    nb = pl.num_programs(0)
    blk = MOE_SLOT_BLOCK
    n_act = nact_ref[0]
    slot = i % 2

    def gather_start(block, buf):
        for r in range(blk):
            pltpu.make_async_copy(hn_hbm.at[pl.ds(tok_ref[block * blk + r], 1)], xbuf.at[buf, pl.ds(r, 1)],
                                  gsem.at[buf]).start()

    def gather_wait(buf):
        pltpu.make_async_copy(hn_hbm.at[pl.ds(0, blk)], xbuf.at[buf], gsem.at[buf]).wait()

    def scatter_start(block, buf):
        for r in range(blk):
            pltpu.make_async_copy(obuf.at[buf, pl.ds(r, 1)], y_hbm.at[pl.ds(dst_ref[block * blk + r], 1)],
                                  ssem.at[buf]).start()

    def scatter_wait(buf):
        pltpu.make_async_copy(obuf.at[buf], y_hbm.at[pl.ds(0, blk)], ssem.at[buf]).wait()

    def compute(buf):
        xb = xbuf[buf].astype(BF16)
        gt = _dot(xb, wgb[...])
        up = _dot(xb, wub[...])
        hmid = (gt * (1.0 / (1.0 + jnp.exp(-gt))) * up).astype(BF16)
        obuf[buf] = _dot(hmid, wdb[...])

    @pl.when(i == 0)
    def _():
        obuf[1] = jnp.zeros(obuf.shape[1:], F32)
        gap = t_pad - t_all
        fills = [(2 * t_pad, blk), (2 * t_pad + blk, blk)] + ([(t_all, gap), (t_pad + t_all, gap)] if gap else [])
        cps = [pltpu.make_async_copy(obuf.at[1, pl.ds(0, n)], y_hbm.at[pl.ds(r0, n)], ssem.at[1]) for r0, n in fills]
        for cp in cps:
            cp.start()
        for cp in cps:
            cp.wait()

    @pl.when((i == 0) & (n_act > 0))
    def _():
        gather_start(0, 0)

    @pl.when((i >= 2) & (i - 2 < n_act))
    def _():
        scatter_wait(slot)

    @pl.when(i < n_act)
    def _():
        @pl.when((i == 0) | (be_ref[i] != be_ref[jnp.maximum(i - 1, 0)]))
        def _():
            wgb[...] = wg_ref[0, 0].astype(BF16)
            wub[...] = wu_ref[0, 0].astype(BF16)
            wdb[...] = wd_ref[0, 0].astype(BF16)

        gather_wait(slot)

    steady = (i >= 1) & (i + 1 < n_act)

    @pl.when(steady)
    def _():
        gather_start(i + 1, 1 - slot)
        scatter_start(i - 1, 1 - slot)
        compute(slot)

    @pl.when((i < n_act) & jnp.logical_not(steady))
    def _():
        @pl.when(i + 1 < n_act)
        def _():
            gather_start(i + 1, 1 - slot)

        @pl.when(i >= 1)
        def _():
            scatter_start(i - 1, 1 - slot)

        compute(slot)

    @pl.when((i == n_act) & (i >= 1))
    def _():
        scatter_start(i - 1, 1 - slot)

    @pl.when((i == nb - 1) & (i - 1 < n_act))
    def _():
        scatter_wait(1 - slot)


def _moe_experts(block_e, slot_tok, slot_dst, n_act, hn, w_gate, w_up, w_down, layer, t_pad):
    n_slots = slot_tok.shape[0]
    blk = MOE_SLOT_BLOCK
    n_blocks = n_slots // blk
    d = hn.shape[1]
    de = w_gate.shape[3]
    assert 0 <= t_pad - hn.shape[0] <= blk
    wmap = lambda i, be, tok, dst, na: (layer, be[i], 0, 0)
    return pl.pallas_call(
        functools.partial(_moe_kernel, t_all=hn.shape[0], t_pad=t_pad),
        grid_spec=pltpu.PrefetchScalarGridSpec(
            num_scalar_prefetch=4, grid=(n_blocks,),
            in_specs=[pl.BlockSpec(memory_space=pl.ANY), pl.BlockSpec((1, 1, d, de), wmap),
                      pl.BlockSpec((1, 1, d, de), wmap), pl.BlockSpec((1, 1, de, d), wmap)],
            out_specs=pl.BlockSpec(memory_space=pl.ANY),
            scratch_shapes=[pltpu.VMEM((2, blk, d), F32), pltpu.VMEM((2, blk, d), F32), pltpu.VMEM((d, de), BF16),
                            pltpu.VMEM((d, de), BF16), pltpu.VMEM((de, d), BF16),
                            pltpu.SemaphoreType.DMA((2,)), pltpu.SemaphoreType.DMA((2,))]),
        out_shape=jax.ShapeDtypeStruct((2 * t_pad + 2 * blk, d), F32),
        compiler_params=_cparams(("arbitrary",)),
        name="moe_experts",
    )(block_e, slot_tok, slot_dst, n_act, hn, w_gate, w_up, w_down)


def _moe_slots(route, counts, t_pad):
    n_tok = route.shape[0]
    blk = MOE_SLOT_BLOCK
    n_asg = 2 * n_tok
    experts = jnp.arange(N_EXPERTS, dtype=I32)
    padded = (counts + blk - 1) // blk * blk
    pad_end = jnp.cumsum(padded)
    pad_start = pad_end - padded
    e = route[:, :2]
    rank = route[:, 2:4]
    dest = jnp.sum(jnp.where(e[..., None] == experts, pad_start, 0), axis=-1) + rank
    n_blocks = -(-n_asg // blk) + N_EXPERTS + 1
    slot_asg = jnp.full((n_blocks * blk,), -1, I32).at[dest.reshape(n_asg)].set(jnp.arange(n_asg, dtype=I32))
    slot = jnp.arange(n_blocks * blk, dtype=I32)
    slot_tok = jnp.maximum(slot_asg, 0) >> 1
    spare = 2 * t_pad + ((slot // blk) % 2) * blk + slot % blk
    slot_dst = jnp.where(slot_asg >= 0, (slot_asg & 1) * t_pad + (slot_asg >> 1), spare)
    n_act = (pad_end[-1] // blk).astype(I32)
    blk_start = jnp.arange(n_blocks, dtype=I32) * blk
    block_e = jnp.minimum(jnp.sum((pad_end[None, :] <= blk_start[:, None]).astype(I32), axis=1), N_EXPERTS - 1)
    last_e = jnp.sum(jnp.where(jnp.arange(n_blocks) == jnp.maximum(n_act - 1, 0), block_e, 0))
    block_e = jnp.where(jnp.arange(n_blocks) < n_act, block_e, last_e).astype(I32)
    return block_e, slot_tok, slot_dst, n_act.reshape(1)


def _combine_ple_kernel(x_ref, ya_ref, yb_ref, mg_ref, p_ref, g_ref, wg_ref, wp_ref, gfin_ref, o_ref, *, final):
    mg = mg_ref[...]
    x1 = x_ref[...] + (mg[:, 0:1] * ya_ref[...] + mg[:, 1:2] * yb_ref[...])
    gate = _dot(_rms(x1, g_ref[...]).astype(BF16), wg_ref[...])
    gate = 1.0 / (1.0 + jnp.exp(-gate))
    out = x1 + gate * _dot(p_ref[...].astype(BF16), wp_ref[...])
    if final:
        out = _rms(out, gfin_ref[...])
    o_ref[...] = out


def _combine_ple(x_mid, y2, row0, t_pad, moe_gate, p, g_ple, w_gate, w_proj, g_final, final, tm):
    t, d = x_mid.shape
    pd = p.shape[1]
    assert row0 % tm == 0 and t_pad % tm == 0
    row = lambda i: (i, 0)
    return pl.pallas_call(
        functools.partial(_combine_ple_kernel, final=final),
        grid=(t // tm,),
        in_specs=[pl.BlockSpec((tm, d), row), pl.BlockSpec((tm, d), lambda i: (row0 // tm + i, 0)),
                  pl.BlockSpec((tm, d), lambda i: ((t_pad + row0) // tm + i, 0)), pl.BlockSpec((tm, ROUTER_LANES), row),
                  pl.BlockSpec((tm, pd), row), _full_spec((1, d)), _full_spec((d, d)), _full_spec((pd, d)),
                  _full_spec((1, d))],
        out_specs=pl.BlockSpec((tm, d), row),
        out_shape=jax.ShapeDtypeStruct((t, d), F32),
        compiler_params=_cparams(("parallel",)),
        name="combine_ple",
    )(x_mid, y2, y2, moe_gate, p, g_ple.reshape(1, d), w_gate, w_proj, g_final.reshape(1, d))


def kernel(x_prompt, x_sample, cache_mla_ckv, cache_mla_kpe, cache_swa_k, cache_swa_v, cache_sb_k, cache_sb_v, page_table, p_prompt, p_sample, g_mix, g_ffn, g_ple, g_final, mla_w_in, mla_g_q, mla_g_kv, mla_w_uq, mla_w_uk, mla_w_uv, mla_w_o, swa_w_in, swa_sinks, swa_w_o, sb_w_in, sb_w_o, moe_w_group, moe_b_group, moe_w_fine, moe_b_fine, moe_w_gate, moe_w_up, moe_w_down, ple_w_gate, ple_w_proj):
    batch, seq, d = x_prompt.shape
    bd, dec_seq, _ = x_sample.shape
    assert dec_seq == 1
    depth = g_mix.shape[0]
    n_pages = page_table.shape[1]
    past_len = n_pages * PAGE_SIZE
    tp = batch * seq
    tm_p = _row_tile(tp, 256)
    tm_s = bd
    assert tp % bd == 0 and seq % tm_p == 0
    t_pad = -(-(tp + bd) // tm_p) * tm_p
    page_table = page_table.astype(I32)
    kpe_pool_t = cache_mla_kpe.transpose(0, 1, 3, 2)

    xp = x_prompt.reshape(tp, d)
    xs = x_sample.reshape(bd, d)
    pos_p = jnp.arange(seq)
    pos_s = jnp.full((bd,), past_len)
    rope_p = _rope_tables(pos_p, 32, 128)
    rope_s = _rope_tables(pos_s, 32, 128)

    outs = {k: [] for k in ("p_ckv", "p_kpe", "s_ckv", "s_kpe", "p_wk", "p_wv", "s_wk", "s_wv", "p_sk", "p_sv", "s_sk", "s_sv")}
    for i in range(depth):
        kind, j = i % 3, i // 3
        if kind == 0:
            weights = _mla_weights(mla_w_in[j], mla_w_uq[j], mla_w_uk[j], mla_w_uv[j])
            q_p, kk_p, v_p, ckv_p, kpe_p = _mla_proj(xp, g_mix[i], mla_g_q[j], mla_g_kv[j], weights, *rope_p, seq // tm_p, tm_p)
            q_s, _, _, ckv_s, kpe_s = _mla_proj(xs, g_mix[i], mla_g_q[j], mla_g_kv[j], weights, *rope_s, 1, tm_s)
            o_p = _mla_flash(q_p, kk_p, v_p, batch, seq, _row_tile(seq, FLASH_Q_TILE), _row_tile(seq, FLASH_K_TILE))
            q3 = q_s.reshape(bd, MLA_HEADS, MLA_QK_PAD)
            w_uk_t = mla_w_uk[j].transpose(1, 2, 0).astype(BF16)
            qlat = _bmm(q3[:, :, 128:].transpose(1, 0, 2), w_uk_t, BF16).transpose(1, 0, 2)
            qx = jnp.concatenate([qlat, q3[:, :, :128]], axis=-1)
            o_lat = _mla_paged(page_table, qx, ckv_s.reshape(bd, 1, -1), kpe_s.reshape(bd, 1, -1),
                               cache_mla_ckv, kpe_pool_t, j)
            w_uv_h = mla_w_uv[j].transpose(1, 0, 2).astype(BF16)
            o_s = _bmm(o_lat.transpose(1, 0, 2), w_uv_h, BF16).transpose(1, 0, 2).reshape(bd, MLA_HEADS * MLA_V)
            w_o = mla_w_o[j]
            outs["p_ckv"].append(ckv_p.reshape(batch, seq, -1)); outs["p_kpe"].append(kpe_p.reshape(batch, seq, -1))
            outs["s_ckv"].append(ckv_s.reshape(bd, 1, -1)); outs["s_kpe"].append(kpe_s.reshape(bd, 1, -1))
        elif kind == 1:
            sinks2 = swa_sinks[j] * LOG2E
            q_p, k_p, v_p = _swa_proj(xp, g_mix[i], swa_w_in[j], *rope_p, seq // tm_p, tm_p)
            q_s, k_s, v_s = _swa_proj(xs, g_mix[i], swa_w_in[j], *rope_s, 1, tm_s)
            o_p = _swa_prompt(sinks2, q_p, k_p, v_p, batch, seq)
            nkv = SWA_KV_HEADS * SWA_HEAD_DIM
            kwin = cache_swa_k[j].reshape(bd, WINDOW, nkv)
            vwin = cache_swa_v[j].reshape(bd, WINDOW, nkv)
            qbd = _block_diag_q(q_s, SWA_HEADS, SWA_KV_HEADS, SWA_HEAD_DIM)
            o_s = _swa_sample(sinks2, qbd, kwin, vwin, k_s.reshape(bd, 1, nkv), v_s.reshape(bd, 1, nkv))
            o_s = o_s.reshape(bd, SWA_HEADS * SWA_HEAD_DIM)
            w_o = swa_w_o[j]
            k4 = k_p.reshape(batch, seq, SWA_KV_HEADS, SWA_HEAD_DIM)
            v4 = v_p.reshape(batch, seq, SWA_KV_HEADS, SWA_HEAD_DIM)
            outs["p_wk"].append(k4[:, seq - WINDOW:]); outs["p_wv"].append(v4[:, seq - WINDOW:])
            outs["s_wk"].append(jnp.concatenate([cache_swa_k[j][:, 1:], k_s.reshape(bd, 1, SWA_KV_HEADS, SWA_HEAD_DIM)], axis=1))
            outs["s_wv"].append(jnp.concatenate([cache_swa_v[j][:, 1:], v_s.reshape(bd, 1, SWA_KV_HEADS, SWA_HEAD_DIM)], axis=1))
        else:
            q_p, k_p, v_p, kb_p, vb_p = _sb_proj(xp, g_mix[i], sb_w_in[j], tm_p)
            q_s, k_s, v_s, _, _ = _sb_proj(xs, g_mix[i], sb_w_in[j], tm_s)
            o_p = _sb_prompt(q_p, kb_p, vb_p, batch, seq)
            qbd = _block_diag_q(q_s, SB_HEADS, SB_KV_HEADS, SB_HEAD_DIM)
            o_s = _sb_paged(page_table, qbd, cache_sb_k, cache_sb_v, j).reshape(bd, SB_HEADS * SB_HEAD_DIM)
            w_o = sb_w_o[j]
            outs["p_sk"].append(k_p.reshape(batch, seq, SB_KV_HEADS, SB_HEAD_DIM))
            outs["p_sv"].append(v_p.reshape(batch, seq, SB_KV_HEADS, SB_HEAD_DIM))
            outs["s_sk"].append(k_s.reshape(bd, 1, SB_KV_HEADS, SB_HEAD_DIM))
            outs["s_sv"].append(v_s.reshape(bd, 1, SB_KV_HEADS, SB_HEAD_DIM))

        w_o = w_o.astype(BF16)
        router = _router_weights(moe_w_group[i], moe_b_group[i], moe_w_fine[i], moe_b_fine[i])
        zero_counts = jnp.zeros((1, ROUTER_LANES), F32)
        xmid_p, hn_p, route_p, gate_p, cnt_p = _oproj_route(xp, o_p, w_o, g_ffn[i], router, zero_counts, tm_p)
        xmid_s, hn_s, route_s, gate_s, cnt = _oproj_route(xs, o_s, w_o, g_ffn[i], router, cnt_p, tm_s)
        hn = jnp.concatenate([hn_p, hn_s], axis=0)
        route = jnp.concatenate([route_p[:, :4], route_s[:, :4]], axis=0)
        block_e, slot_tok, slot_dst, n_act = _moe_slots(route, cnt[0, :N_EXPERTS].astype(I32), t_pad)
        y2 = _moe_experts(block_e, slot_tok, slot_dst, n_act, hn, moe_w_gate, moe_w_up, moe_w_down, i, t_pad)
        wg = ple_w_gate[i].astype(BF16)
        wp = ple_w_proj[i].astype(BF16)
        final = i == depth - 1
        xp = _combine_ple(xmid_p, y2, 0, t_pad, gate_p, p_prompt[i].reshape(tp, -1), g_ple[i], wg, wp, g_final, final, tm_p)
        xs = _combine_ple(xmid_s, y2, tp, t_pad, gate_s, p_sample[i].reshape(bd, -1), g_ple[i], wg, wp, g_final, final, tm_s)

    st = lambda k: jnp.stack(outs[k])
    return (xp.reshape(batch, seq, d), xs.reshape(bd, 1, d),
            st("p_ckv"), st("p_kpe"), st("p_wk"), st("p_wv"), st("p_sk"), st("p_sv"),
            st("s_ckv"), st("s_kpe"), st("s_wk"), st("s_wv"), st("s_sk"), st("s_sv"))
```

```python
import functools
import math

import jax
import jax.numpy as jnp
from jax import lax
from jax.experimental import pallas as pl
from jax.experimental.pallas import tpu as pltpu

F32 = jnp.float32
BF16 = jnp.bfloat16
I32 = jnp.int32

NORM_EPS = 1e-6
ROPE_THETA = 10000.0
PAGE_SIZE = 128

MLA_HEADS = 8
MLA_NOPE = 128
MLA_ROPE = 64
MLA_V = 128
MLA_Q_RANK = 512
MLA_KV_RANK = 256
MLA_QK_PAD = 256

SWA_HEADS = 16
SWA_KV_HEADS = 4
SWA_HEAD_DIM = 64
WINDOW = 128

SB_HEADS = 8
SB_KV_HEADS = 4
SB_HEAD_DIM = 128
SB_BLOCK = 128
SB_DEAD_LOG = -100.0

N_GROUPS = 4
EXPERTS_PER_GROUP = 8
N_EXPERTS = N_GROUPS * EXPERTS_PER_GROUP
D_EXPERT = 512
MOE_SLOT_BLOCK = 256
ROUTER_LANES = 128

LOG2E = 1.4426950408889634
NEG_BIG = -1e30
VMEM_LIMIT = 56 * 1024 * 1024
PAGES_PER_STEP = 16
FLASH_Q_TILE = 256
FLASH_K_TILE = 1024
FLASH_HEADS_PER_STEP = 2


def _cparams(sem, vmem=VMEM_LIMIT):
    return pltpu.CompilerParams(dimension_semantics=sem, vmem_limit_bytes=vmem)


def _rms(x, g):
    return x * lax.rsqrt(jnp.mean(x * x, axis=-1, keepdims=True) + NORM_EPS) * g


def _dot(a, b):
    return jnp.dot(a, b, preferred_element_type=F32)


def _dot_nt(a, b):
    return lax.dot_general(a, b, (((1,), (1,)), ((), ())), preferred_element_type=F32)


def _full_spec(shape):
    nd = len(shape)
    return pl.BlockSpec(shape, lambda *_: (0,) * nd)


def _row_tile(n, pref):
    return pref if n % pref == 0 else n


ROW_CHUNKS = 8


def _store_token_tiles(ref, value, lead=()):
    n = value.shape[0]
    for j in range(ROW_CHUNKS):
        ref[lead + (pl.ds(j, n, stride=ROW_CHUNKS), slice(None))] = value[:, j * 128:(j + 1) * 128]


def _load_token_tiles(ref, n, lead=()):
    return jnp.concatenate([ref[lead + (pl.ds(j, n, stride=ROW_CHUNKS), slice(None))] for j in range(ROW_CHUNKS)], axis=1)


def _rope_tables(pos, half, lanes):
    inv = jnp.exp(-math.log(ROPE_THETA) * jnp.arange(half, dtype=F32) / half)
    ang = pos.astype(F32)[:, None] * inv[None, :]
    cos, sin = jnp.cos(ang), jnp.sin(ang)
    reps = lanes // (2 * half)
    if reps >= 1:
        return jnp.tile(jnp.concatenate([cos, cos], 1), (1, reps)), jnp.tile(jnp.concatenate([sin, sin], 1), (1, reps))
    pad = lanes - 2 * half
    n = pos.shape[0]
    return (jnp.concatenate([cos, cos, jnp.ones((n, pad), F32)], 1),
            jnp.concatenate([sin, sin, jnp.zeros((n, pad), F32)], 1))


def _rot_cols(w, head_dim):
    k, n = w.shape
    w3 = w.reshape(k, n // head_dim, head_dim)
    half = head_dim // 2
    return jnp.concatenate([-w3[..., half:], w3[..., :half]], axis=-1).reshape(k, n)


def _mla_proj_kernel(x_ref, gmix_ref, win_ref, gq_ref, gkv_ref, cos_ref, sin_ref, wq_ref, wqr_ref, wuk_ref, wuv_ref,
                     q_ref, kk_ref, v_ref, ckv_ref, kpe_ref, *, qscale):
    h = _rms(x_ref[...], gmix_ref[...]).astype(BF16)
    proj = _dot(h, win_ref[...])
    c_q = _rms(proj[:, :MLA_Q_RANK], gq_ref[...]).astype(BF16)
    c_kv = _rms(proj[:, MLA_Q_RANK:MLA_Q_RANK + MLA_KV_RANK], gkv_ref[...])
    ckv_ref[...] = c_kv
    cos = cos_ref[...]
    sin = sin_ref[...]
    o = MLA_Q_RANK + MLA_KV_RANK
    kpe = proj[:, o:o + 128] * cos + proj[:, o + 128:o + 256] * sin
    kpe_ref[...] = kpe[:, :MLA_ROPE]
    kpe_b = kpe.astype(BF16)
    ckv_b = c_kv.astype(BF16)
    kn = _dot(ckv_b, wuk_ref[...])
    vv = _dot(ckv_b, wuv_ref[...]).astype(BF16)
    ones = jnp.ones((vv.shape[0], MLA_V), BF16)
    for hd in range(MLA_HEADS):
        v_ref[:, 2 * hd * MLA_V:(2 * hd + 1) * MLA_V] = vv[:, hd * MLA_V:(hd + 1) * MLA_V]
        v_ref[:, (2 * hd + 1) * MLA_V:(2 * hd + 2) * MLA_V] = ones
    qe = _dot(c_q, wq_ref[...])
    qr = _dot(c_q, wqr_ref[...])
    for hd in range(MLA_HEADS):
        a = hd * MLA_QK_PAD
        pe = qe[:, a:a + 128] * cos + qr[:, hd * 128:(hd + 1) * 128] * sin
        q_ref[:, a:a + 128] = (pe * qscale).astype(BF16)
        q_ref[:, a + 128:a + 256] = (qe[:, a + 128:a + 256] * qscale).astype(BF16)
        kk_ref[:, a:a + 128] = kpe_b
        kk_ref[:, a + 128:a + 256] = kn[:, hd * 128:(hd + 1) * 128].astype(BF16)


def _mla_weights(w_in, w_uq, w_uk, w_uv):
    d = w_in.shape[0]
    o = MLA_Q_RANK + MLA_KV_RANK
    w_kpe = w_in[:, o:]
    z64 = jnp.zeros((d, 128 - MLA_ROPE), F32)
    win = jnp.concatenate([w_in[:, :o], w_kpe, z64, _rot_cols(w_kpe, MLA_ROPE), z64], axis=1).astype(BF16)
    r = w_uq.shape[0]
    w_nope = w_uq[:, :, :MLA_NOPE]
    w_pe = w_uq[:, :, MLA_NOPE:]
    zq = jnp.zeros((r, MLA_HEADS, 128 - MLA_ROPE), F32)
    wq = jnp.concatenate([w_pe, zq, w_nope], axis=-1).reshape(r, MLA_HEADS * MLA_QK_PAD).astype(BF16)
    w_pe_rot = _rot_cols(w_pe.reshape(r, MLA_HEADS * MLA_ROPE), MLA_ROPE).reshape(r, MLA_HEADS, MLA_ROPE)
    wqr = jnp.concatenate([w_pe_rot, zq], axis=-1).reshape(r, MLA_HEADS * 128).astype(BF16)
    wuk = w_uk.reshape(MLA_KV_RANK, MLA_HEADS * MLA_NOPE).astype(BF16)
    wuv = w_uv.reshape(MLA_KV_RANK, MLA_HEADS * MLA_V).astype(BF16)
    return win, wq, wqr, wuk, wuv


def _mla_proj(x, g_mix, g_q, g_kv, weights, cos, sin, n_pos_blocks, tm):
    win, wq, wqr, wuk, wuv = weights
    t, d = x.shape
    qscale = (MLA_NOPE + MLA_ROPE) ** -0.5 * LOG2E
    row = lambda i: (i, 0)
    pos = lambda i: (i % n_pos_blocks, 0)
    hq = MLA_HEADS * MLA_QK_PAD
    return pl.pallas_call(
        functools.partial(_mla_proj_kernel, qscale=qscale),
        grid=(t // tm,),
        in_specs=[pl.BlockSpec((tm, d), row), _full_spec((1, d)), _full_spec(win.shape), _full_spec((1, MLA_Q_RANK)),
                  _full_spec((1, MLA_KV_RANK)), pl.BlockSpec((tm, 128), pos), pl.BlockSpec((tm, 128), pos),
                  _full_spec(wq.shape), _full_spec(wqr.shape), _full_spec(wuk.shape), _full_spec(wuv.shape)],
        out_specs=[pl.BlockSpec((tm, hq), row), pl.BlockSpec((tm, hq), row), pl.BlockSpec((tm, 2 * MLA_HEADS * MLA_V), row),
                   pl.BlockSpec((tm, MLA_KV_RANK), row), pl.BlockSpec((tm, MLA_ROPE), row)],
        out_shape=[jax.ShapeDtypeStruct((t, hq), BF16), jax.ShapeDtypeStruct((t, hq), BF16),
                   jax.ShapeDtypeStruct((t, 2 * MLA_HEADS * MLA_V), BF16), jax.ShapeDtypeStruct((t, MLA_KV_RANK), F32),
                   jax.ShapeDtypeStruct((t, MLA_ROPE), F32)],
        compiler_params=_cparams(("parallel",)),
        name="mla_proj",
    )(x, g_mix.reshape(1, d), win, g_q.reshape(1, -1), g_kv.reshape(1, -1), cos, sin, wq, wqr, wuk, wuv)


def _mla_flash_kernel(q_ref, k_ref, v_ref, o_ref, *, tq, tk, nh):
    qi = pl.program_id(2)
    qk, dv = MLA_QK_PAD, MLA_V
    qs = [q_ref[:, h * qk:(h + 1) * qk] for h in range(nh)]

    def scores(h, j):
        return _dot_nt(qs[h], k_ref[pl.ds(pl.multiple_of(j * tk, tk), tk), h * qk:(h + 1) * qk])

    def update(h, j, s, m, acc):
        m_new = jnp.maximum(m, s.max(axis=-1, keepdims=True))
        p = jnp.exp2(s - m_new).astype(BF16)
        vj = v_ref[pl.ds(pl.multiple_of(j * tk, tk), tk), 2 * h * dv:2 * (h + 1) * dv]
        return m_new, acc * jnp.exp2(m - m_new) + _dot(p, vj)

    last = (qi * tq) // tk

    def body(j, carry):
        nxt = [scores(h, j + 1) for h in range(nh)]
        return tuple(update(h, j, carry[h][2], *carry[h][:2]) + (nxt[h],) for h in range(nh))

    init = tuple((jnp.full((tq, 1), NEG_BIG, F32), jnp.zeros((tq, 2 * dv), F32), scores(h, 0)) for h in range(nh))
    carry = lax.fori_loop(0, last, body, init)
    row = qi * tq + lax.broadcasted_iota(I32, (tq, tk), 0)
    col = last * tk + lax.broadcasted_iota(I32, (tq, tk), 1)
    for h in range(nh):
        m, acc, s = carry[h]
        _, acc = update(h, last, jnp.where(col <= row, s, NEG_BIG), m, acc)
        o_ref[:, h * dv:(h + 1) * dv] = (acc[:, :dv] / acc[:, dv:]).astype(BF16)


def _mla_flash(q, kk, v, batch, seq, tq, tk):
    nq = seq // tq
    nh = FLASH_HEADS_PER_STEP
    assert tk % tq == 0 and seq % tq == 0 and MLA_HEADS % nh == 0
    return pl.pallas_call(
        functools.partial(_mla_flash_kernel, tq=tq, tk=tk, nh=nh),
        grid=(batch, MLA_HEADS // nh, nq),
        in_specs=[pl.BlockSpec((tq, nh * MLA_QK_PAD), lambda b, h, i: (b * nq + i, h)),
                  pl.BlockSpec((seq, nh * MLA_QK_PAD), lambda b, h, i: (b, h)),
                  pl.BlockSpec((seq, 2 * nh * MLA_V), lambda b, h, i: (b, h))],
        out_specs=pl.BlockSpec((tq, nh * MLA_V), lambda b, h, i: (b * nq + i, h)),
        out_shape=jax.ShapeDtypeStruct((batch * seq, MLA_HEADS * MLA_V), BF16),
        compiler_params=_cparams(("parallel", "parallel", "arbitrary")),
        name="mla_flash",
    )(q, kk, v)


def _bmm_kernel(x_ref, w_ref, o_ref):
    o_ref[0] = _dot(x_ref[0], w_ref[0]).astype(o_ref.dtype)


def _bmm(x, w, out_dtype):
    h, m, k = x.shape
    n = w.shape[2]
    return pl.pallas_call(
        _bmm_kernel,
        grid=(h,),
        in_specs=[pl.BlockSpec((1, m, k), lambda i: (i, 0, 0)), pl.BlockSpec((1, k, n), lambda i: (i, 0, 0))],
        out_specs=pl.BlockSpec((1, m, n), lambda i: (i, 0, 0)),
        out_shape=jax.ShapeDtypeStruct((h, m, n), out_dtype),
        compiler_params=_cparams(("parallel",)),
        name="head_matmul",
    )(x, w)


def _mla_paged_kernel(pt_ref, qx_ref, cnew_ref, knew_ref, ckv_hbm, kpe_hbm, o_ref, cbuf, kbuf, stage, kstage, sem,
                      *, layer, n_pages, g_pages):
    b = pl.program_id(0)
    n_groups = n_pages // g_pages
    r = MLA_KV_RANK
    qx = qx_ref[0]
    qlat = qx[:, :r]
    qpe = qx[:, r:r + MLA_ROPE]

    def copies(row, grp, slot):
        cps = []
        for j in range(g_pages):
            page = pt_ref[row, grp * g_pages + j]
            cols = pl.ds(j * PAGE_SIZE, PAGE_SIZE)
            cps.append(pltpu.make_async_copy(ckv_hbm.at[layer, page], cbuf.at[slot, cols], sem.at[0, slot]))
            cps.append(pltpu.make_async_copy(kpe_hbm.at[layer, page], kbuf.at[slot, :, cols], sem.at[1, slot]))
        return cps

    def start(row, grp, slot):
        for cp in copies(row, grp, slot):
            cp.start()

    def wait(row, grp, slot):
        for cp in copies(row, grp, slot):
            cp.wait()

    @pl.when(b == 0)
    def _():
        start(0, 0, 0)

    cn = cnew_ref[0]
    qf = qx.astype(F32)
    m = (jnp.sum(qf[:, :r] * cn.astype(BF16).astype(F32), axis=-1, keepdims=True)
         + jnp.sum(qf[:, r:r + MLA_ROPE] * knew_ref[0].astype(BF16).astype(F32), axis=-1, keepdims=True))
    l = jnp.ones_like(m)
    acc = jnp.broadcast_to(cn, (qx.shape[0], r))
    pending = None
    for grp in range(n_groups):
        slot = grp % 2
        if grp + 1 < n_groups:
            start(b, grp + 1, 1 - slot)
        else:
            @pl.when(b + 1 < pl.num_programs(0))
            def _():
                start(b + 1, 0, 1 - slot)
        wait(b, grp, slot)
        stage[slot] = cbuf[slot].astype(BF16)
        kstage[slot] = kbuf[slot].astype(BF16)
        s = _dot_nt(qlat, stage[slot]) + _dot(qpe, kstage[slot])
        if pending is not None:
            acc = acc * pending[1] + _dot(pending[0], stage[1 - slot])
        m_new = jnp.maximum(m, s.max(axis=-1, keepdims=True))
        corr = jnp.exp2(m - m_new)
        p = jnp.exp2(s - m_new)
        l = l * corr + p.sum(axis=-1, keepdims=True)
        m = m_new
        pending = (p.astype(BF16), corr)
    acc = acc * pending[1] + _dot(pending[0], stage[(n_groups - 1) % 2])
    o_ref[0] = (acc / l).astype(o_ref.dtype)


def _mla_paged(page_table, qx, c_new, k_new, ckv_pool, kpe_pool_t, layer):
    bd, n_pages = page_table.shape
    assert n_pages % 2 == 0
    g_pages = math.gcd(PAGES_PER_STEP, n_pages // 2)
    hds = qx.shape[1]
    kw = MLA_KV_RANK + 128
    rows = g_pages * PAGE_SIZE
    per_b = lambda b, pt: (b, 0, 0)
    return pl.pallas_call(
        functools.partial(_mla_paged_kernel, layer=layer, n_pages=n_pages, g_pages=g_pages),
        grid_spec=pltpu.PrefetchScalarGridSpec(
            num_scalar_prefetch=1, grid=(bd,),
            in_specs=[pl.BlockSpec((1, hds, kw), per_b), pl.BlockSpec((1, 1, MLA_KV_RANK), per_b),
                      pl.BlockSpec((1, 1, MLA_ROPE), per_b), pl.BlockSpec(memory_space=pl.ANY),
                      pl.BlockSpec(memory_space=pl.ANY)],
            out_specs=pl.BlockSpec((1, hds, MLA_KV_RANK), per_b),
            scratch_shapes=[pltpu.VMEM((2, rows, MLA_KV_RANK), F32), pltpu.VMEM((2, MLA_ROPE, rows), F32),
                            pltpu.VMEM((2, rows, MLA_KV_RANK), BF16), pltpu.VMEM((2, MLA_ROPE, rows), BF16),
                            pltpu.SemaphoreType.DMA((2, 2))]),
        out_shape=jax.ShapeDtypeStruct((bd, hds, MLA_KV_RANK), BF16),
        compiler_params=_cparams(("arbitrary",)),
        name="mla_paged",
    )(page_table, qx, c_new, k_new, ckv_pool, kpe_pool_t)


def _swa_proj_kernel(x_ref, gmix_ref, win_ref, cos_ref, sin_ref, q_ref, k_ref, v_ref, *, qscale):
    nq, nkv = SWA_HEADS * SWA_HEAD_DIM, SWA_KV_HEADS * SWA_HEAD_DIM
    h = _rms(x_ref[...], gmix_ref[...]).astype(BF16)
    proj = _dot(h, win_ref[...])
    cos = cos_ref[...]
    sin = sin_ref[...]
    r0 = nq + 2 * nkv
    for c in range(nq // 128):
        a = c * 128
        q_ref[:, a:a + 128] = ((proj[:, a:a + 128] * cos + proj[:, r0 + a:r0 + a + 128] * sin) * qscale).astype(BF16)
    for c in range(nkv // 128):
        a = c * 128
        k_ref[:, a:a + 128] = proj[:, nq + a:nq + a + 128] * cos + proj[:, r0 + nq + a:r0 + nq + a + 128] * sin
    v_ref[...] = proj[:, nq + nkv:nq + 2 * nkv]


def _swa_proj(x, g_mix, w_in, cos, sin, n_pos_blocks, tm):
    t, d = x.shape
    nq, nkv = SWA_HEADS * SWA_HEAD_DIM, SWA_KV_HEADS * SWA_HEAD_DIM
    win = jnp.concatenate([w_in, _rot_cols(w_in[:, :nq + nkv], SWA_HEAD_DIM)], axis=1).astype(BF16)
    row = lambda i: (i, 0)
    pos = lambda i: (i % n_pos_blocks, 0)
    return pl.pallas_call(
        functools.partial(_swa_proj_kernel, qscale=SWA_HEAD_DIM ** -0.5 * LOG2E),
        grid=(t // tm,),
        in_specs=[pl.BlockSpec((tm, d), row), _full_spec((1, d)), _full_spec(win.shape),
                  pl.BlockSpec((tm, 128), pos), pl.BlockSpec((tm, 128), pos)],
        out_specs=[pl.BlockSpec((tm, nq), row), pl.BlockSpec((tm, nkv), row), pl.BlockSpec((tm, nkv), row)],
        out_shape=[jax.ShapeDtypeStruct((t, nq), BF16), jax.ShapeDtypeStruct((t, nkv), F32),
                   jax.ShapeDtypeStruct((t, nkv), F32)],
        compiler_params=_cparams(("parallel",)),
        name="swa_proj",
    )(x, g_mix.reshape(1, d), win, cos, sin)


def _sink_column(sink_ref, rows_per_head, kvh, n_rep):
    rid = lax.broadcasted_iota(I32, (n_rep * rows_per_head, 1), 0) // rows_per_head
    col = jnp.zeros((n_rep * rows_per_head, 1), F32)
    for g in range(n_rep):
        col = jnp.where(rid == g, sink_ref[kvh * n_rep + g], col)
    return col


def _swa_prompt_kernel(sink_ref, q_ref, kc_ref, kp_ref, vc_ref, vp_ref, o_ref):
    n = pl.program_id(1)
    w = WINDOW
    n_rep = SWA_HEADS // SWA_KV_HEADS
    hd = SWA_HEAD_DIM
    q = q_ref[...]
    kc = kc_ref[...].astype(BF16)
    kp = kp_ref[...].astype(BF16)
    vc = vc_ref[...].astype(BF16)
    vp = vp_ref[...].astype(BF16)
    t = lax.broadcasted_iota(I32, (n_rep * w, 2 * w), 0) % w
    c = lax.broadcasted_iota(I32, (n_rep * w, 2 * w), 1)
    first_key = jnp.where(n > 0, 0, w)
    valid = (c > t) & (c <= t + w) & (c >= first_key)
    for kvh in range(SWA_KV_HEADS):
        ks = slice(kvh * hd, (kvh + 1) * hd)
        kk = jnp.concatenate([kp[:, ks], kc[:, ks]], axis=0)
        vv = jnp.concatenate([vp[:, ks], vc[:, ks]], axis=0)
        qg = jnp.concatenate([q[:, (kvh * n_rep + g) * hd:(kvh * n_rep + g + 1) * hd] for g in range(n_rep)], axis=0)
        s = jnp.where(valid, _dot_nt(qg, kk), NEG_BIG)
        sink = _sink_column(sink_ref, w, kvh, n_rep)
        m = jnp.maximum(s.max(axis=-1, keepdims=True), sink)
        p = jnp.exp2(s - m)
        denom = p.sum(axis=-1, keepdims=True) + jnp.exp2(sink - m)
        o = _dot(p.astype(BF16), vv) / denom
        for g in range(n_rep):
            a = (kvh * n_rep + g) * hd
            o_ref[:, a:a + hd] = o[g * w:(g + 1) * w].astype(BF16)


def _swa_prompt(sinks2, q, k, v, batch, seq):
    nb = seq // WINDOW
    nq, nkv = SWA_HEADS * SWA_HEAD_DIM, SWA_KV_HEADS * SWA_HEAD_DIM
    cur = lambda b, n: (b * nb + n, 0)
    prev = lambda b, n: (b * nb + jnp.maximum(n - 1, 0), 0)
    return pl.pallas_call(
        _swa_prompt_kernel,
        grid=(batch, nb),
        in_specs=[pl.BlockSpec(memory_space=pltpu.SMEM), pl.BlockSpec((WINDOW, nq), cur),
                  pl.BlockSpec((WINDOW, nkv), cur), pl.BlockSpec((WINDOW, nkv), prev),
                  pl.BlockSpec((WINDOW, nkv), cur), pl.BlockSpec((WINDOW, nkv), prev)],
        out_specs=pl.BlockSpec((WINDOW, nq), cur),
        out_shape=jax.ShapeDtypeStruct((batch * seq, nq), BF16),
        compiler_params=_cparams(("parallel", "parallel")),
        name="swa_prompt",
    )(sinks2, q, k, k, v, v)


def _swa_sample_kernel(sink_ref, qbd_ref, kw_ref, vw_ref, kn_ref, vn_ref, o_ref):
    n_rep = SWA_HEADS // SWA_KV_HEADS
    hd = SWA_HEAD_DIM
    qbd = qbd_ref[0]
    kw = kw_ref[0].astype(BF16)
    vw = vw_ref[0].astype(BF16)
    kn = kn_ref[0].astype(BF16).astype(F32)
    vn = vn_ref[0].astype(BF16).astype(F32)
    s = _dot_nt(qbd, kw)
    col = lax.broadcasted_iota(I32, s.shape, 1)
    s = jnp.where(col >= 1, s, NEG_BIG)
    s_new = jnp.sum(qbd.astype(F32) * kn, axis=-1, keepdims=True)
    sink = _sink_column(sink_ref, 1, 0, SWA_HEADS)
    m = jnp.maximum(jnp.maximum(s.max(axis=-1, keepdims=True), s_new), sink)
    p = jnp.exp2(s - m)
    p_new = jnp.exp2(s_new - m)
    denom = p.sum(axis=-1, keepdims=True) + p_new + jnp.exp2(sink - m)
    full = (_dot(p.astype(BF16), vw) + p_new.astype(BF16).astype(F32) * vn) / denom
    rid = lax.broadcasted_iota(I32, (SWA_HEADS, hd), 0) // n_rep
    o = jnp.zeros((SWA_HEADS, hd), F32)
    for kvh in range(SWA_KV_HEADS):
        o = jnp.where(rid == kvh, full[:, kvh * hd:(kvh + 1) * hd], o)
    o_ref[0] = o.astype(BF16)


def _swa_sample(sinks2, qbd, kwin, vwin, k_new, v_new):
    bd = qbd.shape[0]
    nkv = SWA_KV_HEADS * SWA_HEAD_DIM
    per_b = lambda b: (b, 0, 0)
    return pl.pallas_call(
        _swa_sample_kernel,
        grid=(bd,),
        in_specs=[pl.BlockSpec(memory_space=pltpu.SMEM), pl.BlockSpec((1, SWA_HEADS, nkv), per_b),
                  pl.BlockSpec((1, WINDOW, nkv), per_b), pl.BlockSpec((1, WINDOW, nkv), per_b),
                  pl.BlockSpec((1, 1, nkv), per_b), pl.BlockSpec((1, 1, nkv), per_b)],
        out_specs=pl.BlockSpec((1, SWA_HEADS, SWA_HEAD_DIM), per_b),
        out_shape=jax.ShapeDtypeStruct((bd, SWA_HEADS, SWA_HEAD_DIM), BF16),
        compiler_params=_cparams(("parallel",)),
        name="swa_sample",
    )(sinks2, qbd, kwin, vwin, k_new, v_new)


def _block_diag_q(q, n_heads, n_kv, hd):
    b = q.shape[0]
    q3 = q.reshape(b, n_heads, hd)
    owner = (jnp.arange(n_heads) // (n_heads // n_kv))[:, None] == jnp.arange(n_kv)[None, :]
    return jnp.where(owner[None, :, :, None], q3[:, :, None, :], jnp.zeros((), q.dtype)).reshape(b, n_heads, n_kv * hd)


def _sb_proj_kernel(x_ref, gmix_ref, win_ref, q_ref, k_ref, v_ref, kb_ref, vb_ref, *, qscale):
    nq, nkv = SB_HEADS * SB_HEAD_DIM, SB_KV_HEADS * SB_HEAD_DIM
    h = _rms(x_ref[...], gmix_ref[...]).astype(BF16)
    proj = _dot(h, win_ref[...])
    q_ref[...] = (proj[:, :nq] * qscale).astype(BF16)
    k = proj[:, nq:nq + nkv]
    v = proj[:, nq + nkv:]
    k_ref[...] = k
    v_ref[...] = v
    kb_ref[...] = k.astype(BF16)
    vb_ref[...] = v.astype(BF16)


def _sb_proj(x, g_mix, w_in, tm):
    t, d = x.shape
    nq, nkv = SB_HEADS * SB_HEAD_DIM, SB_KV_HEADS * SB_HEAD_DIM
    row = lambda i: (i, 0)
    return pl.pallas_call(
        functools.partial(_sb_proj_kernel, qscale=SB_HEAD_DIM ** -0.5),
        grid=(t // tm,),
        in_specs=[pl.BlockSpec((tm, d), row), _full_spec((1, d)), _full_spec(w_in.shape)],
        out_specs=[pl.BlockSpec((tm, nq), row)] + [pl.BlockSpec((tm, nkv), row)] * 4,
        out_shape=[jax.ShapeDtypeStruct((t, nq), BF16), jax.ShapeDtypeStruct((t, nkv), F32),
                   jax.ShapeDtypeStruct((t, nkv), F32), jax.ShapeDtypeStruct((t, nkv), BF16),
                   jax.ShapeDtypeStruct((t, nkv), BF16)],
        compiler_params=_cparams(("parallel",)),
        name="sb_proj",
    )(x, g_mix.reshape(1, d), w_in.astype(BF16))


def _suffix_matrix():
    r = jnp.arange(SB_BLOCK)
    tri = (r[:, None] > r[None, :]).astype(BF16)
    return jnp.concatenate([tri, jnp.ones((SB_BLOCK, SB_BLOCK), BF16)], axis=1)


def _sb_weights(z, later, tri_ones, strict):
    m = z.shape[0]
    log_1mb = -(jnp.maximum(z, 0.0) + jnp.log(1.0 + jnp.exp(-jnp.abs(z))))
    if strict is not None:
        log_1mb = jnp.where(strict, log_1mb, 0.0)
    hi = log_1mb.astype(BF16)
    lo = (log_1mb - hi.astype(F32)).astype(BF16)
    sums = _dot(jnp.concatenate([hi, lo], axis=0), tri_ones)
    sums = sums[:m] + sums[m:]
    suffix = sums[:, :SB_BLOCK]
    total = sums[:, SB_BLOCK:]
    a = jnp.exp(z + log_1mb + suffix + later)
    if strict is not None:
        a = jnp.where(strict, a, 0.0)
    return a.astype(BF16), total


def _sb_prompt_kernel(q_ref, k_ref, v_ref, tri_ref, o_ref):
    qi = pl.program_id(1)
    blk = SB_BLOCK
    n_rep = SB_HEADS // SB_KV_HEADS
    hd = SB_HEAD_DIM
    q = q_ref[...]
    qs = jnp.concatenate([q[:, h * hd:(h + 1) * hd] for h in range(SB_HEADS)], axis=0)
    tri_ones = tri_ref[...]
    rows = SB_HEADS * blk
    grp = n_rep * blk

    def tile(j, later, strict):
        start = pl.multiple_of(j * blk, blk)
        kt = k_ref[pl.ds(start, blk), :]
        vt = v_ref[pl.ds(start, blk), :]
        z = jnp.concatenate([_dot_nt(qs[kvh * grp:(kvh + 1) * grp], kt[:, kvh * hd:(kvh + 1) * hd])
                             for kvh in range(SB_KV_HEADS)], axis=0)
        a, total = _sb_weights(z, later, tri_ones, strict)
        d_out = jnp.concatenate([_dot(a[kvh * grp:(kvh + 1) * grp], vt[:, kvh * hd:(kvh + 1) * hd])
                                 for kvh in range(SB_KV_HEADS)], axis=0)
        return d_out, total

    t = lax.broadcasted_iota(I32, (rows, blk), 0) % blk
    c = lax.broadcasted_iota(I32, (rows, blk), 1)
    out, later = tile(qi, jnp.zeros((rows, blk), F32), c < t)

    def cond(carry):
        j, alive, _, _ = carry
        return (j >= 0) & (alive > SB_DEAD_LOG)

    def body(carry):
        j, _, later, out = carry
        d_out, total = tile(j, later, None)
        later = later + total
        return j - 1, jnp.max(later), later, out + d_out

    _, _, _, out = lax.while_loop(cond, body, (qi - 1, jnp.max(later), later, out))
    o_ref[...] = jnp.concatenate([out[h * blk:(h + 1) * blk] for h in range(SB_HEADS)], axis=1).astype(BF16)


def _sb_prompt(q, kb, vb, batch, seq):
    nq = seq // SB_BLOCK
    nqd, nkv = SB_HEADS * SB_HEAD_DIM, SB_KV_HEADS * SB_HEAD_DIM
    qmap = lambda b, i: (b * nq + i, 0)
    return pl.pallas_call(
        _sb_prompt_kernel,
        grid=(batch, nq),
        in_specs=[pl.BlockSpec((SB_BLOCK, nqd), qmap), pl.BlockSpec((seq, nkv), lambda b, i: (b, 0)),
                  pl.BlockSpec((seq, nkv), lambda b, i: (b, 0)), _full_spec((SB_BLOCK, 2 * SB_BLOCK))],
        out_specs=pl.BlockSpec((SB_BLOCK, nqd), qmap),
        out_shape=jax.ShapeDtypeStruct((batch * seq, nqd), BF16),
        compiler_params=_cparams(("parallel", "arbitrary")),
        name="sb_prompt",
    )(q, kb, vb, _suffix_matrix())


def _sb_paged_kernel(pt_ref, qbd_ref, tri_ref, k_hbm, v_hbm, o_ref, kbuf, vbuf, sem, *, layer, n_pages):
    b = pl.program_id(0)
    n_rep = SB_HEADS // SB_KV_HEADS
    hd = SB_HEAD_DIM
    base = (b % 2) * 2

    def copies(row, step, slot):
        page = pt_ref[row, n_pages - 1 - step]
        cps = []
        for kvh in range(SB_KV_HEADS):
            cps.append(pltpu.make_async_copy(k_hbm.at[layer, page, :, kvh, :], kbuf.at[slot, kvh], sem.at[0, slot]))
            cps.append(pltpu.make_async_copy(v_hbm.at[layer, page, :, kvh, :], vbuf.at[slot, kvh], sem.at[1, slot]))
        return cps

    def start(row, step, slot):
        for cp in copies(row, step, slot):
            cp.start()

    def wait(row, step, slot):
        for cp in copies(row, step, slot):
            cp.wait()

    @pl.when(b == 0)
    def _():
        start(0, 0, 0)

    @pl.when(b + 1 < pl.num_programs(0))
    def _():
        start(b + 1, 0, 2 - base)

    qbd = qbd_ref[0]
    tri_ones = tri_ref[...]

    def cond(carry):
        i, alive, _, _ = carry
        return (i < n_pages) & (alive > SB_DEAD_LOG)

    def body(carry):
        i, _, later, acc = carry
        slot = base + i % 2

        @pl.when(i + 1 < n_pages)
        def _():
            start(b, i + 1, base + (i + 1) % 2)

        wait(b, i, slot)
        z = _dot_nt(qbd[:, :hd], kbuf[slot, 0].astype(BF16))
        for kvh in range(1, SB_KV_HEADS):
            z = z + _dot_nt(qbd[:, kvh * hd:(kvh + 1) * hd], kbuf[slot, kvh].astype(BF16))
        a, total = _sb_weights(z, later, tri_ones, None)
        d_acc = jnp.concatenate([_dot(a, vbuf[slot, kvh].astype(BF16)) for kvh in range(SB_KV_HEADS)], axis=1)
        later = later + total
        return i + 1, jnp.max(later), later, acc + d_acc

    init = (jnp.int32(0), jnp.float32(0.0), jnp.zeros((SB_HEADS, SB_BLOCK), F32),
            jnp.zeros((SB_HEADS, SB_KV_HEADS * hd), F32))
    n_done, _, _, acc = lax.while_loop(cond, body, init)

    @pl.when(n_done < n_pages)
    def _():
        wait(b, n_done, base + n_done % 2)

    rid = lax.broadcasted_iota(I32, (SB_HEADS, hd), 0) // n_rep
    o = jnp.zeros((SB_HEADS, hd), F32)
    for kvh in range(SB_KV_HEADS):
        o = jnp.where(rid == kvh, acc[:, kvh * hd:(kvh + 1) * hd], o)
    o_ref[0] = o.astype(o_ref.dtype)


def _sb_paged(page_table, qbd, k_pool, v_pool, layer):
    bd, n_pages = page_table.shape
    nkv = SB_KV_HEADS * SB_HEAD_DIM
    page_shape = (4, SB_KV_HEADS, PAGE_SIZE, SB_HEAD_DIM)
    per_b = lambda b, pt: (b, 0, 0)
    return pl.pallas_call(
        functools.partial(_sb_paged_kernel, layer=layer, n_pages=n_pages),
        grid_spec=pltpu.PrefetchScalarGridSpec(
            num_scalar_prefetch=1, grid=(bd,),
            in_specs=[pl.BlockSpec((1, SB_HEADS, nkv), per_b), pl.BlockSpec((SB_BLOCK, 2 * SB_BLOCK), lambda b, pt: (0, 0)),
                      pl.BlockSpec(memory_space=pl.ANY), pl.BlockSpec(memory_space=pl.ANY)],
            out_specs=pl.BlockSpec((1, SB_HEADS, SB_HEAD_DIM), per_b),
            scratch_shapes=[pltpu.VMEM(page_shape, F32), pltpu.VMEM(page_shape, F32), pltpu.SemaphoreType.DMA((2, 4))]),
        out_shape=jax.ShapeDtypeStruct((bd, SB_HEADS, SB_HEAD_DIM), BF16),
        compiler_params=_cparams(("arbitrary",)),
        name="sb_paged",
    )(page_table, qbd, _suffix_matrix(), k_pool, v_pool)


def _lane_min_where(mask, lane, big):
    return jnp.min(jnp.where(mask, lane, big), axis=-1, keepdims=True)


def _oproj_route_kernel(x_ref, o_ref, wo_ref, gffn_ref, wrh_ref, wrl_ref, br_ref, base_ref, tri_ref,
                        xmid_ref, hn_ref, route_ref, gate_ref, cnt_out_ref, cnt_ref):
    @pl.when(pl.program_id(0) == 0)
    def _():
        cnt_ref[...] = base_ref[...]

    x_mid = x_ref[...] + _dot(o_ref[...], wo_ref[...])
    xmid_ref[...] = x_mid
    hn = _rms(x_mid, gffn_ref[...])
    _store_token_tiles(hn_ref, hn)
    hh = hn.astype(BF16)
    hl = (hn - hh.astype(F32)).astype(BF16)
    wrh = wrh_ref[...]
    logits = _dot(hh, wrh) + _dot(hl, wrh) + _dot(hh, wrl_ref[...]) + br_ref[...]
    lane_i = lax.broadcasted_iota(I32, logits.shape, 1)
    lane = lane_i.astype(F32)
    big = float(ROUTER_LANES)
    is_grp = lane < N_GROUPS
    lg = jnp.where(is_grp, logits, NEG_BIG)
    mg = lg.max(axis=-1, keepdims=True)
    grp = _lane_min_where(is_grp & (lg == mg), lane, big)
    p_grp = 1.0 / jnp.sum(jnp.where(is_grp, jnp.exp(lg - mg), 0.0), axis=-1, keepdims=True)
    lo = N_GROUPS + grp * EXPERTS_PER_GROUP
    in_grp = (lane >= lo) & (lane < lo + EXPERTS_PER_GROUP)
    lf = jnp.where(in_grp, logits, NEG_BIG)
    v1 = lf.max(axis=-1, keepdims=True)
    i1 = _lane_min_where(in_grp & (lf == v1), lane, big)
    rest = in_grp & (lane != i1)
    lf2 = jnp.where(rest, logits, NEG_BIG)
    v2 = lf2.max(axis=-1, keepdims=True)
    i2 = _lane_min_where(rest & (lf2 == v2), lane, big)
    e2 = jnp.exp(v2 - v1)
    g1 = p_grp / (1.0 + e2)
    g2 = p_grp * e2 / (1.0 + e2)
    gate_ref[...] = jnp.where(lane_i == 0, g1, jnp.where(lane_i == 1, g2, 0.0))
    ex1 = i1 - N_GROUPS
    ex2 = i2 - N_GROUPS
    oh1 = lane == ex1
    oh2 = lane == ex2
    oh = jnp.where(oh1, 1.0, 0.0) + jnp.where(oh2, 1.0, 0.0)
    before = _dot(tri_ref[...], oh.astype(BF16)) + cnt_ref[...]
    r1 = jnp.sum(jnp.where(oh1, before, 0.0), axis=-1, keepdims=True)
    r2 = jnp.sum(jnp.where(oh2, before, 0.0), axis=-1, keepdims=True)
    cnt = cnt_ref[...] + jnp.sum(oh, axis=0, keepdims=True)
    cnt_ref[...] = cnt
    cnt_out_ref[...] = cnt
    route_ref[...] = jnp.where(lane_i == 0, ex1, jnp.where(lane_i == 1, ex2, jnp.where(
        lane_i == 2, r1, jnp.where(lane_i == 3, r2, 0.0)))).astype(I32)


def _router_weights(w_group, b_group, w_fine, b_fine):
    d = w_group.shape[0]
    pad = ROUTER_LANES - N_GROUPS - N_EXPERTS
    w = jnp.concatenate([w_group, w_fine, jnp.zeros((d, pad), F32)], axis=1)
    b = jnp.concatenate([b_group, b_fine, jnp.zeros((pad,), F32)]).reshape(1, ROUTER_LANES)
    w_hi = w.astype(BF16)
    w_lo = (w - w_hi.astype(F32)).astype(BF16)
    return w_hi, w_lo, b


def _oproj_route(x, o, w_o, g_ffn, router, base_counts, tm):
    t, d = x.shape
    kd = o.shape[1]
    w_hi, w_lo, b = router
    row = lambda i: (i, 0)
    r = jnp.arange(tm)
    tri = (r[None, :] < r[:, None]).astype(BF16)
    lanes = ROUTER_LANES
    return pl.pallas_call(
        _oproj_route_kernel,
        grid=(t // tm,),
        in_specs=[pl.BlockSpec((tm, d), row), pl.BlockSpec((tm, kd), row), _full_spec((kd, d)), _full_spec((1, d)),
                  _full_spec(w_hi.shape), _full_spec(w_lo.shape), _full_spec(b.shape), _full_spec((1, lanes)),
                  _full_spec((tm, tm))],
        out_specs=[pl.BlockSpec((tm, d), row), pl.BlockSpec((tm * ROW_CHUNKS, d // ROW_CHUNKS), row),
                   pl.BlockSpec((tm, lanes), row), pl.BlockSpec((tm, lanes), row), _full_spec((1, lanes))],
        out_shape=[jax.ShapeDtypeStruct((t, d), F32), jax.ShapeDtypeStruct((t * ROW_CHUNKS, d // ROW_CHUNKS), F32),
                   jax.ShapeDtypeStruct((t, lanes), I32), jax.ShapeDtypeStruct((t, lanes), F32),
                   jax.ShapeDtypeStruct((1, lanes), F32)],
        scratch_shapes=[pltpu.VMEM((1, lanes), F32)],
        compiler_params=_cparams(("arbitrary",)),
        name="oproj_route",
    )(x, o, w_o, g_ffn.reshape(1, d), w_hi, w_lo, b, base_counts, tri)


def _moe_kernel(be_ref, tok_ref, dst_ref, nact_ref, hn_hbm, wg_ref, wu_ref, wd_ref, y_hbm,
                xbuf, obuf, wgb, wub, wdb, gsem, ssem, *, t_all, t_pad):
    i = pl.program_id(0)
    nb = pl.num_programs(0)
    blk = MOE_SLOT_BLOCK
    rc = ROW_CHUNKS
    n_act = nact_ref[0]
    slot = i % 2

    def gather_start(block, buf):
        for r in range(blk):
            src = pl.ds(pl.multiple_of(tok_ref[block * blk + r], rc), rc)
            pltpu.make_async_copy(hn_hbm.at[src], xbuf.at[buf, pl.ds(r * rc, rc)], gsem.at[buf]).start()

    def gather_wait(buf):
        pltpu.make_async_copy(hn_hbm.at[pl.ds(0, blk * rc)], xbuf.at[buf], gsem.at[buf]).wait()

    def scatter_start(block, buf):
        for r in range(blk):
            dst = pl.ds(pl.multiple_of(dst_ref[block * blk + r], rc), rc)
            pltpu.make_async_copy(obuf.at[buf, pl.ds(r * rc, rc)], y_hbm.at[dst], ssem.at[buf]).start()

    def scatter_wait(buf):
        pltpu.make_async_copy(obuf.at[buf], y_hbm.at[pl.ds(0, blk * rc)], ssem.at[buf]).wait()

    def compute(buf):
        xb = _load_token_tiles(xbuf, blk, lead=(buf,)).astype(BF16)
        gt = _dot(xb, wgb[...])
        up = _dot(xb, wub[...])
        hmid = (gt * (1.0 / (1.0 + jnp.exp(-gt))) * up).astype(BF16)
        _store_token_tiles(obuf, _dot(hmid, wdb[...]), lead=(buf,))

    @pl.when(i == 0)
    def _():
        obuf[1] = jnp.zeros(obuf.shape[1:], F32)
        gap = t_pad - t_all
        fills = [(2 * t_pad, blk), (2 * t_pad + blk, blk)] + ([(t_all, gap), (t_pad + t_all, gap)] if gap else [])
        cps = [pltpu.make_async_copy(obuf.at[1, pl.ds(0, n * rc)], y_hbm.at[pl.ds(r0 * rc, n * rc)], ssem.at[1])
               for r0, n in fills]
        for cp in cps:
            cp.start()
        for cp in cps:
            cp.wait()

    @pl.when((i == 0) & (n_act > 0))
    def _():
        gather_start(0, 0)

    @pl.when((i >= 2) & (i - 2 < n_act))
    def _():
        scatter_wait(slot)

    @pl.when(i < n_act)
    def _():
        @pl.when((i == 0) | (be_ref[i] != be_ref[jnp.maximum(i - 1, 0)]))
        def _():
            wgb[...] = wg_ref[0, 0].astype(BF16)
            wub[...] = wu_ref[0, 0].astype(BF16)
            wdb[...] = wd_ref[0, 0].astype(BF16)

        gather_wait(slot)

    steady = (i >= 1) & (i + 1 < n_act)

    @pl.when(steady)
    def _():
        gather_start(i + 1, 1 - slot)
        scatter_start(i - 1, 1 - slot)
        compute(slot)

    @pl.when((i < n_act) & jnp.logical_not(steady))
    def _():
        @pl.when(i + 1 < n_act)
        def _():
            gather_start(i + 1, 1 - slot)

        @pl.when(i >= 1)
        def _():
            scatter_start(i - 1, 1 - slot)

        compute(slot)

    @pl.when((i == n_act) & (i >= 1))
    def _():
        scatter_start(i - 1, 1 - slot)

    @pl.when((i == nb - 1) & (i - 1 < n_act))
    def _():
        scatter_wait(1 - slot)


def _moe_experts(block_e, slot_tok, slot_dst, n_act, hn, w_gate, w_up, w_down, layer, t_pad):
    n_slots = slot_tok.shape[0]
    blk = MOE_SLOT_BLOCK
    n_blocks = n_slots // blk
    rc, lanes = ROW_CHUNKS, hn.shape[1]
    t_all = hn.shape[0] // rc
    d = rc * lanes
    de = w_gate.shape[3]
    assert 0 <= t_pad - t_all <= blk
    wmap = lambda i, be, tok, dst, na: (layer, be[i], 0, 0)
    return pl.pallas_call(
        functools.partial(_moe_kernel, t_all=t_all, t_pad=t_pad),
        grid_spec=pltpu.PrefetchScalarGridSpec(
            num_scalar_prefetch=4, grid=(n_blocks,),
            in_specs=[pl.BlockSpec(memory_space=pl.ANY), pl.BlockSpec((1, 1, d, de), wmap),
                      pl.BlockSpec((1, 1, d, de), wmap), pl.BlockSpec((1, 1, de, d), wmap)],
            out_specs=pl.BlockSpec(memory_space=pl.ANY),
            scratch_shapes=[pltpu.VMEM((2, blk * rc, lanes), F32), pltpu.VMEM((2, blk * rc, lanes), F32),
                            pltpu.VMEM((d, de), BF16), pltpu.VMEM((d, de), BF16), pltpu.VMEM((de, d), BF16),
                            pltpu.SemaphoreType.DMA((2,)), pltpu.SemaphoreType.DMA((2,))]),
        out_shape=jax.ShapeDtypeStruct(((2 * t_pad + 2 * blk) * rc, lanes), F32),
        compiler_params=_cparams(("arbitrary",)),
        name="moe_experts",
    )(block_e, slot_tok, slot_dst, n_act, hn, w_gate, w_up, w_down)


def _moe_slots(route, counts, t_pad):
    n_tok = route.shape[0]
    blk = MOE_SLOT_BLOCK
    n_asg = 2 * n_tok
    experts = jnp.arange(N_EXPERTS, dtype=I32)
    padded = (counts + blk - 1) // blk * blk
    pad_end = jnp.cumsum(padded)
    pad_start = pad_end - padded
    e = route[:, :2]
    rank = route[:, 2:4]
    dest = jnp.sum(jnp.where(e[..., None] == experts, pad_start, 0), axis=-1) + rank
    n_blocks = -(-n_asg // blk) + N_EXPERTS + 1
    slot_asg = jnp.full((n_blocks * blk,), -1, I32).at[dest.reshape(n_asg)].set(jnp.arange(n_asg, dtype=I32))
    slot = jnp.arange(n_blocks * blk, dtype=I32)
    slot_tok = (jnp.maximum(slot_asg, 0) >> 1) * ROW_CHUNKS
    spare = 2 * t_pad + ((slot // blk) % 2) * blk + slot % blk
    slot_dst = jnp.where(slot_asg >= 0, (slot_asg & 1) * t_pad + (slot_asg >> 1), spare) * ROW_CHUNKS
    n_act = (pad_end[-1] // blk).astype(I32)
    blk_start = jnp.arange(n_blocks, dtype=I32) * blk
    block_e = jnp.minimum(jnp.sum((pad_end[None, :] <= blk_start[:, None]).astype(I32), axis=1), N_EXPERTS - 1)
    last_e = jnp.sum(jnp.where(jnp.arange(n_blocks) == jnp.maximum(n_act - 1, 0), block_e, 0))
    block_e = jnp.where(jnp.arange(n_blocks) < n_act, block_e, last_e).astype(I32)
    return block_e, slot_tok, slot_dst, n_act.reshape(1)


def _combine_ple_kernel(x_ref, ya_ref, yb_ref, mg_ref, p_ref, g_ref, wg_ref, wp_ref, gfin_ref, o_ref, *, final):
    mg = mg_ref[...]
    n = x_ref.shape[0]
    x1 = x_ref[...] + (mg[:, 0:1] * _load_token_tiles(ya_ref, n) + mg[:, 1:2] * _load_token_tiles(yb_ref, n))
    gate = _dot(_rms(x1, g_ref[...]).astype(BF16), wg_ref[...])
    gate = 1.0 / (1.0 + jnp.exp(-gate))
    out = x1 + gate * _dot(p_ref[...].astype(BF16), wp_ref[...])
    if final:
        out = _rms(out, gfin_ref[...])
    o_ref[...] = out


def _combine_ple(x_mid, y2, row0, t_pad, moe_gate, p, g_ple, w_gate, w_proj, g_final, final, tm):
    t, d = x_mid.shape
    pd = p.shape[1]
    assert row0 % tm == 0 and t_pad % tm == 0
    row = lambda i: (i, 0)
    tile = (tm * ROW_CHUNKS, d // ROW_CHUNKS)
    return pl.pallas_call(
        functools.partial(_combine_ple_kernel, final=final),
        grid=(t // tm,),
        in_specs=[pl.BlockSpec((tm, d), row), pl.BlockSpec(tile, lambda i: (row0 // tm + i, 0)),
                  pl.BlockSpec(tile, lambda i: ((t_pad + row0) // tm + i, 0)), pl.BlockSpec((tm, ROUTER_LANES), row),
                  pl.BlockSpec((tm, pd), row), _full_spec((1, d)), _full_spec((d, d)), _full_spec((pd, d)),
                  _full_spec((1, d))],
        out_specs=pl.BlockSpec((tm, d), row),
        out_shape=jax.ShapeDtypeStruct((t, d), F32),
        compiler_params=_cparams(("parallel",)),
        name="combine_ple",
    )(x_mid, y2, y2, moe_gate, p, g_ple.reshape(1, d), w_gate, w_proj, g_final.reshape(1, d))


def kernel(x_prompt, x_sample, cache_mla_ckv, cache_mla_kpe, cache_swa_k, cache_swa_v, cache_sb_k, cache_sb_v, page_table, p_prompt, p_sample, g_mix, g_ffn, g_ple, g_final, mla_w_in, mla_g_q, mla_g_kv, mla_w_uq, mla_w_uk, mla_w_uv, mla_w_o, swa_w_in, swa_sinks, swa_w_o, sb_w_in, sb_w_o, moe_w_group, moe_b_group, moe_w_fine, moe_b_fine, moe_w_gate, moe_w_up, moe_w_down, ple_w_gate, ple_w_proj):
    batch, seq, d = x_prompt.shape
    bd, dec_seq, _ = x_sample.shape
    assert dec_seq == 1
    depth = g_mix.shape[0]
    n_pages = page_table.shape[1]
    past_len = n_pages * PAGE_SIZE
    tp = batch * seq
    tm_p = _row_tile(tp, 256)
    tm_s = bd
    assert tp % bd == 0 and seq % tm_p == 0
    t_pad = -(-(tp + bd) // tm_p) * tm_p
    page_table = page_table.astype(I32)
    kpe_pool_t = cache_mla_kpe.transpose(0, 1, 3, 2)

    xp = x_prompt.reshape(tp, d)
    xs = x_sample.reshape(bd, d)
    pos_p = jnp.arange(seq)
    pos_s = jnp.full((bd,), past_len)
    rope_p = _rope_tables(pos_p, 32, 128)
    rope_s = _rope_tables(pos_s, 32, 128)

    outs = {k: [] for k in ("p_ckv", "p_kpe", "s_ckv", "s_kpe", "p_wk", "p_wv", "s_wk", "s_wv", "p_sk", "p_sv", "s_sk", "s_sv")}
    for i in range(depth):
        kind, j = i % 3, i // 3
        if kind == 0:
            weights = _mla_weights(mla_w_in[j], mla_w_uq[j], mla_w_uk[j], mla_w_uv[j])
            q_p, kk_p, v_p, ckv_p, kpe_p = _mla_proj(xp, g_mix[i], mla_g_q[j], mla_g_kv[j], weights, *rope_p, seq // tm_p, tm_p)
            q_s, _, _, ckv_s, kpe_s = _mla_proj(xs, g_mix[i], mla_g_q[j], mla_g_kv[j], weights, *rope_s, 1, tm_s)
            o_p = _mla_flash(q_p, kk_p, v_p, batch, seq, _row_tile(seq, FLASH_Q_TILE), _row_tile(seq, FLASH_K_TILE))
            q3 = q_s.reshape(bd, MLA_HEADS, MLA_QK_PAD)
            w_uk_t = mla_w_uk[j].transpose(1, 2, 0).astype(BF16)
            qlat = _bmm(q3[:, :, 128:].transpose(1, 0, 2), w_uk_t, BF16).transpose(1, 0, 2)
            qx = jnp.concatenate([qlat, q3[:, :, :128]], axis=-1)
            o_lat = _mla_paged(page_table, qx, ckv_s.reshape(bd, 1, -1), kpe_s.reshape(bd, 1, -1),
                               cache_mla_ckv, kpe_pool_t, j)
            w_uv_h = mla_w_uv[j].transpose(1, 0, 2).astype(BF16)
            o_s = _bmm(o_lat.transpose(1, 0, 2), w_uv_h, BF16).transpose(1, 0, 2).reshape(bd, MLA_HEADS * MLA_V)
            w_o = mla_w_o[j]
            outs["p_ckv"].append(ckv_p.reshape(batch, seq, -1)); outs["p_kpe"].append(kpe_p.reshape(batch, seq, -1))
            outs["s_ckv"].append(ckv_s.reshape(bd, 1, -1)); outs["s_kpe"].append(kpe_s.reshape(bd, 1, -1))
        elif kind == 1:
            sinks2 = swa_sinks[j] * LOG2E
            q_p, k_p, v_p = _swa_proj(xp, g_mix[i], swa_w_in[j], *rope_p, seq // tm_p, tm_p)
            q_s, k_s, v_s = _swa_proj(xs, g_mix[i], swa_w_in[j], *rope_s, 1, tm_s)
            o_p = _swa_prompt(sinks2, q_p, k_p, v_p, batch, seq)
            nkv = SWA_KV_HEADS * SWA_HEAD_DIM
            kwin = cache_swa_k[j].reshape(bd, WINDOW, nkv)
            vwin = cache_swa_v[j].reshape(bd, WINDOW, nkv)
            qbd = _block_diag_q(q_s, SWA_HEADS, SWA_KV_HEADS, SWA_HEAD_DIM)
            o_s = _swa_sample(sinks2, qbd, kwin, vwin, k_s.reshape(bd, 1, nkv), v_s.reshape(bd, 1, nkv))
            o_s = o_s.reshape(bd, SWA_HEADS * SWA_HEAD_DIM)
            w_o = swa_w_o[j]
            k4 = k_p.reshape(batch, seq, SWA_KV_HEADS, SWA_HEAD_DIM)
            v4 = v_p.reshape(batch, seq, SWA_KV_HEADS, SWA_HEAD_DIM)
            outs["p_wk"].append(k4[:, seq - WINDOW:]); outs["p_wv"].append(v4[:, seq - WINDOW:])
            outs["s_wk"].append(jnp.concatenate([cache_swa_k[j][:, 1:], k_s.reshape(bd, 1, SWA_KV_HEADS, SWA_HEAD_DIM)], axis=1))
            outs["s_wv"].append(jnp.concatenate([cache_swa_v[j][:, 1:], v_s.reshape(bd, 1, SWA_KV_HEADS, SWA_HEAD_DIM)], axis=1))
        else:
            q_p, k_p, v_p, kb_p, vb_p = _sb_proj(xp, g_mix[i], sb_w_in[j], tm_p)
            q_s, k_s, v_s, _, _ = _sb_proj(xs, g_mix[i], sb_w_in[j], tm_s)
            o_p = _sb_prompt(q_p, kb_p, vb_p, batch, seq)
            qbd = _block_diag_q(q_s, SB_HEADS, SB_KV_HEADS, SB_HEAD_DIM)
            o_s = _sb_paged(page_table, qbd, cache_sb_k, cache_sb_v, j).reshape(bd, SB_HEADS * SB_HEAD_DIM)
            w_o = sb_w_o[j]
            outs["p_sk"].append(k_p.reshape(batch, seq, SB_KV_HEADS, SB_HEAD_DIM))
            outs["p_sv"].append(v_p.reshape(batch, seq, SB_KV_HEADS, SB_HEAD_DIM))
            outs["s_sk"].append(k_s.reshape(bd, 1, SB_KV_HEADS, SB_HEAD_DIM))
            outs["s_sv"].append(v_s.reshape(bd, 1, SB_KV_HEADS, SB_HEAD_DIM))

        w_o = w_o.astype(BF16)
        router = _router_weights(moe_w_group[i], moe_b_group[i], moe_w_fine[i], moe_b_fine[i])
        zero_counts = jnp.zeros((1, ROUTER_LANES), F32)
        xmid_p, hn_p, route_p, gate_p, cnt_p = _oproj_route(xp, o_p, w_o, g_ffn[i], router, zero_counts, tm_p)
        xmid_s, hn_s, route_s, gate_s, cnt = _oproj_route(xs, o_s, w_o, g_ffn[i], router, cnt_p, tm_s)
        hn = jnp.concatenate([hn_p, hn_s], axis=0)
        route = jnp.concatenate([route_p[:, :4], route_s[:, :4]], axis=0)
        block_e, slot_tok, slot_dst, n_act = _moe_slots(route, cnt[0, :N_EXPERTS].astype(I32), t_pad)
        y2 = _moe_experts(block_e, slot_tok, slot_dst, n_act, hn, moe_w_gate, moe_w_up, moe_w_down, i, t_pad)
        wg = ple_w_gate[i].astype(BF16)
        wp = ple_w_proj[i].astype(BF16)
        final = i == depth - 1
        xp = _combine_ple(xmid_p, y2, 0, t_pad, gate_p, p_prompt[i].reshape(tp, -1), g_ple[i], wg, wp, g_final, final, tm_p)
        xs = _combine_ple(xmid_s, y2, tp, t_pad, gate_s, p_sample[i].reshape(bd, -1), g_ple[i], wg, wp, g_final, final, tm_s)

    st = lambda k: jnp.stack(outs[k])
    return (xp.reshape(batch, seq, d), xs.reshape(bd, 1, d),
            st("p_ckv"), st("p_kpe"), st("p_wk"), st("p_wv"), st("p_sk"), st("p_sv"),
            st("s_ckv"), st("s_kpe"), st("s_wk"), st("s_wv"), st("s_sk"), st("s_sv"))
```

```python
import functools
import math

import jax
import jax.numpy as jnp
from jax import lax
from jax.experimental import pallas as pl
from jax.experimental.pallas import tpu as pltpu

F32 = jnp.float32
BF16 = jnp.bfloat16
I32 = jnp.int32

NORM_EPS = 1e-6
ROPE_THETA = 10000.0
PAGE_SIZE = 128

MLA_HEADS = 8
MLA_NOPE = 128
MLA_ROPE = 64
MLA_V = 128
MLA_Q_RANK = 512
MLA_KV_RANK = 256
MLA_QK_PAD = 256

SWA_HEADS = 16
SWA_KV_HEADS = 4
SWA_HEAD_DIM = 64
WINDOW = 128

SB_HEADS = 8
SB_KV_HEADS = 4
SB_HEAD_DIM = 128
SB_BLOCK = 128
SB_DEAD_LOG = -100.0

N_GROUPS = 4
EXPERTS_PER_GROUP = 8
N_EXPERTS = N_GROUPS * EXPERTS_PER_GROUP
D_EXPERT = 512
MOE_SLOT_BLOCK = 256
ROUTER_LANES = 128

LOG2E = 1.4426950408889634
NEG_BIG = -1e30
VMEM_LIMIT = 56 * 1024 * 1024
PAGES_PER_STEP = 16
FLASH_Q_TILE = 256
FLASH_K_TILE = 1024
FLASH_HEADS_PER_STEP = 2


def _cparams(sem, vmem=VMEM_LIMIT):
    return pltpu.CompilerParams(dimension_semantics=sem, vmem_limit_bytes=vmem)


def _rms(x, g):
    return x * lax.rsqrt(jnp.mean(x * x, axis=-1, keepdims=True) + NORM_EPS) * g


def _dot(a, b):
    return jnp.dot(a, b, preferred_element_type=F32)


def _dot_nt(a, b):
    return lax.dot_general(a, b, (((1,), (1,)), ((), ())), preferred_element_type=F32)


def _full_spec(shape):
    nd = len(shape)
    return pl.BlockSpec(shape, lambda *_: (0,) * nd)


def _row_tile(n, pref):
    return pref if n % pref == 0 else n


ROW_CHUNKS = 8


def _store_token_tiles(ref, value, lead=()):
    n = value.shape[0]
    for j in range(ROW_CHUNKS):
        ref[lead + (pl.ds(j, n, stride=ROW_CHUNKS), slice(None))] = value[:, j * 128:(j + 1) * 128]


def _load_token_tiles(ref, n, lead=()):
    return jnp.concatenate([ref[lead + (pl.ds(j, n, stride=ROW_CHUNKS), slice(None))] for j in range(ROW_CHUNKS)], axis=1)


def _rope_tables(pos, half, lanes):
    inv = jnp.exp(-math.log(ROPE_THETA) * jnp.arange(half, dtype=F32) / half)
    ang = pos.astype(F32)[:, None] * inv[None, :]
    cos, sin = jnp.cos(ang), jnp.sin(ang)
    reps = lanes // (2 * half)
    if reps >= 1:
        return jnp.tile(jnp.concatenate([cos, cos], 1), (1, reps)), jnp.tile(jnp.concatenate([sin, sin], 1), (1, reps))
    pad = lanes - 2 * half
    n = pos.shape[0]
    return (jnp.concatenate([cos, cos, jnp.ones((n, pad), F32)], 1),
            jnp.concatenate([sin, sin, jnp.zeros((n, pad), F32)], 1))


def _rot_cols(w, head_dim):
    k, n = w.shape
    w3 = w.reshape(k, n // head_dim, head_dim)
    half = head_dim // 2
    return jnp.concatenate([-w3[..., half:], w3[..., :half]], axis=-1).reshape(k, n)


def _mla_proj_kernel(x_ref, gmix_ref, win_ref, gq_ref, gkv_ref, cos_ref, sin_ref, wq_ref, wqr_ref, wuk_ref, wuv_ref,
                     q_ref, kk_ref, v_ref, ckv_ref, kpe_ref, *, qscale):
    h = _rms(x_ref[...], gmix_ref[...]).astype(BF16)
    proj = _dot(h, win_ref[...])
    c_q = _rms(proj[:, :MLA_Q_RANK], gq_ref[...]).astype(BF16)
    c_kv = _rms(proj[:, MLA_Q_RANK:MLA_Q_RANK + MLA_KV_RANK], gkv_ref[...])
    ckv_ref[...] = c_kv
    cos = cos_ref[...]
    sin = sin_ref[...]
    o = MLA_Q_RANK + MLA_KV_RANK
    kpe = proj[:, o:o + 128] * cos + proj[:, o + 128:o + 256] * sin
    kpe_ref[...] = kpe[:, :MLA_ROPE]
    kpe_b = kpe.astype(BF16)
    ckv_b = c_kv.astype(BF16)
    kn = _dot(ckv_b, wuk_ref[...])
    vv = _dot(ckv_b, wuv_ref[...]).astype(BF16)
    ones = jnp.ones((vv.shape[0], MLA_V), BF16)
    for hd in range(MLA_HEADS):
        v_ref[:, 2 * hd * MLA_V:(2 * hd + 1) * MLA_V] = vv[:, hd * MLA_V:(hd + 1) * MLA_V]
        v_ref[:, (2 * hd + 1) * MLA_V:(2 * hd + 2) * MLA_V] = ones
    qe = _dot(c_q, wq_ref[...])
    qr = _dot(c_q, wqr_ref[...])
    for hd in range(MLA_HEADS):
        a = hd * MLA_QK_PAD
        pe = qe[:, a:a + 128] * cos + qr[:, hd * 128:(hd + 1) * 128] * sin
        q_ref[:, a:a + 128] = (pe * qscale).astype(BF16)
        q_ref[:, a + 128:a + 256] = (qe[:, a + 128:a + 256] * qscale).astype(BF16)
        kk_ref[:, a:a + 128] = kpe_b
        kk_ref[:, a + 128:a + 256] = kn[:, hd * 128:(hd + 1) * 128].astype(BF16)


def _mla_weights(w_in, w_uq, w_uk, w_uv):
    d = w_in.shape[0]
    o = MLA_Q_RANK + MLA_KV_RANK
    w_kpe = w_in[:, o:]
    z64 = jnp.zeros((d, 128 - MLA_ROPE), F32)
    win = jnp.concatenate([w_in[:, :o], w_kpe, z64, _rot_cols(w_kpe, MLA_ROPE), z64], axis=1).astype(BF16)
    r = w_uq.shape[0]
    w_nope = w_uq[:, :, :MLA_NOPE]
    w_pe = w_uq[:, :, MLA_NOPE:]
    zq = jnp.zeros((r, MLA_HEADS, 128 - MLA_ROPE), F32)
    wq = jnp.concatenate([w_pe, zq, w_nope], axis=-1).reshape(r, MLA_HEADS * MLA_QK_PAD).astype(BF16)
    w_pe_rot = _rot_cols(w_pe.reshape(r, MLA_HEADS * MLA_ROPE), MLA_ROPE).reshape(r, MLA_HEADS, MLA_ROPE)
    wqr = jnp.concatenate([w_pe_rot, zq], axis=-1).reshape(r, MLA_HEADS * 128).astype(BF16)
    wuk = w_uk.reshape(MLA_KV_RANK, MLA_HEADS * MLA_NOPE).astype(BF16)
    wuv = w_uv.reshape(MLA_KV_RANK, MLA_HEADS * MLA_V).astype(BF16)
    return win, wq, wqr, wuk, wuv


def _mla_proj(x, g_mix, g_q, g_kv, weights, cos, sin, n_pos_blocks, tm):
    win, wq, wqr, wuk, wuv = weights
    t, d = x.shape
    qscale = (MLA_NOPE + MLA_ROPE) ** -0.5 * LOG2E
    row = lambda i: (i, 0)
    pos = lambda i: (i % n_pos_blocks, 0)
    hq = MLA_HEADS * MLA_QK_PAD
    return pl.pallas_call(
        functools.partial(_mla_proj_kernel, qscale=qscale),
        grid=(t // tm,),
        in_specs=[pl.BlockSpec((tm, d), row), _full_spec((1, d)), _full_spec(win.shape), _full_spec((1, MLA_Q_RANK)),
                  _full_spec((1, MLA_KV_RANK)), pl.BlockSpec((tm, 128), pos), pl.BlockSpec((tm, 128), pos),
                  _full_spec(wq.shape), _full_spec(wqr.shape), _full_spec(wuk.shape), _full_spec(wuv.shape)],
        out_specs=[pl.BlockSpec((tm, hq), row), pl.BlockSpec((tm, hq), row), pl.BlockSpec((tm, 2 * MLA_HEADS * MLA_V), row),
                   pl.BlockSpec((tm, MLA_KV_RANK), row), pl.BlockSpec((tm, MLA_ROPE), row)],
        out_shape=[jax.ShapeDtypeStruct((t, hq), BF16), jax.ShapeDtypeStruct((t, hq), BF16),
                   jax.ShapeDtypeStruct((t, 2 * MLA_HEADS * MLA_V), BF16), jax.ShapeDtypeStruct((t, MLA_KV_RANK), F32),
                   jax.ShapeDtypeStruct((t, MLA_ROPE), F32)],
        compiler_params=_cparams(("parallel",)),
        name="mla_proj",
    )(x, g_mix.reshape(1, d), win, g_q.reshape(1, -1), g_kv.reshape(1, -1), cos, sin, wq, wqr, wuk, wuv)


def _mla_flash_kernel(q_ref, k_ref, v_ref, o_ref, *, tq, tk, nh):
    qi = pl.program_id(2)
    qk, dv = MLA_QK_PAD, MLA_V
    qs = [q_ref[:, h * qk:(h + 1) * qk] for h in range(nh)]

    def scores(h, j):
        return _dot_nt(qs[h], k_ref[pl.ds(pl.multiple_of(j * tk, tk), tk), h * qk:(h + 1) * qk])

    def update(h, j, s, m, acc):
        m_new = jnp.maximum(m, s.max(axis=-1, keepdims=True))
        p = jnp.exp2(s - m_new).astype(BF16)
        vj = v_ref[pl.ds(pl.multiple_of(j * tk, tk), tk), 2 * h * dv:2 * (h + 1) * dv]
        return m_new, acc * jnp.exp2(m - m_new) + _dot(p, vj)

    last = (qi * tq) // tk

    def body(j, carry):
        nxt = [scores(h, j + 1) for h in range(nh)]
        return tuple(update(h, j, carry[h][2], *carry[h][:2]) + (nxt[h],) for h in range(nh))

    init = tuple((jnp.full((tq, 1), NEG_BIG, F32), jnp.zeros((tq, 2 * dv), F32), scores(h, 0)) for h in range(nh))
    carry = lax.fori_loop(0, last, body, init)
    row = qi * tq + lax.broadcasted_iota(I32, (tq, tk), 0)
    col = last * tk + lax.broadcasted_iota(I32, (tq, tk), 1)
    for h in range(nh):
        m, acc, s = carry[h]
        _, acc = update(h, last, jnp.where(col <= row, s, NEG_BIG), m, acc)
        o_ref[:, h * dv:(h + 1) * dv] = (acc[:, :dv] / acc[:, dv:]).astype(BF16)


def _mla_flash(q, kk, v, batch, seq, tq, tk):
    nq = seq // tq
    nh = FLASH_HEADS_PER_STEP
    assert tk % tq == 0 and seq % tq == 0 and MLA_HEADS % nh == 0
    return pl.pallas_call(
        functools.partial(_mla_flash_kernel, tq=tq, tk=tk, nh=nh),
        grid=(batch, MLA_HEADS // nh, nq),
        in_specs=[pl.BlockSpec((tq, nh * MLA_QK_PAD), lambda b, h, i: (b * nq + i, h)),
                  pl.BlockSpec((seq, nh * MLA_QK_PAD), lambda b, h, i: (b, h)),
                  pl.BlockSpec((seq, 2 * nh * MLA_V), lambda b, h, i: (b, h))],
        out_specs=pl.BlockSpec((tq, nh * MLA_V), lambda b, h, i: (b * nq + i, h)),
        out_shape=jax.ShapeDtypeStruct((batch * seq, MLA_HEADS * MLA_V), BF16),
        compiler_params=_cparams(("parallel", "parallel", "arbitrary")),
        name="mla_flash",
    )(q, kk, v)


def _bmm_kernel(x_ref, w_ref, o_ref):
    o_ref[0] = _dot(x_ref[0], w_ref[0]).astype(o_ref.dtype)


def _bmm(x, w, out_dtype):
    h, m, k = x.shape
    n = w.shape[2]
    return pl.pallas_call(
        _bmm_kernel,
        grid=(h,),
        in_specs=[pl.BlockSpec((1, m, k), lambda i: (i, 0, 0)), pl.BlockSpec((1, k, n), lambda i: (i, 0, 0))],
        out_specs=pl.BlockSpec((1, m, n), lambda i: (i, 0, 0)),
        out_shape=jax.ShapeDtypeStruct((h, m, n), out_dtype),
        compiler_params=_cparams(("parallel",)),
        name="head_matmul",
    )(x, w)


def _mla_paged_kernel(pt_ref, qx_ref, cnew_ref, knew_ref, ckv_hbm, kpe_hbm, o_ref, cbuf, kbuf, stage, kstage, sem,
                      *, layer, n_pages, g_pages):
    b = pl.program_id(0)
    n_groups = n_pages // g_pages
    r = MLA_KV_RANK
    qx = qx_ref[0]
    qlat = qx[:, :r]
    qpe = qx[:, r:r + MLA_ROPE]

    def copies(row, grp, slot):
        cps = []
        for j in range(g_pages):
            page = pt_ref[row, grp * g_pages + j]
            cols = pl.ds(j * PAGE_SIZE, PAGE_SIZE)
            cps.append(pltpu.make_async_copy(ckv_hbm.at[layer, page], cbuf.at[slot, cols], sem.at[0, slot]))
            cps.append(pltpu.make_async_copy(kpe_hbm.at[layer, page], kbuf.at[slot, :, cols], sem.at[1, slot]))
        return cps

    def start(row, grp, slot):
        for cp in copies(row, grp, slot):
            cp.start()

    def wait(row, grp, slot):
        for cp in copies(row, grp, slot):
            cp.wait()

    n_fetch, ahead = cbuf.shape[0], cbuf.shape[0] - 1

    def fetch_slot(row, grp):
        return (row * n_groups + grp) % n_fetch

    @pl.when(b == 0)
    def _():
        for g0 in range(ahead):
            start(0, g0, fetch_slot(0, g0))

    cn = cnew_ref[0]
    qf = qx.astype(F32)
    m = (jnp.sum(qf[:, :r] * cn.astype(BF16).astype(F32), axis=-1, keepdims=True)
         + jnp.sum(qf[:, r:r + MLA_ROPE] * knew_ref[0].astype(BF16).astype(F32), axis=-1, keepdims=True))
    l = jnp.ones_like(m)
    acc = jnp.broadcast_to(cn, (qx.shape[0], r))
    pending = None
    for grp in range(n_groups):
        slot = grp % 2
        nxt = grp + ahead
        if nxt < n_groups:
            start(b, nxt, fetch_slot(b, nxt))
        else:
            @pl.when(b + 1 < pl.num_programs(0))
            def _():
                start(b + 1, nxt - n_groups, fetch_slot(b + 1, nxt - n_groups))
        fs = fetch_slot(b, grp)
        wait(b, grp, fs)
        stage[slot] = cbuf[fs].astype(BF16)
        kstage[slot] = kbuf[fs].astype(BF16)
        s = _dot_nt(qlat, stage[slot]) + _dot(qpe, kstage[slot])
        if pending is not None:
            acc = acc * pending[1] + _dot(pending[0], stage[1 - slot])
        m_new = jnp.maximum(m, s.max(axis=-1, keepdims=True))
        corr = jnp.exp2(m - m_new)
        p = jnp.exp2(s - m_new)
        l = l * corr + p.sum(axis=-1, keepdims=True)
        m = m_new
        pending = (p.astype(BF16), corr)
    acc = acc * pending[1] + _dot(pending[0], stage[(n_groups - 1) % 2])
    o_ref[0] = (acc / l).astype(o_ref.dtype)


def _mla_paged(page_table, qx, c_new, k_new, ckv_pool, kpe_pool_t, layer):
    bd, n_pages = page_table.shape
    assert n_pages % 2 == 0
    g_pages = math.gcd(PAGES_PER_STEP, n_pages // 2)
    n_fetch = 3
    hds = qx.shape[1]
    kw = MLA_KV_RANK + 128
    rows = g_pages * PAGE_SIZE
    per_b = lambda b, pt: (b, 0, 0)
    return pl.pallas_call(
        functools.partial(_mla_paged_kernel, layer=layer, n_pages=n_pages, g_pages=g_pages),
        grid_spec=pltpu.PrefetchScalarGridSpec(
            num_scalar_prefetch=1, grid=(bd,),
            in_specs=[pl.BlockSpec((1, hds, kw), per_b), pl.BlockSpec((1, 1, MLA_KV_RANK), per_b),
                      pl.BlockSpec((1, 1, MLA_ROPE), per_b), pl.BlockSpec(memory_space=pl.ANY),
                      pl.BlockSpec(memory_space=pl.ANY)],
            out_specs=pl.BlockSpec((1, hds, MLA_KV_RANK), per_b),
            scratch_shapes=[pltpu.VMEM((n_fetch, rows, MLA_KV_RANK), F32), pltpu.VMEM((n_fetch, MLA_ROPE, rows), F32),
                            pltpu.VMEM((2, rows, MLA_KV_RANK), BF16), pltpu.VMEM((2, MLA_ROPE, rows), BF16),
                            pltpu.SemaphoreType.DMA((2, n_fetch))]),
        out_shape=jax.ShapeDtypeStruct((bd, hds, MLA_KV_RANK), BF16),
        compiler_params=_cparams(("arbitrary",)),
        name="mla_paged",
    )(page_table, qx, c_new, k_new, ckv_pool, kpe_pool_t)


def _swa_proj_kernel(x_ref, gmix_ref, win_ref, cos_ref, sin_ref, q_ref, k_ref, v_ref, *, qscale):
    nq, nkv = SWA_HEADS * SWA_HEAD_DIM, SWA_KV_HEADS * SWA_HEAD_DIM
    h = _rms(x_ref[...], gmix_ref[...]).astype(BF16)
    proj = _dot(h, win_ref[...])
    cos = cos_ref[...]
    sin = sin_ref[...]
    r0 = nq + 2 * nkv
    for c in range(nq // 128):
        a = c * 128
        q_ref[:, a:a + 128] = ((proj[:, a:a + 128] * cos + proj[:, r0 + a:r0 + a + 128] * sin) * qscale).astype(BF16)
    for c in range(nkv // 128):
        a = c * 128
        k_ref[:, a:a + 128] = proj[:, nq + a:nq + a + 128] * cos + proj[:, r0 + nq + a:r0 + nq + a + 128] * sin
    v_ref[...] = proj[:, nq + nkv:nq + 2 * nkv]


def _swa_proj(x, g_mix, w_in, cos, sin, n_pos_blocks, tm):
    t, d = x.shape
    nq, nkv = SWA_HEADS * SWA_HEAD_DIM, SWA_KV_HEADS * SWA_HEAD_DIM
    win = jnp.concatenate([w_in, _rot_cols(w_in[:, :nq + nkv], SWA_HEAD_DIM)], axis=1).astype(BF16)
    row = lambda i: (i, 0)
    pos = lambda i: (i % n_pos_blocks, 0)
    return pl.pallas_call(
        functools.partial(_swa_proj_kernel, qscale=SWA_HEAD_DIM ** -0.5 * LOG2E),
        grid=(t // tm,),
        in_specs=[pl.BlockSpec((tm, d), row), _full_spec((1, d)), _full_spec(win.shape),
                  pl.BlockSpec((tm, 128), pos), pl.BlockSpec((tm, 128), pos)],
        out_specs=[pl.BlockSpec((tm, nq), row), pl.BlockSpec((tm, nkv), row), pl.BlockSpec((tm, nkv), row)],
        out_shape=[jax.ShapeDtypeStruct((t, nq), BF16), jax.ShapeDtypeStruct((t, nkv), F32),
                   jax.ShapeDtypeStruct((t, nkv), F32)],
        compiler_params=_cparams(("parallel",)),
        name="swa_proj",
    )(x, g_mix.reshape(1, d), win, cos, sin)


def _sink_column(sink_ref, rows_per_head, kvh, n_rep):
    rid = lax.broadcasted_iota(I32, (n_rep * rows_per_head, 1), 0) // rows_per_head
    col = jnp.zeros((n_rep * rows_per_head, 1), F32)
    for g in range(n_rep):
        col = jnp.where(rid == g, sink_ref[kvh * n_rep + g], col)
    return col


def _swa_prompt_kernel(sink_ref, q_ref, kc_ref, kp_ref, vc_ref, vp_ref, o_ref):
    n = pl.program_id(1)
    w = WINDOW
    n_rep = SWA_HEADS // SWA_KV_HEADS
    hd = SWA_HEAD_DIM
    q = q_ref[...]
    kc = kc_ref[...].astype(BF16)
    kp = kp_ref[...].astype(BF16)
    vc = vc_ref[...].astype(BF16)
    vp = vp_ref[...].astype(BF16)
    t = lax.broadcasted_iota(I32, (n_rep * w, 2 * w), 0) % w
    c = lax.broadcasted_iota(I32, (n_rep * w, 2 * w), 1)
    first_key = jnp.where(n > 0, 0, w)
    valid = (c > t) & (c <= t + w) & (c >= first_key)
    for kvh in range(SWA_KV_HEADS):
        ks = slice(kvh * hd, (kvh + 1) * hd)
        kk = jnp.concatenate([kp[:, ks], kc[:, ks]], axis=0)
        vv = jnp.concatenate([vp[:, ks], vc[:, ks]], axis=0)
        qg = jnp.concatenate([q[:, (kvh * n_rep + g) * hd:(kvh * n_rep + g + 1) * hd] for g in range(n_rep)], axis=0)
        s = jnp.where(valid, _dot_nt(qg, kk), NEG_BIG)
        sink = _sink_column(sink_ref, w, kvh, n_rep)
        m = jnp.maximum(s.max(axis=-1, keepdims=True), sink)
        p = jnp.exp2(s - m)
        denom = p.sum(axis=-1, keepdims=True) + jnp.exp2(sink - m)
        o = _dot(p.astype(BF16), vv) / denom
        for g in range(n_rep):
            a = (kvh * n_rep + g) * hd
            o_ref[:, a:a + hd] = o[g * w:(g + 1) * w].astype(BF16)


def _swa_prompt(sinks2, q, k, v, batch, seq):
    nb = seq // WINDOW
    nq, nkv = SWA_HEADS * SWA_HEAD_DIM, SWA_KV_HEADS * SWA_HEAD_DIM
    cur = lambda b, n: (b * nb + n, 0)
    prev = lambda b, n: (b * nb + jnp.maximum(n - 1, 0), 0)
    return pl.pallas_call(
        _swa_prompt_kernel,
        grid=(batch, nb),
        in_specs=[pl.BlockSpec(memory_space=pltpu.SMEM), pl.BlockSpec((WINDOW, nq), cur),
                  pl.BlockSpec((WINDOW, nkv), cur), pl.BlockSpec((WINDOW, nkv), prev),
                  pl.BlockSpec((WINDOW, nkv), cur), pl.BlockSpec((WINDOW, nkv), prev)],
        out_specs=pl.BlockSpec((WINDOW, nq), cur),
        out_shape=jax.ShapeDtypeStruct((batch * seq, nq), BF16),
        compiler_params=_cparams(("parallel", "parallel")),
        name="swa_prompt",
    )(sinks2, q, k, k, v, v)


def _swa_sample_kernel(sink_ref, qbd_ref, kw_ref, vw_ref, kn_ref, vn_ref, o_ref):
    n_rep = SWA_HEADS // SWA_KV_HEADS
    hd = SWA_HEAD_DIM
    qbd = qbd_ref[0]
    kw = kw_ref[0].astype(BF16)
    vw = vw_ref[0].astype(BF16)
    kn = kn_ref[0].astype(BF16).astype(F32)
    vn = vn_ref[0].astype(BF16).astype(F32)
    s = _dot_nt(qbd, kw)
    col = lax.broadcasted_iota(I32, s.shape, 1)
    s = jnp.where(col >= 1, s, NEG_BIG)
    s_new = jnp.sum(qbd.astype(F32) * kn, axis=-1, keepdims=True)
    sink = _sink_column(sink_ref, 1, 0, SWA_HEADS)
    m = jnp.maximum(jnp.maximum(s.max(axis=-1, keepdims=True), s_new), sink)
    p = jnp.exp2(s - m)
    p_new = jnp.exp2(s_new - m)
    denom = p.sum(axis=-1, keepdims=True) + p_new + jnp.exp2(sink - m)
    full = (_dot(p.astype(BF16), vw) + p_new.astype(BF16).astype(F32) * vn) / denom
    rid = lax.broadcasted_iota(I32, (SWA_HEADS, hd), 0) // n_rep
    o = jnp.zeros((SWA_HEADS, hd), F32)
    for kvh in range(SWA_KV_HEADS):
        o = jnp.where(rid == kvh, full[:, kvh * hd:(kvh + 1) * hd], o)
    o_ref[0] = o.astype(BF16)


def _swa_sample(sinks2, qbd, kwin, vwin, k_new, v_new):
    bd = qbd.shape[0]
    nkv = SWA_KV_HEADS * SWA_HEAD_DIM
    per_b = lambda b: (b, 0, 0)
    return pl.pallas_call(
        _swa_sample_kernel,
        grid=(bd,),
        in_specs=[pl.BlockSpec(memory_space=pltpu.SMEM), pl.BlockSpec((1, SWA_HEADS, nkv), per_b),
                  pl.BlockSpec((1, WINDOW, nkv), per_b), pl.BlockSpec((1, WINDOW, nkv), per_b),
                  pl.BlockSpec((1, 1, nkv), per_b), pl.BlockSpec((1, 1, nkv), per_b)],
        out_specs=pl.BlockSpec((1, SWA_HEADS, SWA_HEAD_DIM), per_b),
        out_shape=jax.ShapeDtypeStruct((bd, SWA_HEADS, SWA_HEAD_DIM), BF16),
        compiler_params=_cparams(("parallel",)),
        name="swa_sample",
    )(sinks2, qbd, kwin, vwin, k_new, v_new)


def _block_diag_q(q, n_heads, n_kv, hd):
    b = q.shape[0]
    q3 = q.reshape(b, n_heads, hd)
    owner = (jnp.arange(n_heads) // (n_heads // n_kv))[:, None] == jnp.arange(n_kv)[None, :]
    return jnp.where(owner[None, :, :, None], q3[:, :, None, :], jnp.zeros((), q.dtype)).reshape(b, n_heads, n_kv * hd)


def _sb_proj_kernel(x_ref, gmix_ref, win_ref, q_ref, k_ref, v_ref, kb_ref, vb_ref, *, qscale):
    nq, nkv = SB_HEADS * SB_HEAD_DIM, SB_KV_HEADS * SB_HEAD_DIM
    h = _rms(x_ref[...], gmix_ref[...]).astype(BF16)
    proj = _dot(h, win_ref[...])
    q_ref[...] = (proj[:, :nq] * qscale).astype(BF16)
    k = proj[:, nq:nq + nkv]
    v = proj[:, nq + nkv:]
    k_ref[...] = k
    v_ref[...] = v
    kb_ref[...] = k.astype(BF16)
    vb_ref[...] = v.astype(BF16)


def _sb_proj(x, g_mix, w_in, tm):
    t, d = x.shape
    nq, nkv = SB_HEADS * SB_HEAD_DIM, SB_KV_HEADS * SB_HEAD_DIM
    row = lambda i: (i, 0)
    return pl.pallas_call(
        functools.partial(_sb_proj_kernel, qscale=SB_HEAD_DIM ** -0.5),
        grid=(t // tm,),
        in_specs=[pl.BlockSpec((tm, d), row), _full_spec((1, d)), _full_spec(w_in.shape)],
        out_specs=[pl.BlockSpec((tm, nq), row)] + [pl.BlockSpec((tm, nkv), row)] * 4,
        out_shape=[jax.ShapeDtypeStruct((t, nq), BF16), jax.ShapeDtypeStruct((t, nkv), F32),
                   jax.ShapeDtypeStruct((t, nkv), F32), jax.ShapeDtypeStruct((t, nkv), BF16),
                   jax.ShapeDtypeStruct((t, nkv), BF16)],
        compiler_params=_cparams(("parallel",)),
        name="sb_proj",
    )(x, g_mix.reshape(1, d), w_in.astype(BF16))


def _suffix_matrix():
    r = jnp.arange(SB_BLOCK)
    tri = (r[:, None] > r[None, :]).astype(BF16)
    return jnp.concatenate([tri, jnp.ones((SB_BLOCK, SB_BLOCK), BF16)], axis=1)


def _sb_weights(z, later, tri_ones, strict):
    m = z.shape[0]
    log_1mb = -(jnp.maximum(z, 0.0) + jnp.log(1.0 + jnp.exp(-jnp.abs(z))))
    if strict is not None:
        log_1mb = jnp.where(strict, log_1mb, 0.0)
    hi = log_1mb.astype(BF16)
    lo = (log_1mb - hi.astype(F32)).astype(BF16)
    sums = _dot(jnp.concatenate([hi, lo], axis=0), tri_ones)
    sums = sums[:m] + sums[m:]
    suffix = sums[:, :SB_BLOCK]
    total = sums[:, SB_BLOCK:]
    a = jnp.exp(z + log_1mb + suffix + later)
    if strict is not None:
        a = jnp.where(strict, a, 0.0)
    return a.astype(BF16), total


def _sb_prompt_kernel(q_ref, k_ref, v_ref, tri_ref, o_ref):
    qi = pl.program_id(1)
    blk = SB_BLOCK
    n_rep = SB_HEADS // SB_KV_HEADS
    hd = SB_HEAD_DIM
    q = q_ref[...]
    qs = jnp.concatenate([q[:, h * hd:(h + 1) * hd] for h in range(SB_HEADS)], axis=0)
    tri_ones = tri_ref[...]
    rows = SB_HEADS * blk
    grp = n_rep * blk

    def tile(j, later, strict):
        start = pl.multiple_of(j * blk, blk)
        kt = k_ref[pl.ds(start, blk), :]
        vt = v_ref[pl.ds(start, blk), :]
        z = jnp.concatenate([_dot_nt(qs[kvh * grp:(kvh + 1) * grp], kt[:, kvh * hd:(kvh + 1) * hd])
                             for kvh in range(SB_KV_HEADS)], axis=0)
        a, total = _sb_weights(z, later, tri_ones, strict)
        d_out = jnp.concatenate([_dot(a[kvh * grp:(kvh + 1) * grp], vt[:, kvh * hd:(kvh + 1) * hd])
                                 for kvh in range(SB_KV_HEADS)], axis=0)
        return d_out, total

    t = lax.broadcasted_iota(I32, (rows, blk), 0) % blk
    c = lax.broadcasted_iota(I32, (rows, blk), 1)
    out, later = tile(qi, jnp.zeros((rows, blk), F32), c < t)

    def cond(carry):
        j, alive, _, _ = carry
        return (j >= 0) & (alive > SB_DEAD_LOG)

    def body(carry):
        j, _, later, out = carry
        d_out, total = tile(j, later, None)
        later = later + total
        return j - 1, jnp.max(later), later, out + d_out

    _, _, _, out = lax.while_loop(cond, body, (qi - 1, jnp.max(later), later, out))
    o_ref[...] = jnp.concatenate([out[h * blk:(h + 1) * blk] for h in range(SB_HEADS)], axis=1).astype(BF16)


def _sb_prompt(q, kb, vb, batch, seq):
    nq = seq // SB_BLOCK
    nqd, nkv = SB_HEADS * SB_HEAD_DIM, SB_KV_HEADS * SB_HEAD_DIM
    qmap = lambda b, i: (b * nq + i, 0)
    return pl.pallas_call(
        _sb_prompt_kernel,
        grid=(batch, nq),
        in_specs=[pl.BlockSpec((SB_BLOCK, nqd), qmap), pl.BlockSpec((seq, nkv), lambda b, i: (b, 0)),
                  pl.BlockSpec((seq, nkv), lambda b, i: (b, 0)), _full_spec((SB_BLOCK, 2 * SB_BLOCK))],
        out_specs=pl.BlockSpec((SB_BLOCK, nqd), qmap),
        out_shape=jax.ShapeDtypeStruct((batch * seq, nqd), BF16),
        compiler_params=_cparams(("parallel", "arbitrary")),
        name="sb_prompt",
    )(q, kb, vb, _suffix_matrix())


def _sb_paged_kernel(pt_ref, qbd_ref, tri_ref, k_hbm, v_hbm, o_ref, kbuf, vbuf, sem, *, layer, n_pages):
    b = pl.program_id(0)
    n_rep = SB_HEADS // SB_KV_HEADS
    hd = SB_HEAD_DIM
    base = (b % 2) * 2

    def copies(row, step, slot):
        page = pt_ref[row, n_pages - 1 - step]
        cps = []
        for kvh in range(SB_KV_HEADS):
            cps.append(pltpu.make_async_copy(k_hbm.at[layer, page, :, kvh, :], kbuf.at[slot, kvh], sem.at[0, slot]))
            cps.append(pltpu.make_async_copy(v_hbm.at[layer, page, :, kvh, :], vbuf.at[slot, kvh], sem.at[1, slot]))
        return cps

    def start(row, step, slot):
        for cp in copies(row, step, slot):
            cp.start()

    def wait(row, step, slot):
        for cp in copies(row, step, slot):
            cp.wait()

    @pl.when(b == 0)
    def _():
        start(0, 0, 0)

    @pl.when(b + 1 < pl.num_programs(0))
    def _():
        start(b + 1, 0, 2 - base)

    qbd = qbd_ref[0]
    tri_ones = tri_ref[...]

    def cond(carry):
        i, alive, _, _ = carry
        return (i < n_pages) & (alive > SB_DEAD_LOG)

    def body(carry):
        i, _, later, acc = carry
        slot = base + i % 2

        @pl.when(i + 1 < n_pages)
        def _():
            start(b, i + 1, base + (i + 1) % 2)

        wait(b, i, slot)
        z = _dot_nt(qbd[:, :hd], kbuf[slot, 0].astype(BF16))
        for kvh in range(1, SB_KV_HEADS):
            z = z + _dot_nt(qbd[:, kvh * hd:(kvh + 1) * hd], kbuf[slot, kvh].astype(BF16))
        a, total = _sb_weights(z, later, tri_ones, None)
        d_acc = jnp.concatenate([_dot(a, vbuf[slot, kvh].astype(BF16)) for kvh in range(SB_KV_HEADS)], axis=1)
        later = later + total
        return i + 1, jnp.max(later), later, acc + d_acc

    init = (jnp.int32(0), jnp.float32(0.0), jnp.zeros((SB_HEADS, SB_BLOCK), F32),
            jnp.zeros((SB_HEADS, SB_KV_HEADS * hd), F32))
    n_done, _, _, acc = lax.while_loop(cond, body, init)

    @pl.when(n_done < n_pages)
    def _():
        wait(b, n_done, base + n_done % 2)

    rid = lax.broadcasted_iota(I32, (SB_HEADS, hd), 0) // n_rep
    o = jnp.zeros((SB_HEADS, hd), F32)
    for kvh in range(SB_KV_HEADS):
        o = jnp.where(rid == kvh, acc[:, kvh * hd:(kvh + 1) * hd], o)
    o_ref[0] = o.astype(o_ref.dtype)


def _sb_paged(page_table, qbd, k_pool, v_pool, layer):
    bd, n_pages = page_table.shape
    nkv = SB_KV_HEADS * SB_HEAD_DIM
    page_shape = (4, SB_KV_HEADS, PAGE_SIZE, SB_HEAD_DIM)
    per_b = lambda b, pt: (b, 0, 0)
    return pl.pallas_call(
        functools.partial(_sb_paged_kernel, layer=layer, n_pages=n_pages),
        grid_spec=pltpu.PrefetchScalarGridSpec(
            num_scalar_prefetch=1, grid=(bd,),
            in_specs=[pl.BlockSpec((1, SB_HEADS, nkv), per_b), pl.BlockSpec((SB_BLOCK, 2 * SB_BLOCK), lambda b, pt: (0, 0)),
                      pl.BlockSpec(memory_space=pl.ANY), pl.BlockSpec(memory_space=pl.ANY)],
            out_specs=pl.BlockSpec((1, SB_HEADS, SB_HEAD_DIM), per_b),
            scratch_shapes=[pltpu.VMEM(page_shape, F32), pltpu.VMEM(page_shape, F32), pltpu.SemaphoreType.DMA((2, 4))]),
        out_shape=jax.ShapeDtypeStruct((bd, SB_HEADS, SB_HEAD_DIM), BF16),
        compiler_params=_cparams(("arbitrary",)),
        name="sb_paged",
    )(page_table, qbd, _suffix_matrix(), k_pool, v_pool)


def _lane_min_where(mask, lane, big):
    return jnp.min(jnp.where(mask, lane, big), axis=-1, keepdims=True)


def _oproj_route_kernel(x_ref, o_ref, wo_ref, gffn_ref, wrh_ref, wrl_ref, br_ref, base_ref, tri_ref,
                        xmid_ref, hn_ref, route_ref, gate_ref, cnt_out_ref, cnt_ref):
    @pl.when(pl.program_id(0) == 0)
    def _():
        cnt_ref[...] = base_ref[...]

    x_mid = x_ref[...] + _dot(o_ref[...], wo_ref[...])
    xmid_ref[...] = x_mid
    hn = _rms(x_mid, gffn_ref[...])
    _store_token_tiles(hn_ref, hn)
    hh = hn.astype(BF16)
    hl = (hn - hh.astype(F32)).astype(BF16)
    wrh = wrh_ref[...]
    logits = _dot(hh, wrh) + _dot(hl, wrh) + _dot(hh, wrl_ref[...]) + br_ref[...]
    lane_i = lax.broadcasted_iota(I32, logits.shape, 1)
    lane = lane_i.astype(F32)
    big = float(ROUTER_LANES)
    is_grp = lane < N_GROUPS
    lg = jnp.where(is_grp, logits, NEG_BIG)
    mg = lg.max(axis=-1, keepdims=True)
    grp = _lane_min_where(is_grp & (lg == mg), lane, big)
    p_grp = 1.0 / jnp.sum(jnp.where(is_grp, jnp.exp(lg - mg), 0.0), axis=-1, keepdims=True)
    lo = N_GROUPS + grp * EXPERTS_PER_GROUP
    in_grp = (lane >= lo) & (lane < lo + EXPERTS_PER_GROUP)
    lf = jnp.where(in_grp, logits, NEG_BIG)
    v1 = lf.max(axis=-1, keepdims=True)
    i1 = _lane_min_where(in_grp & (lf == v1), lane, big)
    rest = in_grp & (lane != i1)
    lf2 = jnp.where(rest, logits, NEG_BIG)
    v2 = lf2.max(axis=-1, keepdims=True)
    i2 = _lane_min_where(rest & (lf2 == v2), lane, big)
    e2 = jnp.exp(v2 - v1)
    g1 = p_grp / (1.0 + e2)
    g2 = p_grp * e2 / (1.0 + e2)
    gate_ref[...] = jnp.where(lane_i == 0, g1, jnp.where(lane_i == 1, g2, 0.0))
    ex1 = i1 - N_GROUPS
    ex2 = i2 - N_GROUPS
    oh1 = lane == ex1
    oh2 = lane == ex2
    oh = jnp.where(oh1, 1.0, 0.0) + jnp.where(oh2, 1.0, 0.0)
    before = _dot(tri_ref[...], oh.astype(BF16)) + cnt_ref[...]
    r1 = jnp.sum(jnp.where(oh1, before, 0.0), axis=-1, keepdims=True)
    r2 = jnp.sum(jnp.where(oh2, before, 0.0), axis=-1, keepdims=True)
    cnt = cnt_ref[...] + jnp.sum(oh, axis=0, keepdims=True)
    cnt_ref[...] = cnt
    cnt_out_ref[...] = cnt
    route_ref[...] = jnp.where(lane_i == 0, ex1, jnp.where(lane_i == 1, ex2, jnp.where(
        lane_i == 2, r1, jnp.where(lane_i == 3, r2, 0.0)))).astype(I32)


def _router_weights(w_group, b_group, w_fine, b_fine):
    d = w_group.shape[0]
    pad = ROUTER_LANES - N_GROUPS - N_EXPERTS
    w = jnp.concatenate([w_group, w_fine, jnp.zeros((d, pad), F32)], axis=1)
    b = jnp.concatenate([b_group, b_fine, jnp.zeros((pad,), F32)]).reshape(1, ROUTER_LANES)
    w_hi = w.astype(BF16)
    w_lo = (w - w_hi.astype(F32)).astype(BF16)
    return w_hi, w_lo, b


def _oproj_route(x, o, w_o, g_ffn, router, base_counts, tm):
    t, d = x.shape
    kd = o.shape[1]
    w_hi, w_lo, b = router
    row = lambda i: (i, 0)
    r = jnp.arange(tm)
    tri = (r[None, :] < r[:, None]).astype(BF16)
    lanes = ROUTER_LANES
    return pl.pallas_call(
        _oproj_route_kernel,
        grid=(t // tm,),
        in_specs=[pl.BlockSpec((tm, d), row), pl.BlockSpec((tm, kd), row), _full_spec((kd, d)), _full_spec((1, d)),
                  _full_spec(w_hi.shape), _full_spec(w_lo.shape), _full_spec(b.shape), _full_spec((1, lanes)),
                  _full_spec((tm, tm))],
        out_specs=[pl.BlockSpec((tm, d), row), pl.BlockSpec((tm * ROW_CHUNKS, d // ROW_CHUNKS), row),
                   pl.BlockSpec((tm, lanes), row), pl.BlockSpec((tm, lanes), row), _full_spec((1, lanes))],
        out_shape=[jax.ShapeDtypeStruct((t, d), F32), jax.ShapeDtypeStruct((t * ROW_CHUNKS, d // ROW_CHUNKS), F32),
                   jax.ShapeDtypeStruct((t, lanes), I32), jax.ShapeDtypeStruct((t, lanes), F32),
                   jax.ShapeDtypeStruct((1, lanes), F32)],
        scratch_shapes=[pltpu.VMEM((1, lanes), F32)],
        compiler_params=_cparams(("arbitrary",)),
        name="oproj_route",
    )(x, o, w_o, g_ffn.reshape(1, d), w_hi, w_lo, b, base_counts, tri)


def _moe_kernel(be_ref, tok_ref, dst_ref, nact_ref, hn_hbm, wg_ref, wu_ref, wd_ref, y_hbm,
                xbuf, obuf, wgb, wub, wdb, gsem, ssem, *, t_all, t_pad):
    i = pl.program_id(0)
    nb = pl.num_programs(0)
    blk = MOE_SLOT_BLOCK
    rc = ROW_CHUNKS
    n_act = nact_ref[0]
    slot = i % 2

    def gather_start(block, buf):
        for r in range(blk):
            src = pl.ds(pl.multiple_of(tok_ref[block * blk + r], rc), rc)
            pltpu.make_async_copy(hn_hbm.at[src], xbuf.at[buf, pl.ds(r * rc, rc)], gsem.at[buf]).start()

    def gather_wait(buf):
        pltpu.make_async_copy(hn_hbm.at[pl.ds(0, blk * rc)], xbuf.at[buf], gsem.at[buf]).wait()

    def scatter_start(block, buf):
        for r in range(blk):
            dst = pl.ds(pl.multiple_of(dst_ref[block * blk + r], rc), rc)
            pltpu.make_async_copy(obuf.at[buf, pl.ds(r * rc, rc)], y_hbm.at[dst], ssem.at[buf]).start()

    def scatter_wait(buf):
        pltpu.make_async_copy(obuf.at[buf], y_hbm.at[pl.ds(0, blk * rc)], ssem.at[buf]).wait()

    def compute(buf):
        xb = _load_token_tiles(xbuf, blk, lead=(buf,)).astype(BF16)
        gt = _dot(xb, wgb[...])
        up = _dot(xb, wub[...])
        hmid = (gt * (1.0 / (1.0 + jnp.exp(-gt))) * up).astype(BF16)
        _store_token_tiles(obuf, _dot(hmid, wdb[...]), lead=(buf,))

    @pl.when(i == 0)
    def _():
        obuf[1] = jnp.zeros(obuf.shape[1:], F32)
        gap = t_pad - t_all
        fills = [(2 * t_pad, blk), (2 * t_pad + blk, blk)] + ([(t_all, gap), (t_pad + t_all, gap)] if gap else [])
        cps = [pltpu.make_async_copy(obuf.at[1, pl.ds(0, n * rc)], y_hbm.at[pl.ds(r0 * rc, n * rc)], ssem.at[1])
               for r0, n in fills]
        for cp in cps:
            cp.start()
        for cp in cps:
            cp.wait()

    @pl.when((i == 0) & (n_act > 0))
    def _():
        gather_start(0, 0)

    @pl.when((i >= 2) & (i - 2 < n_act))
    def _():
        scatter_wait(slot)

    @pl.when(i < n_act)
    def _():
        @pl.when((i == 0) | (be_ref[i] != be_ref[jnp.maximum(i - 1, 0)]))
        def _():
            wgb[...] = wg_ref[0, 0].astype(BF16)
            wub[...] = wu_ref[0, 0].astype(BF16)
            wdb[...] = wd_ref[0, 0].astype(BF16)

        gather_wait(slot)

    steady = (i >= 1) & (i + 1 < n_act)

    @pl.when(steady)
    def _():
        gather_start(i + 1, 1 - slot)
        scatter_start(i - 1, 1 - slot)
        compute(slot)

    @pl.when((i < n_act) & jnp.logical_not(steady))
    def _():
        @pl.when(i + 1 < n_act)
        def _():
            gather_start(i + 1, 1 - slot)

        @pl.when(i >= 1)
        def _():
            scatter_start(i - 1, 1 - slot)

        compute(slot)

    @pl.when((i == n_act) & (i >= 1))
    def _():
        scatter_start(i - 1, 1 - slot)

    @pl.when((i == nb - 1) & (i - 1 < n_act))
    def _():
        scatter_wait(1 - slot)


def _moe_experts(block_e, slot_tok, slot_dst, n_act, hn, w_gate, w_up, w_down, layer, t_pad):
    n_slots = slot_tok.shape[0]
    blk = MOE_SLOT_BLOCK
    n_blocks = n_slots // blk
    rc, lanes = ROW_CHUNKS, hn.shape[1]
    t_all = hn.shape[0] // rc
    d = rc * lanes
    de = w_gate.shape[3]
    assert 0 <= t_pad - t_all <= blk
    wmap = lambda i, be, tok, dst, na: (layer, be[i], 0, 0)
    return pl.pallas_call(
        functools.partial(_moe_kernel, t_all=t_all, t_pad=t_pad),
        grid_spec=pltpu.PrefetchScalarGridSpec(
            num_scalar_prefetch=4, grid=(n_blocks,),
            in_specs=[pl.BlockSpec(memory_space=pl.ANY), pl.BlockSpec((1, 1, d, de), wmap),
                      pl.BlockSpec((1, 1, d, de), wmap), pl.BlockSpec((1, 1, de, d), wmap)],
            out_specs=pl.BlockSpec(memory_space=pl.ANY),
            scratch_shapes=[pltpu.VMEM((2, blk * rc, lanes), F32), pltpu.VMEM((2, blk * rc, lanes), F32),
                            pltpu.VMEM((d, de), BF16), pltpu.VMEM((d, de), BF16), pltpu.VMEM((de, d), BF16),
                            pltpu.SemaphoreType.DMA((2,)), pltpu.SemaphoreType.DMA((2,))]),
        out_shape=jax.ShapeDtypeStruct(((2 * t_pad + 2 * blk) * rc, lanes), F32),
        compiler_params=_cparams(("arbitrary",)),
        name="moe_experts",
    )(block_e, slot_tok, slot_dst, n_act, hn, w_gate, w_up, w_down)


def _moe_slots(route, counts, t_pad):
    n_tok = route.shape[0]
    blk = MOE_SLOT_BLOCK
    n_asg = 2 * n_tok
    experts = jnp.arange(N_EXPERTS, dtype=I32)
    padded = (counts + blk - 1) // blk * blk
    pad_end = jnp.cumsum(padded)
    pad_start = pad_end - padded
    e = route[:, :2]
    rank = route[:, 2:4]
    dest = jnp.sum(jnp.where(e[..., None] == experts, pad_start, 0), axis=-1) + rank
    n_blocks = -(-n_asg // blk) + N_EXPERTS + 1
    slot_asg = jnp.full((n_blocks * blk,), -1, I32).at[dest.reshape(n_asg)].set(jnp.arange(n_asg, dtype=I32))
    slot = jnp.arange(n_blocks * blk, dtype=I32)
    slot_tok = (jnp.maximum(slot_asg, 0) >> 1) * ROW_CHUNKS
    spare = 2 * t_pad + ((slot // blk) % 2) * blk + slot % blk
    slot_dst = jnp.where(slot_asg >= 0, (slot_asg & 1) * t_pad + (slot_asg >> 1), spare) * ROW_CHUNKS
    n_act = (pad_end[-1] // blk).astype(I32)
    blk_start = jnp.arange(n_blocks, dtype=I32) * blk
    block_e = jnp.minimum(jnp.sum((pad_end[None, :] <= blk_start[:, None]).astype(I32), axis=1), N_EXPERTS - 1)
    last_e = jnp.sum(jnp.where(jnp.arange(n_blocks) == jnp.maximum(n_act - 1, 0), block_e, 0))
    block_e = jnp.where(jnp.arange(n_blocks) < n_act, block_e, last_e).astype(I32)
    return block_e, slot_tok, slot_dst, n_act.reshape(1)


def _combine_ple_kernel(x_ref, ya_ref, yb_ref, mg_ref, p_ref, g_ref, wg_ref, wp_ref, gfin_ref, o_ref, *, final):
    mg = mg_ref[...]
    n = x_ref.shape[0]
    x1 = x_ref[...] + (mg[:, 0:1] * _load_token_tiles(ya_ref, n) + mg[:, 1:2] * _load_token_tiles(yb_ref, n))
    gate = _dot(_rms(x1, g_ref[...]).astype(BF16), wg_ref[...])
    gate = 1.0 / (1.0 + jnp.exp(-gate))
    out = x1 + gate * _dot(p_ref[...].astype(BF16), wp_ref[...])
    if final:
        out = _rms(out, gfin_ref[...])
    o_ref[...] = out


def _combine_ple(x_mid, y2, row0, t_pad, moe_gate, p, g_ple, w_gate, w_proj, g_final, final, tm):
    t, d = x_mid.shape
    pd = p.shape[1]
    assert row0 % tm == 0 and t_pad % tm == 0
    row = lambda i: (i, 0)
    tile = (tm * ROW_CHUNKS, d // ROW_CHUNKS)
    return pl.pallas_call(
        functools.partial(_combine_ple_kernel, final=final),
        grid=(t // tm,),
        in_specs=[pl.BlockSpec((tm, d), row), pl.BlockSpec(tile, lambda i: (row0 // tm + i, 0)),
                  pl.BlockSpec(tile, lambda i: ((t_pad + row0) // tm + i, 0)), pl.BlockSpec((tm, ROUTER_LANES), row),
                  pl.BlockSpec((tm, pd), row), _full_spec((1, d)), _full_spec((d, d)), _full_spec((pd, d)),
                  _full_spec((1, d))],
        out_specs=pl.BlockSpec((tm, d), row),
        out_shape=jax.ShapeDtypeStruct((t, d), F32),
        compiler_params=_cparams(("parallel",)),
        name="combine_ple",
    )(x_mid, y2, y2, moe_gate, p, g_ple.reshape(1, d), w_gate, w_proj, g_final.reshape(1, d))


def kernel(x_prompt, x_sample, cache_mla_ckv, cache_mla_kpe, cache_swa_k, cache_swa_v, cache_sb_k, cache_sb_v, page_table, p_prompt, p_sample, g_mix, g_ffn, g_ple, g_final, mla_w_in, mla_g_q, mla_g_kv, mla_w_uq, mla_w_uk, mla_w_uv, mla_w_o, swa_w_in, swa_sinks, swa_w_o, sb_w_in, sb_w_o, moe_w_group, moe_b_group, moe_w_fine, moe_b_fine, moe_w_gate, moe_w_up, moe_w_down, ple_w_gate, ple_w_proj):
    batch, seq, d = x_prompt.shape
    bd, dec_seq, _ = x_sample.shape
    assert dec_seq == 1
    depth = g_mix.shape[0]
    n_pages = page_table.shape[1]
    past_len = n_pages * PAGE_SIZE
    tp = batch * seq
    tm_p = _row_tile(tp, 256)
    tm_s = bd
    assert tp % bd == 0 and seq % tm_p == 0
    t_pad = -(-(tp + bd) // tm_p) * tm_p
    page_table = page_table.astype(I32)
    kpe_pool_t = cache_mla_kpe.transpose(0, 1, 3, 2)

    xp = x_prompt.reshape(tp, d)
    xs = x_sample.reshape(bd, d)
    pos_p = jnp.arange(seq)
    pos_s = jnp.full((bd,), past_len)
    rope_p = _rope_tables(pos_p, 32, 128)
    rope_s = _rope_tables(pos_s, 32, 128)

    outs = {k: [] for k in ("p_ckv", "p_kpe", "s_ckv", "s_kpe", "p_wk", "p_wv", "s_wk", "s_wv", "p_sk", "p_sv", "s_sk", "s_sv")}
    for i in range(depth):
        kind, j = i % 3, i // 3
        if kind == 0:
            weights = _mla_weights(mla_w_in[j], mla_w_uq[j], mla_w_uk[j], mla_w_uv[j])
            q_p, kk_p, v_p, ckv_p, kpe_p = _mla_proj(xp, g_mix[i], mla_g_q[j], mla_g_kv[j], weights, *rope_p, seq // tm_p, tm_p)
            q_s, _, _, ckv_s, kpe_s = _mla_proj(xs, g_mix[i], mla_g_q[j], mla_g_kv[j], weights, *rope_s, 1, tm_s)
            o_p = _mla_flash(q_p, kk_p, v_p, batch, seq, _row_tile(seq, FLASH_Q_TILE), _row_tile(seq, FLASH_K_TILE))
            q3 = q_s.reshape(bd, MLA_HEADS, MLA_QK_PAD)
            w_uk_t = mla_w_uk[j].transpose(1, 2, 0).astype(BF16)
            qlat = _bmm(q3[:, :, 128:].transpose(1, 0, 2), w_uk_t, BF16).transpose(1, 0, 2)
            qx = jnp.concatenate([qlat, q3[:, :, :128]], axis=-1)
            o_lat = _mla_paged(page_table, qx, ckv_s.reshape(bd, 1, -1), kpe_s.reshape(bd, 1, -1),
                               cache_mla_ckv, kpe_pool_t, j)
            w_uv_h = mla_w_uv[j].transpose(1, 0, 2).astype(BF16)
            o_s = _bmm(o_lat.transpose(1, 0, 2), w_uv_h, BF16).transpose(1, 0, 2).reshape(bd, MLA_HEADS * MLA_V)
            w_o = mla_w_o[j]
            outs["p_ckv"].append(ckv_p.reshape(batch, seq, -1)); outs["p_kpe"].append(kpe_p.reshape(batch, seq, -1))
            outs["s_ckv"].append(ckv_s.reshape(bd, 1, -1)); outs["s_kpe"].append(kpe_s.reshape(bd, 1, -1))
        elif kind == 1:
            sinks2 = swa_sinks[j] * LOG2E
            q_p, k_p, v_p = _swa_proj(xp, g_mix[i], swa_w_in[j], *rope_p, seq // tm_p, tm_p)
            q_s, k_s, v_s = _swa_proj(xs, g_mix[i], swa_w_in[j], *rope_s, 1, tm_s)
            o_p = _swa_prompt(sinks2, q_p, k_p, v_p, batch, seq)
            nkv = SWA_KV_HEADS * SWA_HEAD_DIM
            kwin = cache_swa_k[j].reshape(bd, WINDOW, nkv)
            vwin = cache_swa_v[j].reshape(bd, WINDOW, nkv)
            qbd = _block_diag_q(q_s, SWA_HEADS, SWA_KV_HEADS, SWA_HEAD_DIM)
            o_s = _swa_sample(sinks2, qbd, kwin, vwin, k_s.reshape(bd, 1, nkv), v_s.reshape(bd, 1, nkv))
            o_s = o_s.reshape(bd, SWA_HEADS * SWA_HEAD_DIM)
            w_o = swa_w_o[j]
            k4 = k_p.reshape(batch, seq, SWA_KV_HEADS, SWA_HEAD_DIM)
            v4 = v_p.reshape(batch, seq, SWA_KV_HEADS, SWA_HEAD_DIM)
            outs["p_wk"].append(k4[:, seq - WINDOW:]); outs["p_wv"].append(v4[:, seq - WINDOW:])
            outs["s_wk"].append(jnp.concatenate([cache_swa_k[j][:, 1:], k_s.reshape(bd, 1, SWA_KV_HEADS, SWA_HEAD_DIM)], axis=1))
            outs["s_wv"].append(jnp.concatenate([cache_swa_v[j][:, 1:], v_s.reshape(bd, 1, SWA_KV_HEADS, SWA_HEAD_DIM)], axis=1))
        else:
            q_p, k_p, v_p, kb_p, vb_p = _sb_proj(xp, g_mix[i], sb_w_in[j], tm_p)
            q_s, k_s, v_s, _, _ = _sb_proj(xs, g_mix[i], sb_w_in[j], tm_s)
            o_p = _sb_prompt(q_p, kb_p, vb_p, batch, seq)
            qbd = _block_diag_q(q_s, SB_HEADS, SB_KV_HEADS, SB_HEAD_DIM)
            o_s = _sb_paged(page_table, qbd, cache_sb_k, cache_sb_v, j).reshape(bd, SB_HEADS * SB_HEAD_DIM)
            w_o = sb_w_o[j]
            outs["p_sk"].append(k_p.reshape(batch, seq, SB_KV_HEADS, SB_HEAD_DIM))
            outs["p_sv"].append(v_p.reshape(batch, seq, SB_KV_HEADS, SB_HEAD_DIM))
            outs["s_sk"].append(k_s.reshape(bd, 1, SB_KV_HEADS, SB_HEAD_DIM))
            outs["s_sv"].append(v_s.reshape(bd, 1, SB_KV_HEADS, SB_HEAD_DIM))

        w_o = w_o.astype(BF16)
        router = _router_weights(moe_w_group[i], moe_b_group[i], moe_w_fine[i], moe_b_fine[i])
        zero_counts = jnp.zeros((1, ROUTER_LANES), F32)
        xmid_p, hn_p, route_p, gate_p, cnt_p = _oproj_route(xp, o_p, w_o, g_ffn[i], router, zero_counts, tm_p)
        xmid_s, hn_s, route_s, gate_s, cnt = _oproj_route(xs, o_s, w_o, g_ffn[i], router, cnt_p, tm_s)
        hn = jnp.concatenate([hn_p, hn_s], axis=0)
        route = jnp.concatenate([route_p[:, :4], route_s[:, :4]], axis=0)
        block_e, slot_tok, slot_dst, n_act = _moe_slots(route, cnt[0, :N_EXPERTS].astype(I32), t_pad)
        y2 = _moe_experts(block_e, slot_tok, slot_dst, n_act, hn, moe_w_gate, moe_w_up, moe_w_down, i, t_pad)
        wg = ple_w_gate[i].astype(BF16)
        wp = ple_w_proj[i].astype(BF16)
        final = i == depth - 1
        xp = _combine_ple(xmid_p, y2, 0, t_pad, gate_p, p_prompt[i].reshape(tp, -1), g_ple[i], wg, wp, g_final, final, tm_p)
        xs = _combine_ple(xmid_s, y2, tp, t_pad, gate_s, p_sample[i].reshape(bd, -1), g_ple[i], wg, wp, g_final, final, tm_s)

    st = lambda k: jnp.stack(outs[k])
    return (xp.reshape(batch, seq, d), xs.reshape(bd, 1, d),
            st("p_ckv"), st("p_kpe"), st("p_wk"), st("p_wv"), st("p_sk"), st("p_sv"),
            st("s_ckv"), st("s_kpe"), st("s_wk"), st("s_wv"), st("s_sk"), st("s_sv"))
```

```python
import functools
import math

import jax
import jax.numpy as jnp
from jax import lax
from jax.experimental import pallas as pl
from jax.experimental.pallas import tpu as pltpu

F32 = jnp.float32
BF16 = jnp.bfloat16
I32 = jnp.int32

NORM_EPS = 1e-6
ROPE_THETA = 10000.0
PAGE_SIZE = 128

MLA_HEADS = 8
MLA_NOPE = 128
MLA_ROPE = 64
MLA_V = 128
MLA_Q_RANK = 512
MLA_KV_RANK = 256
MLA_QK_PAD = 256

SWA_HEADS = 16
SWA_KV_HEADS = 4
SWA_HEAD_DIM = 64
WINDOW = 128

SB_HEADS = 8
SB_KV_HEADS = 4
SB_HEAD_DIM = 128
SB_BLOCK = 128
SB_DEAD_LOG = -100.0

N_GROUPS = 4
EXPERTS_PER_GROUP = 8
N_EXPERTS = N_GROUPS * EXPERTS_PER_GROUP
D_EXPERT = 512
MOE_SLOT_BLOCK = 256
ROUTER_LANES = 128

LOG2E = 1.4426950408889634
NEG_BIG = -1e30
VMEM_LIMIT = 56 * 1024 * 1024
PAGES_PER_STEP = 16
FLASH_Q_TILE = 256
FLASH_K_TILE = 1024
FLASH_HEADS_PER_STEP = 2


def _cparams(sem, vmem=VMEM_LIMIT):
    return pltpu.CompilerParams(dimension_semantics=sem, vmem_limit_bytes=vmem)


def _rms(x, g):
    return x * lax.rsqrt(jnp.mean(x * x, axis=-1, keepdims=True) + NORM_EPS) * g


def _dot(a, b):
    return jnp.dot(a, b, preferred_element_type=F32)


def _dot_nt(a, b):
    return lax.dot_general(a, b, (((1,), (1,)), ((), ())), preferred_element_type=F32)


def _full_spec(shape):
    nd = len(shape)
    return pl.BlockSpec(shape, lambda *_: (0,) * nd)


def _row_tile(n, pref):
    return pref if n % pref == 0 else n


ROW_CHUNKS = 8


def _store_token_tiles(ref, value, lead=()):
    n = value.shape[0]
    for j in range(ROW_CHUNKS):
        ref[lead + (pl.ds(j, n, stride=ROW_CHUNKS), slice(None))] = value[:, j * 128:(j + 1) * 128]


def _load_token_tiles(ref, n, lead=()):
    return jnp.concatenate([ref[lead + (pl.ds(j, n, stride=ROW_CHUNKS), slice(None))] for j in range(ROW_CHUNKS)], axis=1)


def _rope_tables(pos, half, lanes):
    inv = jnp.exp(-math.log(ROPE_THETA) * jnp.arange(half, dtype=F32) / half)
    ang = pos.astype(F32)[:, None] * inv[None, :]
    cos, sin = jnp.cos(ang), jnp.sin(ang)
    reps = lanes // (2 * half)
    if reps >= 1:
        return jnp.tile(jnp.concatenate([cos, cos], 1), (1, reps)), jnp.tile(jnp.concatenate([sin, sin], 1), (1, reps))
    pad = lanes - 2 * half
    n = pos.shape[0]
    return (jnp.concatenate([cos, cos, jnp.ones((n, pad), F32)], 1),
            jnp.concatenate([sin, sin, jnp.zeros((n, pad), F32)], 1))


def _rot_cols(w, head_dim):
    k, n = w.shape
    w3 = w.reshape(k, n // head_dim, head_dim)
    half = head_dim // 2
    return jnp.concatenate([-w3[..., half:], w3[..., :half]], axis=-1).reshape(k, n)


def _mla_proj_kernel(x_ref, gmix_ref, win_ref, gq_ref, gkv_ref, cos_ref, sin_ref, wq_ref, wqr_ref, wuk_ref, wuv_ref,
                     q_ref, kk_ref, v_ref, ckv_ref, kpe_ref, *, qscale):
    h = _rms(x_ref[...], gmix_ref[...]).astype(BF16)
    proj = _dot(h, win_ref[...])
    c_q = _rms(proj[:, :MLA_Q_RANK], gq_ref[...]).astype(BF16)
    c_kv = _rms(proj[:, MLA_Q_RANK:MLA_Q_RANK + MLA_KV_RANK], gkv_ref[...])
    ckv_ref[...] = c_kv
    cos = cos_ref[...]
    sin = sin_ref[...]
    o = MLA_Q_RANK + MLA_KV_RANK
    kpe = proj[:, o:o + 128] * cos + proj[:, o + 128:o + 256] * sin
    kpe_ref[...] = kpe[:, :MLA_ROPE]
    kpe_b = kpe.astype(BF16)
    ckv_b = c_kv.astype(BF16)
    kn = _dot(ckv_b, wuk_ref[...])
    vv = _dot(ckv_b, wuv_ref[...]).astype(BF16)
    ones = jnp.ones((vv.shape[0], MLA_V), BF16)
    for hd in range(MLA_HEADS):
        v_ref[:, 2 * hd * MLA_V:(2 * hd + 1) * MLA_V] = vv[:, hd * MLA_V:(hd + 1) * MLA_V]
        v_ref[:, (2 * hd + 1) * MLA_V:(2 * hd + 2) * MLA_V] = ones
    qe = _dot(c_q, wq_ref[...])
    qr = _dot(c_q, wqr_ref[...])
    for hd in range(MLA_HEADS):
        a = hd * MLA_QK_PAD
        pe = qe[:, a:a + 128] * cos + qr[:, hd * 128:(hd + 1) * 128] * sin
        q_ref[:, a:a + 128] = (pe * qscale).astype(BF16)
        q_ref[:, a + 128:a + 256] = (qe[:, a + 128:a + 256] * qscale).astype(BF16)
        kk_ref[:, a:a + 128] = kpe_b
        kk_ref[:, a + 128:a + 256] = kn[:, hd * 128:(hd + 1) * 128].astype(BF16)


def _mla_weights(w_in, w_uq, w_uk, w_uv):
    d = w_in.shape[0]
    o = MLA_Q_RANK + MLA_KV_RANK
    w_kpe = w_in[:, o:]
    z64 = jnp.zeros((d, 128 - MLA_ROPE), F32)
    win = jnp.concatenate([w_in[:, :o], w_kpe, z64, _rot_cols(w_kpe, MLA_ROPE), z64], axis=1).astype(BF16)
    r = w_uq.shape[0]
    w_nope = w_uq[:, :, :MLA_NOPE]
    w_pe = w_uq[:, :, MLA_NOPE:]
    zq = jnp.zeros((r, MLA_HEADS, 128 - MLA_ROPE), F32)
    wq = jnp.concatenate([w_pe, zq, w_nope], axis=-1).reshape(r, MLA_HEADS * MLA_QK_PAD).astype(BF16)
    w_pe_rot = _rot_cols(w_pe.reshape(r, MLA_HEADS * MLA_ROPE), MLA_ROPE).reshape(r, MLA_HEADS, MLA_ROPE)
    wqr = jnp.concatenate([w_pe_rot, zq], axis=-1).reshape(r, MLA_HEADS * 128).astype(BF16)
    wuk = w_uk.reshape(MLA_KV_RANK, MLA_HEADS * MLA_NOPE).astype(BF16)
    wuv = w_uv.reshape(MLA_KV_RANK, MLA_HEADS * MLA_V).astype(BF16)
    return win, wq, wqr, wuk, wuv


def _mla_proj(x, g_mix, g_q, g_kv, weights, cos, sin, n_pos_blocks, tm):
    win, wq, wqr, wuk, wuv = weights
    t, d = x.shape
    qscale = (MLA_NOPE + MLA_ROPE) ** -0.5 * LOG2E
    row = lambda i: (i, 0)
    pos = lambda i: (i % n_pos_blocks, 0)
    hq = MLA_HEADS * MLA_QK_PAD
    return pl.pallas_call(
        functools.partial(_mla_proj_kernel, qscale=qscale),
        grid=(t // tm,),
        in_specs=[pl.BlockSpec((tm, d), row), _full_spec((1, d)), _full_spec(win.shape), _full_spec((1, MLA_Q_RANK)),
                  _full_spec((1, MLA_KV_RANK)), pl.BlockSpec((tm, 128), pos), pl.BlockSpec((tm, 128), pos),
                  _full_spec(wq.shape), _full_spec(wqr.shape), _full_spec(wuk.shape), _full_spec(wuv.shape)],
        out_specs=[pl.BlockSpec((tm, hq), row), pl.BlockSpec((tm, hq), row), pl.BlockSpec((tm, 2 * MLA_HEADS * MLA_V), row),
                   pl.BlockSpec((tm, MLA_KV_RANK), row), pl.BlockSpec((tm, MLA_ROPE), row)],
        out_shape=[jax.ShapeDtypeStruct((t, hq), BF16), jax.ShapeDtypeStruct((t, hq), BF16),
                   jax.ShapeDtypeStruct((t, 2 * MLA_HEADS * MLA_V), BF16), jax.ShapeDtypeStruct((t, MLA_KV_RANK), F32),
                   jax.ShapeDtypeStruct((t, MLA_ROPE), F32)],
        compiler_params=_cparams(("parallel",)),
        name="mla_proj",
    )(x, g_mix.reshape(1, d), win, g_q.reshape(1, -1), g_kv.reshape(1, -1), cos, sin, wq, wqr, wuk, wuv)


def _mla_flash_kernel(q_ref, k_ref, v_ref, o_ref, *, tq, tk, nh):
    qi = pl.program_id(2)
    qk, dv = MLA_QK_PAD, MLA_V
    qs = [q_ref[:, h * qk:(h + 1) * qk] for h in range(nh)]

    def scores(h, j):
        return _dot_nt(qs[h], k_ref[pl.ds(pl.multiple_of(j * tk, tk), tk), h * qk:(h + 1) * qk])

    def update(h, j, s, m, acc):
        m_new = jnp.maximum(m, s.max(axis=-1, keepdims=True))
        p = jnp.exp2(s - m_new).astype(BF16)
        vj = v_ref[pl.ds(pl.multiple_of(j * tk, tk), tk), 2 * h * dv:2 * (h + 1) * dv]
        return m_new, acc * jnp.exp2(m - m_new) + _dot(p, vj)

    last = (qi * tq) // tk

    def body(j, carry):
        nxt = [scores(h, j + 1) for h in range(nh)]
        return tuple(update(h, j, carry[h][2], *carry[h][:2]) + (nxt[h],) for h in range(nh))

    init = tuple((jnp.full((tq, 1), NEG_BIG, F32), jnp.zeros((tq, 2 * dv), F32), scores(h, 0)) for h in range(nh))
    carry = lax.fori_loop(0, last, body, init)
    row = qi * tq + lax.broadcasted_iota(I32, (tq, tk), 0)
    col = last * tk + lax.broadcasted_iota(I32, (tq, tk), 1)
    for h in range(nh):
        m, acc, s = carry[h]
        _, acc = update(h, last, jnp.where(col <= row, s, NEG_BIG), m, acc)
        o_ref[:, h * dv:(h + 1) * dv] = (acc[:, :dv] / acc[:, dv:]).astype(BF16)


def _mla_flash(q, kk, v, batch, seq, tq, tk):
    nq = seq // tq
    nh = FLASH_HEADS_PER_STEP
    assert tk % tq == 0 and seq % tq == 0 and MLA_HEADS % nh == 0
    return pl.pallas_call(
        functools.partial(_mla_flash_kernel, tq=tq, tk=tk, nh=nh),
        grid=(batch, MLA_HEADS // nh, nq),
        in_specs=[pl.BlockSpec((tq, nh * MLA_QK_PAD), lambda b, h, i: (b * nq + i, h)),
                  pl.BlockSpec((seq, nh * MLA_QK_PAD), lambda b, h, i: (b, h)),
                  pl.BlockSpec((seq, 2 * nh * MLA_V), lambda b, h, i: (b, h))],
        out_specs=pl.BlockSpec((tq, nh * MLA_V), lambda b, h, i: (b * nq + i, h)),
        out_shape=jax.ShapeDtypeStruct((batch * seq, MLA_HEADS * MLA_V), BF16),
        compiler_params=_cparams(("parallel", "parallel", "arbitrary")),
        name="mla_flash",
    )(q, kk, v)


def _bmm_kernel(x_ref, w_ref, o_ref):
    o_ref[0] = _dot(x_ref[0], w_ref[0]).astype(o_ref.dtype)


def _bmm(x, w, out_dtype):
    h, m, k = x.shape
    n = w.shape[2]
    return pl.pallas_call(
        _bmm_kernel,
        grid=(h,),
        in_specs=[pl.BlockSpec((1, m, k), lambda i: (i, 0, 0)), pl.BlockSpec((1, k, n), lambda i: (i, 0, 0))],
        out_specs=pl.BlockSpec((1, m, n), lambda i: (i, 0, 0)),
        out_shape=jax.ShapeDtypeStruct((h, m, n), out_dtype),
        compiler_params=_cparams(("parallel",)),
        name="head_matmul",
    )(x, w)


def _mla_paged_kernel(pt_ref, qx_ref, cnew_ref, knew_ref, ckv_hbm, kpe_hbm, o_ref, cbuf, kbuf, stage, kstage, sem,
                      *, layer, n_pages, g_pages):
    b = pl.program_id(0)
    n_groups = n_pages // g_pages
    r = MLA_KV_RANK
    qx = qx_ref[0]
    qlat = qx[:, :r]
    qpe = qx[:, r:r + MLA_ROPE]

    def copies(row, grp, slot):
        cps = []
        for j in range(g_pages):
            page = pt_ref[row, grp * g_pages + j]
            cols = pl.ds(j * PAGE_SIZE, PAGE_SIZE)
            cps.append(pltpu.make_async_copy(ckv_hbm.at[layer, page], cbuf.at[slot, cols], sem.at[0, slot]))
            cps.append(pltpu.make_async_copy(kpe_hbm.at[layer, page], kbuf.at[slot, :, cols], sem.at[1, slot]))
        return cps

    def start(row, grp, slot):
        for cp in copies(row, grp, slot):
            cp.start()

    def wait(row, grp, slot):
        for cp in copies(row, grp, slot):
            cp.wait()

    n_fetch, ahead = cbuf.shape[0], cbuf.shape[0] - 1

    def fetch_slot(row, grp):
        return (row * n_groups + grp) % n_fetch

    @pl.when(b == 0)
    def _():
        for g0 in range(ahead):
            start(0, g0, fetch_slot(0, g0))

    cn = cnew_ref[0]
    qf = qx.astype(F32)
    m = (jnp.sum(qf[:, :r] * cn.astype(BF16).astype(F32), axis=-1, keepdims=True)
         + jnp.sum(qf[:, r:r + MLA_ROPE] * knew_ref[0].astype(BF16).astype(F32), axis=-1, keepdims=True))
    l = jnp.ones_like(m)
    acc = jnp.broadcast_to(cn, (qx.shape[0], r))
    pending = None
    for grp in range(n_groups):
        slot = grp % 2
        nxt = grp + ahead
        if nxt < n_groups:
            start(b, nxt, fetch_slot(b, nxt))
        else:
            @pl.when(b + 1 < pl.num_programs(0))
            def _():
                start(b + 1, nxt - n_groups, fetch_slot(b + 1, nxt - n_groups))
        fs = fetch_slot(b, grp)
        wait(b, grp, fs)
        stage[slot] = cbuf[fs].astype(BF16)
        kstage[slot] = kbuf[fs].astype(BF16)
        s = _dot_nt(qlat, stage[slot]) + _dot(qpe, kstage[slot])
        if pending is not None:
            acc = acc * pending[1] + _dot(pending[0], stage[1 - slot])
        m_new = jnp.maximum(m, s.max(axis=-1, keepdims=True))
        corr = jnp.exp2(m - m_new)
        p = jnp.exp2(s - m_new)
        l = l * corr + p.sum(axis=-1, keepdims=True)
        m = m_new
        pending = (p.astype(BF16), corr)
    acc = acc * pending[1] + _dot(pending[0], stage[(n_groups - 1) % 2])
    o_ref[0] = (acc / l).astype(o_ref.dtype)


def _mla_paged(page_table, qx, c_new, k_new, ckv_pool, kpe_pool_t, layer):
    bd, n_pages = page_table.shape
    assert n_pages % 2 == 0
    g_pages = math.gcd(PAGES_PER_STEP, n_pages // 2)
    n_fetch = 3
    hds = qx.shape[1]
    kw = MLA_KV_RANK + 128
    rows = g_pages * PAGE_SIZE
    per_b = lambda b, pt: (b, 0, 0)
    return pl.pallas_call(
        functools.partial(_mla_paged_kernel, layer=layer, n_pages=n_pages, g_pages=g_pages),
        grid_spec=pltpu.PrefetchScalarGridSpec(
            num_scalar_prefetch=1, grid=(bd,),
            in_specs=[pl.BlockSpec((1, hds, kw), per_b), pl.BlockSpec((1, 1, MLA_KV_RANK), per_b),
                      pl.BlockSpec((1, 1, MLA_ROPE), per_b), pl.BlockSpec(memory_space=pl.ANY),
                      pl.BlockSpec(memory_space=pl.ANY)],
            out_specs=pl.BlockSpec((1, hds, MLA_KV_RANK), per_b),
            scratch_shapes=[pltpu.VMEM((n_fetch, rows, MLA_KV_RANK), F32), pltpu.VMEM((n_fetch, MLA_ROPE, rows), F32),
                            pltpu.VMEM((2, rows, MLA_KV_RANK), BF16), pltpu.VMEM((2, MLA_ROPE, rows), BF16),
                            pltpu.SemaphoreType.DMA((2, n_fetch))]),
        out_shape=jax.ShapeDtypeStruct((bd, hds, MLA_KV_RANK), BF16),
        compiler_params=_cparams(("arbitrary",)),
        name="mla_paged",
    )(page_table, qx, c_new, k_new, ckv_pool, kpe_pool_t)


def _swa_proj_kernel(x_ref, gmix_ref, win_ref, cos_ref, sin_ref, q_ref, k_ref, v_ref, *, qscale):
    nq, nkv = SWA_HEADS * SWA_HEAD_DIM, SWA_KV_HEADS * SWA_HEAD_DIM
    h = _rms(x_ref[...], gmix_ref[...]).astype(BF16)
    proj = _dot(h, win_ref[...])
    cos = cos_ref[...]
    sin = sin_ref[...]
    r0 = nq + 2 * nkv
    for c in range(nq // 128):
        a = c * 128
        q_ref[:, a:a + 128] = ((proj[:, a:a + 128] * cos + proj[:, r0 + a:r0 + a + 128] * sin) * qscale).astype(BF16)
    for c in range(nkv // 128):
        a = c * 128
        k_ref[:, a:a + 128] = proj[:, nq + a:nq + a + 128] * cos + proj[:, r0 + nq + a:r0 + nq + a + 128] * sin
    v_ref[...] = proj[:, nq + nkv:nq + 2 * nkv]


def _swa_proj(x, g_mix, w_in, cos, sin, n_pos_blocks, tm):
    t, d = x.shape
    nq, nkv = SWA_HEADS * SWA_HEAD_DIM, SWA_KV_HEADS * SWA_HEAD_DIM
    win = jnp.concatenate([w_in, _rot_cols(w_in[:, :nq + nkv], SWA_HEAD_DIM)], axis=1).astype(BF16)
    row = lambda i: (i, 0)
    pos = lambda i: (i % n_pos_blocks, 0)
    return pl.pallas_call(
        functools.partial(_swa_proj_kernel, qscale=SWA_HEAD_DIM ** -0.5 * LOG2E),
        grid=(t // tm,),
        in_specs=[pl.BlockSpec((tm, d), row), _full_spec((1, d)), _full_spec(win.shape),
                  pl.BlockSpec((tm, 128), pos), pl.BlockSpec((tm, 128), pos)],
        out_specs=[pl.BlockSpec((tm, nq), row), pl.BlockSpec((tm, nkv), row), pl.BlockSpec((tm, nkv), row)],
        out_shape=[jax.ShapeDtypeStruct((t, nq), BF16), jax.ShapeDtypeStruct((t, nkv), F32),
                   jax.ShapeDtypeStruct((t, nkv), F32)],
        compiler_params=_cparams(("parallel",)),
        name="swa_proj",
    )(x, g_mix.reshape(1, d), win, cos, sin)


def _sink_column(sink_ref, rows_per_head, kvh, n_rep):
    rid = lax.broadcasted_iota(I32, (n_rep * rows_per_head, 1), 0) // rows_per_head
    col = jnp.zeros((n_rep * rows_per_head, 1), F32)
    for g in range(n_rep):
        col = jnp.where(rid == g, sink_ref[kvh * n_rep + g], col)
    return col


def _swa_prompt_kernel(sink_ref, q_ref, kc_ref, kp_ref, vc_ref, vp_ref, o_ref):
    n = pl.program_id(1)
    w = WINDOW
    n_rep = SWA_HEADS // SWA_KV_HEADS
    hd = SWA_HEAD_DIM
    q = q_ref[...]
    kc = kc_ref[...].astype(BF16)
    kp = kp_ref[...].astype(BF16)
    vc = vc_ref[...].astype(BF16)
    vp = vp_ref[...].astype(BF16)
    t = lax.broadcasted_iota(I32, (n_rep * w, 2 * w), 0) % w
    c = lax.broadcasted_iota(I32, (n_rep * w, 2 * w), 1)
    first_key = jnp.where(n > 0, 0, w)
    valid = (c > t) & (c <= t + w) & (c >= first_key)
    for kvh in range(SWA_KV_HEADS):
        ks = slice(kvh * hd, (kvh + 1) * hd)
        kk = jnp.concatenate([kp[:, ks], kc[:, ks]], axis=0)
        vv = jnp.concatenate([vp[:, ks], vc[:, ks]], axis=0)
        qg = jnp.concatenate([q[:, (kvh * n_rep + g) * hd:(kvh * n_rep + g + 1) * hd] for g in range(n_rep)], axis=0)
        s = jnp.where(valid, _dot_nt(qg, kk), NEG_BIG)
        sink = _sink_column(sink_ref, w, kvh, n_rep)
        m = jnp.maximum(s.max(axis=-1, keepdims=True), sink)
        p = jnp.exp2(s - m)
        denom = p.sum(axis=-1, keepdims=True) + jnp.exp2(sink - m)
        o = _dot(p.astype(BF16), vv) / denom
        for g in range(n_rep):
            a = (kvh * n_rep + g) * hd
            o_ref[:, a:a + hd] = o[g * w:(g + 1) * w].astype(BF16)


def _swa_prompt(sinks2, q, k, v, batch, seq):
    nb = seq // WINDOW
    nq, nkv = SWA_HEADS * SWA_HEAD_DIM, SWA_KV_HEADS * SWA_HEAD_DIM
    cur = lambda b, n: (b * nb + n, 0)
    prev = lambda b, n: (b * nb + jnp.maximum(n - 1, 0), 0)
    return pl.pallas_call(
        _swa_prompt_kernel,
        grid=(batch, nb),
        in_specs=[pl.BlockSpec(memory_space=pltpu.SMEM), pl.BlockSpec((WINDOW, nq), cur),
                  pl.BlockSpec((WINDOW, nkv), cur), pl.BlockSpec((WINDOW, nkv), prev),
                  pl.BlockSpec((WINDOW, nkv), cur), pl.BlockSpec((WINDOW, nkv), prev)],
        out_specs=pl.BlockSpec((WINDOW, nq), cur),
        out_shape=jax.ShapeDtypeStruct((batch * seq, nq), BF16),
        compiler_params=_cparams(("parallel", "parallel")),
        name="swa_prompt",
    )(sinks2, q, k, k, v, v)


def _swa_sample_kernel(sink_ref, qbd_ref, kw_ref, vw_ref, kn_ref, vn_ref, o_ref):
    n_rep = SWA_HEADS // SWA_KV_HEADS
    hd = SWA_HEAD_DIM
    qbd = qbd_ref[0]
    kw = kw_ref[0].astype(BF16)
    vw = vw_ref[0].astype(BF16)
    kn = kn_ref[0].astype(BF16).astype(F32)
    vn = vn_ref[0].astype(BF16).astype(F32)
    s = _dot_nt(qbd, kw)
    col = lax.broadcasted_iota(I32, s.shape, 1)
    s = jnp.where(col >= 1, s, NEG_BIG)
    s_new = jnp.sum(qbd.astype(F32) * kn, axis=-1, keepdims=True)
    sink = _sink_column(sink_ref, 1, 0, SWA_HEADS)
    m = jnp.maximum(jnp.maximum(s.max(axis=-1, keepdims=True), s_new), sink)
    p = jnp.exp2(s - m)
    p_new = jnp.exp2(s_new - m)
    denom = p.sum(axis=-1, keepdims=True) + p_new + jnp.exp2(sink - m)
    full = (_dot(p.astype(BF16), vw) + p_new.astype(BF16).astype(F32) * vn) / denom
    rid = lax.broadcasted_iota(I32, (SWA_HEADS, hd), 0) // n_rep
    o = jnp.zeros((SWA_HEADS, hd), F32)
    for kvh in range(SWA_KV_HEADS):
        o = jnp.where(rid == kvh, full[:, kvh * hd:(kvh + 1) * hd], o)
    o_ref[0] = o.astype(BF16)


def _swa_sample(sinks2, qbd, kwin, vwin, k_new, v_new):
    bd = qbd.shape[0]
    nkv = SWA_KV_HEADS * SWA_HEAD_DIM
    per_b = lambda b: (b, 0, 0)
    return pl.pallas_call(
        _swa_sample_kernel,
        grid=(bd,),
        in_specs=[pl.BlockSpec(memory_space=pltpu.SMEM), pl.BlockSpec((1, SWA_HEADS, nkv), per_b),
                  pl.BlockSpec((1, WINDOW, nkv), per_b), pl.BlockSpec((1, WINDOW, nkv), per_b),
                  pl.BlockSpec((1, 1, nkv), per_b), pl.BlockSpec((1, 1, nkv), per_b)],
        out_specs=pl.BlockSpec((1, SWA_HEADS, SWA_HEAD_DIM), per_b),
        out_shape=jax.ShapeDtypeStruct((bd, SWA_HEADS, SWA_HEAD_DIM), BF16),
        compiler_params=_cparams(("parallel",)),
        name="swa_sample",
    )(sinks2, qbd, kwin, vwin, k_new, v_new)


def _block_diag_q(q, n_heads, n_kv, hd):
    b = q.shape[0]
    q3 = q.reshape(b, n_heads, hd)
    owner = (jnp.arange(n_heads) // (n_heads // n_kv))[:, None] == jnp.arange(n_kv)[None, :]
    return jnp.where(owner[None, :, :, None], q3[:, :, None, :], jnp.zeros((), q.dtype)).reshape(b, n_heads, n_kv * hd)


def _sb_proj_kernel(x_ref, gmix_ref, win_ref, q_ref, k_ref, v_ref, kb_ref, vb_ref, *, qscale):
    nq, nkv = SB_HEADS * SB_HEAD_DIM, SB_KV_HEADS * SB_HEAD_DIM
    h = _rms(x_ref[...], gmix_ref[...]).astype(BF16)
    proj = _dot(h, win_ref[...])
    q_ref[...] = (proj[:, :nq] * qscale).astype(BF16)
    k = proj[:, nq:nq + nkv]
    v = proj[:, nq + nkv:]
    k_ref[...] = k
    v_ref[...] = v
    kb_ref[...] = k.astype(BF16)
    vb_ref[...] = v.astype(BF16)


def _sb_proj(x, g_mix, w_in, tm):
    t, d = x.shape
    nq, nkv = SB_HEADS * SB_HEAD_DIM, SB_KV_HEADS * SB_HEAD_DIM
    row = lambda i: (i, 0)
    return pl.pallas_call(
        functools.partial(_sb_proj_kernel, qscale=SB_HEAD_DIM ** -0.5),
        grid=(t // tm,),
        in_specs=[pl.BlockSpec((tm, d), row), _full_spec((1, d)), _full_spec(w_in.shape)],
        out_specs=[pl.BlockSpec((tm, nq), row)] + [pl.BlockSpec((tm, nkv), row)] * 4,
        out_shape=[jax.ShapeDtypeStruct((t, nq), BF16), jax.ShapeDtypeStruct((t, nkv), F32),
                   jax.ShapeDtypeStruct((t, nkv), F32), jax.ShapeDtypeStruct((t, nkv), BF16),
                   jax.ShapeDtypeStruct((t, nkv), BF16)],
        compiler_params=_cparams(("parallel",)),
        name="sb_proj",
    )(x, g_mix.reshape(1, d), w_in.astype(BF16))


def _suffix_matrix():
    r = jnp.arange(SB_BLOCK)
    tri = (r[:, None] > r[None, :]).astype(BF16)
    return jnp.concatenate([tri, jnp.ones((SB_BLOCK, SB_BLOCK), BF16)], axis=1)


def _sb_weights(z, later, tri_ones, strict):
    m = z.shape[0]
    log_1mb = -(jnp.maximum(z, 0.0) + jnp.log(1.0 + jnp.exp(-jnp.abs(z))))
    if strict is not None:
        log_1mb = jnp.where(strict, log_1mb, 0.0)
    hi = log_1mb.astype(BF16)
    lo = (log_1mb - hi.astype(F32)).astype(BF16)
    sums = _dot(jnp.concatenate([hi, lo], axis=0), tri_ones)
    sums = sums[:m] + sums[m:]
    suffix = sums[:, :SB_BLOCK]
    total = sums[:, SB_BLOCK:]
    a = jnp.exp(z + log_1mb + suffix + later)
    if strict is not None:
        a = jnp.where(strict, a, 0.0)
    return a.astype(BF16), total


def _sb_prompt_kernel(q_ref, k_ref, v_ref, tri_ref, o_ref):
    qi = pl.program_id(1)
    blk = SB_BLOCK
    n_rep = SB_HEADS // SB_KV_HEADS
    hd = SB_HEAD_DIM
    q = q_ref[...]
    qs = jnp.concatenate([q[:, h * hd:(h + 1) * hd] for h in range(SB_HEADS)], axis=0)
    tri_ones = tri_ref[...]
    rows = SB_HEADS * blk
    grp = n_rep * blk

    def tile(j, later, strict):
        start = pl.multiple_of(j * blk, blk)
        kt = k_ref[pl.ds(start, blk), :]
        vt = v_ref[pl.ds(start, blk), :]
        z = jnp.concatenate([_dot_nt(qs[kvh * grp:(kvh + 1) * grp], kt[:, kvh * hd:(kvh + 1) * hd])
                             for kvh in range(SB_KV_HEADS)], axis=0)
        a, total = _sb_weights(z, later, tri_ones, strict)
        d_out = jnp.concatenate([_dot(a[kvh * grp:(kvh + 1) * grp], vt[:, kvh * hd:(kvh + 1) * hd])
                                 for kvh in range(SB_KV_HEADS)], axis=0)
        return d_out, total

    t = lax.broadcasted_iota(I32, (rows, blk), 0) % blk
    c = lax.broadcasted_iota(I32, (rows, blk), 1)
    out, later = tile(qi, jnp.zeros((rows, blk), F32), c < t)

    def cond(carry):
        j, alive, _, _ = carry
        return (j >= 0) & (alive > SB_DEAD_LOG)

    def body(carry):
        j, _, later, out = carry
        d_out, total = tile(j, later, None)
        later = later + total
        return j - 1, jnp.max(later), later, out + d_out

    _, _, _, out = lax.while_loop(cond, body, (qi - 1, jnp.max(later), later, out))
    o_ref[...] = jnp.concatenate([out[h * blk:(h + 1) * blk] for h in range(SB_HEADS)], axis=1).astype(BF16)


def _sb_prompt(q, kb, vb, batch, seq):
    nq = seq // SB_BLOCK
    nqd, nkv = SB_HEADS * SB_HEAD_DIM, SB_KV_HEADS * SB_HEAD_DIM
    qmap = lambda b, i: (b * nq + i, 0)
    return pl.pallas_call(
        _sb_prompt_kernel,
        grid=(batch, nq),
        in_specs=[pl.BlockSpec((SB_BLOCK, nqd), qmap), pl.BlockSpec((seq, nkv), lambda b, i: (b, 0)),
                  pl.BlockSpec((seq, nkv), lambda b, i: (b, 0)), _full_spec((SB_BLOCK, 2 * SB_BLOCK))],
        out_specs=pl.BlockSpec((SB_BLOCK, nqd), qmap),
        out_shape=jax.ShapeDtypeStruct((batch * seq, nqd), BF16),
        compiler_params=_cparams(("parallel", "arbitrary")),
        name="sb_prompt",
    )(q, kb, vb, _suffix_matrix())


def _sb_paged_kernel(pt_ref, qbd_ref, tri_ref, k_hbm, v_hbm, o_ref, kbuf, vbuf, sem, *, layer, n_pages):
    b = pl.program_id(0)
    n_rep = SB_HEADS // SB_KV_HEADS
    hd = SB_HEAD_DIM
    base = (b % 2) * 2

    def copies(row, step, slot):
        page = pt_ref[row, n_pages - 1 - step]
        cps = []
        for kvh in range(SB_KV_HEADS):
            cps.append(pltpu.make_async_copy(k_hbm.at[layer, page, :, kvh, :], kbuf.at[slot, kvh], sem.at[0, slot]))
            cps.append(pltpu.make_async_copy(v_hbm.at[layer, page, :, kvh, :], vbuf.at[slot, kvh], sem.at[1, slot]))
        return cps

    def start(row, step, slot):
        for cp in copies(row, step, slot):
            cp.start()

    def wait(row, step, slot):
        for cp in copies(row, step, slot):
            cp.wait()

    @pl.when(b == 0)
    def _():
        start(0, 0, 0)

    @pl.when(b + 1 < pl.num_programs(0))
    def _():
        start(b + 1, 0, 2 - base)

    qbd = qbd_ref[0]
    tri_ones = tri_ref[...]

    def cond(carry):
        i, alive, _, _ = carry
        return (i < n_pages) & (alive > SB_DEAD_LOG)

    def body(carry):
        i, _, later, acc = carry
        slot = base + i % 2

        @pl.when(i + 1 < n_pages)
        def _():
            start(b, i + 1, base + (i + 1) % 2)

        wait(b, i, slot)
        z = _dot_nt(qbd[:, :hd], kbuf[slot, 0].astype(BF16))
        for kvh in range(1, SB_KV_HEADS):
            z = z + _dot_nt(qbd[:, kvh * hd:(kvh + 1) * hd], kbuf[slot, kvh].astype(BF16))
        a, total = _sb_weights(z, later, tri_ones, None)
        d_acc = jnp.concatenate([_dot(a, vbuf[slot, kvh].astype(BF16)) for kvh in range(SB_KV_HEADS)], axis=1)
        later = later + total
        return i + 1, jnp.max(later), later, acc + d_acc

    init = (jnp.int32(0), jnp.float32(0.0), jnp.zeros((SB_HEADS, SB_BLOCK), F32),
            jnp.zeros((SB_HEADS, SB_KV_HEADS * hd), F32))
    n_done, _, _, acc = lax.while_loop(cond, body, init)

    @pl.when(n_done < n_pages)
    def _():
        wait(b, n_done, base + n_done % 2)

    rid = lax.broadcasted_iota(I32, (SB_HEADS, hd), 0) // n_rep
    o = jnp.zeros((SB_HEADS, hd), F32)
    for kvh in range(SB_KV_HEADS):
        o = jnp.where(rid == kvh, acc[:, kvh * hd:(kvh + 1) * hd], o)
    o_ref[0] = o.astype(o_ref.dtype)


def _sb_paged(page_table, qbd, k_pool, v_pool, layer):
    bd, n_pages = page_table.shape
    nkv = SB_KV_HEADS * SB_HEAD_DIM
    page_shape = (4, SB_KV_HEADS, PAGE_SIZE, SB_HEAD_DIM)
    per_b = lambda b, pt: (b, 0, 0)
    return pl.pallas_call(
        functools.partial(_sb_paged_kernel, layer=layer, n_pages=n_pages),
        grid_spec=pltpu.PrefetchScalarGridSpec(
            num_scalar_prefetch=1, grid=(bd,),
            in_specs=[pl.BlockSpec((1, SB_HEADS, nkv), per_b), pl.BlockSpec((SB_BLOCK, 2 * SB_BLOCK), lambda b, pt: (0, 0)),
                      pl.BlockSpec(memory_space=pl.ANY), pl.BlockSpec(memory_space=pl.ANY)],
            out_specs=pl.BlockSpec((1, SB_HEADS, SB_HEAD_DIM), per_b),
            scratch_shapes=[pltpu.VMEM(page_shape, F32), pltpu.VMEM(page_shape, F32), pltpu.SemaphoreType.DMA((2, 4))]),
        out_shape=jax.ShapeDtypeStruct((bd, SB_HEADS, SB_HEAD_DIM), BF16),
        compiler_params=_cparams(("arbitrary",)),
        name="sb_paged",
    )(page_table, qbd, _suffix_matrix(), k_pool, v_pool)


def _lane_min_where(mask, lane, big):
    return jnp.min(jnp.where(mask, lane, big), axis=-1, keepdims=True)


def _oproj_route_kernel(x_ref, o_ref, wo_ref, gffn_ref, wrh_ref, wrl_ref, br_ref, base_ref, tri_ref,
                        xmid_ref, hn_ref, route_ref, gate_ref, cnt_out_ref, cnt_ref):
    @pl.when(pl.program_id(0) == 0)
    def _():
        cnt_ref[...] = base_ref[...]

    x_mid = x_ref[...] + _dot(o_ref[...], wo_ref[...])
    xmid_ref[...] = x_mid
    hn = _rms(x_mid, gffn_ref[...])
    _store_token_tiles(hn_ref, hn)
    hh = hn.astype(BF16)
    hl = (hn - hh.astype(F32)).astype(BF16)
    wrh = wrh_ref[...]
    logits = _dot(hh, wrh) + _dot(hl, wrh) + _dot(hh, wrl_ref[...]) + br_ref[...]
    lane_i = lax.broadcasted_iota(I32, logits.shape, 1)
    lane = lane_i.astype(F32)
    big = float(ROUTER_LANES)
    is_grp = lane < N_GROUPS
    lg = jnp.where(is_grp, logits, NEG_BIG)
    mg = lg.max(axis=-1, keepdims=True)
    grp = _lane_min_where(is_grp & (lg == mg), lane, big)
    p_grp = 1.0 / jnp.sum(jnp.where(is_grp, jnp.exp(lg - mg), 0.0), axis=-1, keepdims=True)
    lo = N_GROUPS + grp * EXPERTS_PER_GROUP
    in_grp = (lane >= lo) & (lane < lo + EXPERTS_PER_GROUP)
    lf = jnp.where(in_grp, logits, NEG_BIG)
    v1 = lf.max(axis=-1, keepdims=True)
    i1 = _lane_min_where(in_grp & (lf == v1), lane, big)
    rest = in_grp & (lane != i1)
    lf2 = jnp.where(rest, logits, NEG_BIG)
    v2 = lf2.max(axis=-1, keepdims=True)
    i2 = _lane_min_where(rest & (lf2 == v2), lane, big)
    e2 = jnp.exp(v2 - v1)
    g1 = p_grp / (1.0 + e2)
    g2 = p_grp * e2 / (1.0 + e2)
    gate_ref[...] = jnp.where(lane_i == 0, g1, jnp.where(lane_i == 1, g2, 0.0))
    ex1 = i1 - N_GROUPS
    ex2 = i2 - N_GROUPS
    oh1 = lane == ex1
    oh2 = lane == ex2
    oh = jnp.where(oh1, 1.0, 0.0) + jnp.where(oh2, 1.0, 0.0)
    before = _dot(tri_ref[...], oh.astype(BF16)) + cnt_ref[...]
    r1 = jnp.sum(jnp.where(oh1, before, 0.0), axis=-1, keepdims=True)
    r2 = jnp.sum(jnp.where(oh2, before, 0.0), axis=-1, keepdims=True)
    cnt = cnt_ref[...] + jnp.sum(oh, axis=0, keepdims=True)
    cnt_ref[...] = cnt
    cnt_out_ref[...] = cnt
    route_ref[...] = jnp.where(lane_i == 0, ex1, jnp.where(lane_i == 1, ex2, jnp.where(
        lane_i == 2, r1, jnp.where(lane_i == 3, r2, 0.0)))).astype(I32)


def _router_weights(w_group, b_group, w_fine, b_fine):
    d = w_group.shape[0]
    pad = ROUTER_LANES - N_GROUPS - N_EXPERTS
    w = jnp.concatenate([w_group, w_fine, jnp.zeros((d, pad), F32)], axis=1)
    b = jnp.concatenate([b_group, b_fine, jnp.zeros((pad,), F32)]).reshape(1, ROUTER_LANES)
    w_hi = w.astype(BF16)
    w_lo = (w - w_hi.astype(F32)).astype(BF16)
    return w_hi, w_lo, b


def _oproj_route(x, o, w_o, g_ffn, router, base_counts, tm):
    t, d = x.shape
    kd = o.shape[1]
    w_hi, w_lo, b = router
    row = lambda i: (i, 0)
    r = jnp.arange(tm)
    tri = (r[None, :] < r[:, None]).astype(BF16)
    lanes = ROUTER_LANES
    return pl.pallas_call(
        _oproj_route_kernel,
        grid=(t // tm,),
        in_specs=[pl.BlockSpec((tm, d), row), pl.BlockSpec((tm, kd), row), _full_spec((kd, d)), _full_spec((1, d)),
                  _full_spec(w_hi.shape), _full_spec(w_lo.shape), _full_spec(b.shape), _full_spec((1, lanes)),
                  _full_spec((tm, tm))],
        out_specs=[pl.BlockSpec((tm, d), row), pl.BlockSpec((tm * ROW_CHUNKS, d // ROW_CHUNKS), row),
                   pl.BlockSpec((tm, lanes), row), pl.BlockSpec((tm, lanes), row), _full_spec((1, lanes))],
        out_shape=[jax.ShapeDtypeStruct((t, d), F32), jax.ShapeDtypeStruct((t * ROW_CHUNKS, d // ROW_CHUNKS), F32),
                   jax.ShapeDtypeStruct((t, lanes), I32), jax.ShapeDtypeStruct((t, lanes), F32),
                   jax.ShapeDtypeStruct((1, lanes), F32)],
        scratch_shapes=[pltpu.VMEM((1, lanes), F32)],
        compiler_params=_cparams(("arbitrary",)),
        name="oproj_route",
    )(x, o, w_o, g_ffn.reshape(1, d), w_hi, w_lo, b, base_counts, tri)


def _moe_kernel(be_ref, tok_ref, dst_ref, nact_ref, hn_hbm, wg_ref, wu_ref, wd_ref, y_hbm,
                xbuf, obuf, wgb, wub, wdb, gsem, ssem, *, t_all, t_pad):
    i = pl.program_id(0)
    nb = pl.num_programs(0)
    blk = MOE_SLOT_BLOCK
    rc = ROW_CHUNKS
    n_act = nact_ref[0]
    slot = i % 2

    def gather_start(block, buf):
        for r in range(blk):
            src = pl.ds(pl.multiple_of(tok_ref[block * blk + r], rc), rc)
            pltpu.make_async_copy(hn_hbm.at[src], xbuf.at[buf, pl.ds(r * rc, rc)], gsem.at[buf]).start()

    def gather_wait(buf):
        pltpu.make_async_copy(hn_hbm.at[pl.ds(0, blk * rc)], xbuf.at[buf], gsem.at[buf]).wait()

    def scatter_start(block, buf):
        for r in range(blk):
            dst = pl.ds(pl.multiple_of(dst_ref[block * blk + r], rc), rc)
            pltpu.make_async_copy(obuf.at[buf, pl.ds(r * rc, rc)], y_hbm.at[dst], ssem.at[buf]).start(priority=r % 2)

    def scatter_wait(buf):
        pltpu.make_async_copy(obuf.at[buf], y_hbm.at[pl.ds(0, blk * rc)], ssem.at[buf]).wait()

    def compute(buf):
        xb = _load_token_tiles(xbuf, blk, lead=(buf,)).astype(BF16)
        gt = _dot(xb, wgb[...])
        up = _dot(xb, wub[...])
        hmid = (gt * (1.0 / (1.0 + jnp.exp(-gt))) * up).astype(BF16)
        _store_token_tiles(obuf, _dot(hmid, wdb[...]), lead=(buf,))

    @pl.when(i == 0)
    def _():
        obuf[1] = jnp.zeros(obuf.shape[1:], F32)
        gap = t_pad - t_all
        fills = [(2 * t_pad, blk), (2 * t_pad + blk, blk)] + ([(t_all, gap), (t_pad + t_all, gap)] if gap else [])
        cps = [pltpu.make_async_copy(obuf.at[1, pl.ds(0, n * rc)], y_hbm.at[pl.ds(r0 * rc, n * rc)], ssem.at[1])
               for r0, n in fills]
        for cp in cps:
            cp.start()
        for cp in cps:
            cp.wait()

    @pl.when((i == 0) & (n_act > 0))
    def _():
        gather_start(0, 0)

    @pl.when((i >= 2) & (i - 2 < n_act))
    def _():
        scatter_wait(slot)

    @pl.when(i < n_act)
    def _():
        @pl.when((i == 0) | (be_ref[i] != be_ref[jnp.maximum(i - 1, 0)]))
        def _():
            wgb[...] = wg_ref[0, 0].astype(BF16)
            wub[...] = wu_ref[0, 0].astype(BF16)
            wdb[...] = wd_ref[0, 0].astype(BF16)

        gather_wait(slot)

    steady = (i >= 1) & (i + 1 < n_act)

    @pl.when(steady)
    def _():
        gather_start(i + 1, 1 - slot)
        scatter_start(i - 1, 1 - slot)
        compute(slot)

    @pl.when((i < n_act) & jnp.logical_not(steady))
    def _():
        @pl.when(i + 1 < n_act)
        def _():
            gather_start(i + 1, 1 - slot)

        @pl.when(i >= 1)
        def _():
            scatter_start(i - 1, 1 - slot)

        compute(slot)

    @pl.when((i == n_act) & (i >= 1))
    def _():
        scatter_start(i - 1, 1 - slot)

    @pl.when((i == nb - 1) & (i - 1 < n_act))
    def _():
        scatter_wait(1 - slot)


def _moe_experts(block_e, slot_tok, slot_dst, n_act, hn, w_gate, w_up, w_down, layer, t_pad):
    n_slots = slot_tok.shape[0]
    blk = MOE_SLOT_BLOCK
    n_blocks = n_slots // blk
    rc, lanes = ROW_CHUNKS, hn.shape[1]
    t_all = hn.shape[0] // rc
    d = rc * lanes
    de = w_gate.shape[3]
    assert 0 <= t_pad - t_all <= blk
    wmap = lambda i, be, tok, dst, na: (layer, be[i], 0, 0)
    return pl.pallas_call(
        functools.partial(_moe_kernel, t_all=t_all, t_pad=t_pad),
        grid_spec=pltpu.PrefetchScalarGridSpec(
            num_scalar_prefetch=4, grid=(n_blocks,),
            in_specs=[pl.BlockSpec(memory_space=pl.ANY), pl.BlockSpec((1, 1, d, de), wmap),
                      pl.BlockSpec((1, 1, d, de), wmap), pl.BlockSpec((1, 1, de, d), wmap)],
            out_specs=pl.BlockSpec(memory_space=pl.ANY),
            scratch_shapes=[pltpu.VMEM((2, blk * rc, lanes), F32), pltpu.VMEM((2, blk * rc, lanes), F32),
                            pltpu.VMEM((d, de), BF16), pltpu.VMEM((d, de), BF16), pltpu.VMEM((de, d), BF16),
                            pltpu.SemaphoreType.DMA((2,)), pltpu.SemaphoreType.DMA((2,))]),
        out_shape=jax.ShapeDtypeStruct(((2 * t_pad + 2 * blk) * rc, lanes), F32),
        compiler_params=_cparams(("arbitrary",)),
        name="moe_experts",
    )(block_e, slot_tok, slot_dst, n_act, hn, w_gate, w_up, w_down)


def _moe_slots(route, counts, t_pad):
    n_tok = route.shape[0]
    blk = MOE_SLOT_BLOCK
    n_asg = 2 * n_tok
    experts = jnp.arange(N_EXPERTS, dtype=I32)
    padded = (counts + blk - 1) // blk * blk
    pad_end = jnp.cumsum(padded)
    pad_start = pad_end - padded
    e = route[:, :2]
    rank = route[:, 2:4]
    dest = jnp.sum(jnp.where(e[..., None] == experts, pad_start, 0), axis=-1) + rank
    n_blocks = -(-n_asg // blk) + N_EXPERTS + 1
    slot_asg = jnp.full((n_blocks * blk,), -1, I32).at[dest.reshape(n_asg)].set(jnp.arange(n_asg, dtype=I32))
    slot = jnp.arange(n_blocks * blk, dtype=I32)
    slot_tok = (jnp.maximum(slot_asg, 0) >> 1) * ROW_CHUNKS
    spare = 2 * t_pad + ((slot // blk) % 2) * blk + slot % blk
    slot_dst = jnp.where(slot_asg >= 0, (slot_asg & 1) * t_pad + (slot_asg >> 1), spare) * ROW_CHUNKS
    n_act = (pad_end[-1] // blk).astype(I32)
    blk_start = jnp.arange(n_blocks, dtype=I32) * blk
    block_e = jnp.minimum(jnp.sum((pad_end[None, :] <= blk_start[:, None]).astype(I32), axis=1), N_EXPERTS - 1)
    last_e = jnp.sum(jnp.where(jnp.arange(n_blocks) == jnp.maximum(n_act - 1, 0), block_e, 0))
    block_e = jnp.where(jnp.arange(n_blocks) < n_act, block_e, last_e).astype(I32)
    return block_e, slot_tok, slot_dst, n_act.reshape(1)


def _combine_ple_kernel(x_ref, ya_ref, yb_ref, mg_ref, p_ref, g_ref, wg_ref, wp_ref, gfin_ref, o_ref, *, final):
    mg = mg_ref[...]
    n = x_ref.shape[0]
    x1 = x_ref[...] + (mg[:, 0:1] * _load_token_tiles(ya_ref, n) + mg[:, 1:2] * _load_token_tiles(yb_ref, n))
    gate = _dot(_rms(x1, g_ref[...]).astype(BF16), wg_ref[...])
    gate = 1.0 / (1.0 + jnp.exp(-gate))
    out = x1 + gate * _dot(p_ref[...].astype(BF16), wp_ref[...])
    if final:
        out = _rms(out, gfin_ref[...])
    o_ref[...] = out


def _combine_ple(x_mid, y2, row0, t_pad, moe_gate, p, g_ple, w_gate, w_proj, g_final, final, tm):
    t, d = x_mid.shape
    pd = p.shape[1]
    assert row0 % tm == 0 and t_pad % tm == 0
    row = lambda i: (i, 0)
    tile = (tm * ROW_CHUNKS, d // ROW_CHUNKS)
    return pl.pallas_call(
        functools.partial(_combine_ple_kernel, final=final),
        grid=(t // tm,),
        in_specs=[pl.BlockSpec((tm, d), row), pl.BlockSpec(tile, lambda i: (row0 // tm + i, 0)),
                  pl.BlockSpec(tile, lambda i: ((t_pad + row0) // tm + i, 0)), pl.BlockSpec((tm, ROUTER_LANES), row),
                  pl.BlockSpec((tm, pd), row), _full_spec((1, d)), _full_spec((d, d)), _full_spec((pd, d)),
                  _full_spec((1, d))],
        out_specs=pl.BlockSpec((tm, d), row),
        out_shape=jax.ShapeDtypeStruct((t, d), F32),
        compiler_params=_cparams(("parallel",)),
        name="combine_ple",
    )(x_mid, y2, y2, moe_gate, p, g_ple.reshape(1, d), w_gate, w_proj, g_final.reshape(1, d))


def kernel(x_prompt, x_sample, cache_mla_ckv, cache_mla_kpe, cache_swa_k, cache_swa_v, cache_sb_k, cache_sb_v, page_table, p_prompt, p_sample, g_mix, g_ffn, g_ple, g_final, mla_w_in, mla_g_q, mla_g_kv, mla_w_uq, mla_w_uk, mla_w_uv, mla_w_o, swa_w_in, swa_sinks, swa_w_o, sb_w_in, sb_w_o, moe_w_group, moe_b_group, moe_w_fine, moe_b_fine, moe_w_gate, moe_w_up, moe_w_down, ple_w_gate, ple_w_proj):
    batch, seq, d = x_prompt.shape
    bd, dec_seq, _ = x_sample.shape
    assert dec_seq == 1
    depth = g_mix.shape[0]
    n_pages = page_table.shape[1]
    past_len = n_pages * PAGE_SIZE
    tp = batch * seq
    tm_p = _row_tile(tp, 256)
    tm_s = bd
    assert tp % bd == 0 and seq % tm_p == 0
    t_pad = -(-(tp + bd) // tm_p) * tm_p
    page_table = page_table.astype(I32)
    kpe_pool_t = cache_mla_kpe.transpose(0, 1, 3, 2)

    xp = x_prompt.reshape(tp, d)
    xs = x_sample.reshape(bd, d)
    pos_p = jnp.arange(seq)
    pos_s = jnp.full((bd,), past_len)
    rope_p = _rope_tables(pos_p, 32, 128)
    rope_s = _rope_tables(pos_s, 32, 128)

    outs = {k: [] for k in ("p_ckv", "p_kpe", "s_ckv", "s_kpe", "p_wk", "p_wv", "s_wk", "s_wv", "p_sk", "p_sv", "s_sk", "s_sv")}
    for i in range(depth):
        kind, j = i % 3, i // 3
        if kind == 0:
            weights = _mla_weights(mla_w_in[j], mla_w_uq[j], mla_w_uk[j], mla_w_uv[j])
            q_p, kk_p, v_p, ckv_p, kpe_p = _mla_proj(xp, g_mix[i], mla_g_q[j], mla_g_kv[j], weights, *rope_p, seq // tm_p, tm_p)
            q_s, _, _, ckv_s, kpe_s = _mla_proj(xs, g_mix[i], mla_g_q[j], mla_g_kv[j], weights, *rope_s, 1, tm_s)
            o_p = _mla_flash(q_p, kk_p, v_p, batch, seq, _row_tile(seq, FLASH_Q_TILE), _row_tile(seq, FLASH_K_TILE))
            q3 = q_s.reshape(bd, MLA_HEADS, MLA_QK_PAD)
            w_uk_t = mla_w_uk[j].transpose(1, 2, 0).astype(BF16)
            qlat = _bmm(q3[:, :, 128:].transpose(1, 0, 2), w_uk_t, BF16).transpose(1, 0, 2)
            qx = jnp.concatenate([qlat, q3[:, :, :128]], axis=-1)
            o_lat = _mla_paged(page_table, qx, ckv_s.reshape(bd, 1, -1), kpe_s.reshape(bd, 1, -1),
                               cache_mla_ckv, kpe_pool_t, j)
            w_uv_h = mla_w_uv[j].transpose(1, 0, 2).astype(BF16)
            o_s = _bmm(o_lat.transpose(1, 0, 2), w_uv_h, BF16).transpose(1, 0, 2).reshape(bd, MLA_HEADS * MLA_V)
            w_o = mla_w_o[j]
            outs["p_ckv"].append(ckv_p.reshape(batch, seq, -1)); outs["p_kpe"].append(kpe_p.reshape(batch, seq, -1))
            outs["s_ckv"].append(ckv_s.reshape(bd, 1, -1)); outs["s_kpe"].append(kpe_s.reshape(bd, 1, -1))
        elif kind == 1:
            sinks2 = swa_sinks[j] * LOG2E
            q_p, k_p, v_p = _swa_proj(xp, g_mix[i], swa_w_in[j], *rope_p, seq // tm_p, tm_p)
            q_s, k_s, v_s = _swa_proj(xs, g_mix[i], swa_w_in[j], *rope_s, 1, tm_s)
            o_p = _swa_prompt(sinks2, q_p, k_p, v_p, batch, seq)
            nkv = SWA_KV_HEADS * SWA_HEAD_DIM
            kwin = cache_swa_k[j].reshape(bd, WINDOW, nkv)
            vwin = cache_swa_v[j].reshape(bd, WINDOW, nkv)
            qbd = _block_diag_q(q_s, SWA_HEADS, SWA_KV_HEADS, SWA_HEAD_DIM)
            o_s = _swa_sample(sinks2, qbd, kwin, vwin, k_s.reshape(bd, 1, nkv), v_s.reshape(bd, 1, nkv))
            o_s = o_s.reshape(bd, SWA_HEADS * SWA_HEAD_DIM)
            w_o = swa_w_o[j]
            k4 = k_p.reshape(batch, seq, SWA_KV_HEADS, SWA_HEAD_DIM)
            v4 = v_p.reshape(batch, seq, SWA_KV_HEADS, SWA_HEAD_DIM)
            outs["p_wk"].append(k4[:, seq - WINDOW:]); outs["p_wv"].append(v4[:, seq - WINDOW:])
            outs["s_wk"].append(jnp.concatenate([cache_swa_k[j][:, 1:], k_s.reshape(bd, 1, SWA_KV_HEADS, SWA_HEAD_DIM)], axis=1))
            outs["s_wv"].append(jnp.concatenate([cache_swa_v[j][:, 1:], v_s.reshape(bd, 1, SWA_KV_HEADS, SWA_HEAD_DIM)], axis=1))
        else:
            q_p, k_p, v_p, kb_p, vb_p = _sb_proj(xp, g_mix[i], sb_w_in[j], tm_p)
            q_s, k_s, v_s, _, _ = _sb_proj(xs, g_mix[i], sb_w_in[j], tm_s)
            o_p = _sb_prompt(q_p, kb_p, vb_p, batch, seq)
            qbd = _block_diag_q(q_s, SB_HEADS, SB_KV_HEADS, SB_HEAD_DIM)
            o_s = _sb_paged(page_table, qbd, cache_sb_k, cache_sb_v, j).reshape(bd, SB_HEADS * SB_HEAD_DIM)
            w_o = sb_w_o[j]
            outs["p_sk"].append(k_p.reshape(batch, seq, SB_KV_HEADS, SB_HEAD_DIM))
            outs["p_sv"].append(v_p.reshape(batch, seq, SB_KV_HEADS, SB_HEAD_DIM))
            outs["s_sk"].append(k_s.reshape(bd, 1, SB_KV_HEADS, SB_HEAD_DIM))
            outs["s_sv"].append(v_s.reshape(bd, 1, SB_KV_HEADS, SB_HEAD_DIM))

        w_o = w_o.astype(BF16)
        router = _router_weights(moe_w_group[i], moe_b_group[i], moe_w_fine[i], moe_b_fine[i])
        zero_counts = jnp.zeros((1, ROUTER_LANES), F32)
        xmid_p, hn_p, route_p, gate_p, cnt_p = _oproj_route(xp, o_p, w_o, g_ffn[i], router, zero_counts, tm_p)
        xmid_s, hn_s, route_s, gate_s, cnt = _oproj_route(xs, o_s, w_o, g_ffn[i], router, cnt_p, tm_s)
        hn = jnp.concatenate([hn_p, hn_s], axis=0)
        route = jnp.concatenate([route_p[:, :4], route_s[:, :4]], axis=0)
        block_e, slot_tok, slot_dst, n_act = _moe_slots(route, cnt[0, :N_EXPERTS].astype(I32), t_pad)
        y2 = _moe_experts(block_e, slot_tok, slot_dst, n_act, hn, moe_w_gate, moe_w_up, moe_w_down, i, t_pad)
        wg = ple_w_gate[i].astype(BF16)
        wp = ple_w_proj[i].astype(BF16)
        final = i == depth - 1
        xp = _combine_ple(xmid_p, y2, 0, t_pad, gate_p, p_prompt[i].reshape(tp, -1), g_ple[i], wg, wp, g_final, final, tm_p)
        xs = _combine_ple(xmid_s, y2, tp, t_pad, gate_s, p_sample[i].reshape(bd, -1), g_ple[i], wg, wp, g_final, final, tm_s)

    st = lambda k: jnp.stack(outs[k])
    return (xp.reshape(batch, seq, d), xs.reshape(bd, 1, d),
            st("p_ckv"), st("p_kpe"), st("p_wk"), st("p_wv"), st("p_sk"), st("p_sv"),
            st("s_ckv"), st("s_kpe"), st("s_wk"), st("s_wv"), st("s_sk"), st("s_sv"))
```
